```python
import jax, jax.numpy as jnp
from jax import lax
import numpy as np

D_MODEL = 2048
BATCH = 2
SEQ = 16384
DEPTH = 2
DEC_BATCH = 8
DEC_SEQ = 4096
PAST_LEN = 128

GRID_W = 64

POOL_WIDTH = D_MODEL // 4
POOL_WINDOWS = (2, 4, 8, 16)
POOL_GROUPS = len(POOL_WINDOWS)
POOL_GC = POOL_WIDTH // POOL_GROUPS

HEAD_DIM = 128
N_Q_HEADS = (D_MODEL // 2) // HEAD_DIM
N_KV_HEADS = 2
GQA_GROUP = N_Q_HEADS // N_KV_HEADS
ATTN_WIDTH = N_Q_HEADS * HEAD_DIM
KV_WIDTH = N_KV_HEADS * HEAD_DIM
Q_BLOCK = 128
ROPE_THETA = 10000.0
ROPE_HALF = HEAD_DIM // 2
ROPE_FREQS_PER_AXIS = HEAD_DIM // 4

FOURIER_WIDTH = D_MODEL - POOL_WIDTH - ATTN_WIDTH
FOURIER_GROUPS = 4
FOURIER_GC = FOURIER_WIDTH // FOURIER_GROUPS

IN_WIDTH = POOL_WIDTH + ATTN_WIDTH + 2 * KV_WIDTH + FOURIER_WIDTH
IN_SPLITS = (POOL_WIDTH,
             POOL_WIDTH + ATTN_WIDTH,
             POOL_WIDTH + ATTN_WIDTH + KV_WIDTH,
             POOL_WIDTH + ATTN_WIDTH + 2 * KV_WIDTH)
MIX_WIDTH = POOL_WIDTH + ATTN_WIDTH + FOURIER_WIDTH

D_FF = 11 * D_MODEL // 4
CONV_WIDTH = 3

NORM_EPS = 1e-6

kernel_name = 'hybrid_pool_attn_fourier_encoder'


def rms_norm(x, g):
    xf = x.astype(jnp.float32)
    y = xf * lax.rsqrt(jnp.mean(xf * xf, axis=-1, keepdims=True) + NORM_EPS)
    return (y * g.astype(jnp.float32)).astype(x.dtype)


def axial_rope_tables(seq_len):
    rows = seq_len // GRID_W
    row_idx = jnp.repeat(jnp.arange(rows, dtype=jnp.float32), GRID_W)
    col_idx = jnp.tile(jnp.arange(GRID_W, dtype=jnp.float32), rows)
    inv_freq = 1.0 / (ROPE_THETA ** (jnp.arange(ROPE_FREQS_PER_AXIS, dtype=jnp.float32) / ROPE_FREQS_PER_AXIS))
    ang = jnp.concatenate([row_idx[:, None] * inv_freq[None, :],
                           col_idx[:, None] * inv_freq[None, :]], axis=-1)
    return jnp.cos(ang), jnp.sin(ang)


def apply_rope(x, cos, sin):
    extra = x.ndim - 3
    shape = (1, cos.shape[0]) + (1,) * extra + (cos.shape[1],)
    c = cos.reshape(shape)
    s = sin.reshape(shape)
    xf = x.astype(jnp.float32)
    x1, x2 = xf[..., :ROPE_HALF], xf[..., ROPE_HALF:]
    return jnp.concatenate([x1 * c - x2 * s, x2 * c + x1 * s], axis=-1).astype(x.dtype)


def multiscale_pool(u, pool_w, pool_scale):
    B, L, _ = u.shape
    uf = u.astype(jnp.float32)
    csum = jnp.concatenate([jnp.zeros((B, 1, POOL_WIDTH), jnp.float32),
                            jnp.cumsum(uf, axis=1)], axis=1)
    t = np.arange(L)
    outs = []
    for g, w in enumerate(POOL_WINDOWS):
        lo = np.clip(t - w // 2, 0, L - 1)
        hi = np.clip(t + w - 1 - w // 2, 0, L - 1)
        cnt = jnp.asarray((hi - lo + 1).astype(np.float32))[None, :, None]
        sl = slice(g * POOL_GC, (g + 1) * POOL_GC)
        cg = csum[..., sl]
        win_sum = jnp.take(cg, jnp.asarray(hi + 1), axis=1) - jnp.take(cg, jnp.asarray(lo), axis=1)
        outs.append(win_sum / cnt - uf[..., sl])
    d = jnp.stack(outs, axis=2).astype(u.dtype)
    y = jnp.einsum('blgc,gce->blge', d, pool_w).reshape(B, L, POOL_WIDTH)
    return y * pool_scale


def gqa_attention(q, k, v):
    B, L = q.shape[0], q.shape[1]
    n_blk = L // Q_BLOCK
    scale = HEAD_DIM ** -0.5
    qb = q.reshape(B, n_blk, Q_BLOCK, N_KV_HEADS, GQA_GROUP, HEAD_DIM).transpose(1, 0, 2, 3, 4, 5)

    def block(q_blk):
        s = jnp.einsum('bqkgd,bskd->bkgqs', q_blk, k,
                       preferred_element_type=jnp.float32) * scale
        p = jax.nn.softmax(s, axis=-1)
        return jnp.einsum('bkgqs,bskd->bqkgd', p.astype(v.dtype), v)

    o = lax.map(block, qb)
    return o.transpose(1, 0, 2, 3, 4, 5).reshape(B, L, ATTN_WIDTH)


def fourier_mix(u, fourier_w):
    B, L, _ = u.shape
    ug = u.astype(jnp.float32).reshape(B, L, FOURIER_GROUPS, FOURIER_GC)
    f = jnp.fft.fft2(ug, axes=(1, 3), norm='ortho').real.astype(u.dtype)
    return jnp.einsum('blgc,gce->blge', f, fourier_w).reshape(B, L, FOURIER_WIDTH)


def token_mixer(h, w_in, pool_w, pool_scale, q_norm, k_norm, fourier_w, w_out):
    B, L, _ = h.shape
    z = h @ w_in
    u_pool, q, k, v, u_four = jnp.split(z, IN_SPLITS, axis=-1)
    cos, sin = axial_rope_tables(L)
    q = apply_rope(rms_norm(q.reshape(B, L, N_KV_HEADS, GQA_GROUP, HEAD_DIM), q_norm), cos, sin)
    k = apply_rope(rms_norm(k.reshape(B, L, N_KV_HEADS, HEAD_DIM), k_norm), cos, sin)
    v = v.reshape(B, L, N_KV_HEADS, HEAD_DIM)
    heads = jnp.concatenate([multiscale_pool(u_pool, pool_w, pool_scale),
                             gqa_attention(q, k, v),
                             fourier_mix(u_four, fourier_w)], axis=-1)
    return heads @ w_out


def conv_gated_mlp(h, w_up, conv_w, conv_b, w_down):
    u = h @ w_up
    up = jnp.pad(u, ((0, 0), (1, 1), (0, 0)))
    c = up[:, :-2] * conv_w[0] + up[:, 1:-1] * conv_w[1] + up[:, 2:] * conv_w[2] + conv_b
    gate, val = jnp.split(c, 2, axis=-1)
    return (jax.nn.gelu(gate, approximate=True) * val) @ w_down


def run_trunk(x, g_pre_mix, g_post_mix, w_in, pool_w, pool_scale, q_norm, k_norm,
              fourier_w, w_out, g_pre_ffn, g_post_ffn, w_up, conv_w, conv_b, w_down):
    for l in range(DEPTH):
        m = token_mixer(rms_norm(x, g_pre_mix[l]), w_in[l], pool_w[l], pool_scale[l],
                        q_norm[l], k_norm[l], fourier_w[l], w_out[l])
        x = x + rms_norm(m, g_post_mix[l])
        f = conv_gated_mlp(rms_norm(x, g_pre_ffn[l]), w_up[l], conv_w[l], conv_b[l], w_down[l])
        x = x + rms_norm(f, g_post_ffn[l])
    return x


def setup_inputs(seed: int = 0) -> dict:
    key = jax.random.key(seed)
    ks = jax.random.split(key, 20)
    f32 = jnp.float32

    def nrm(k, shape, scale):
        return jax.random.normal(k, shape, f32) * scale

    def gain(k, shape):
        return 1.0 + 0.05 * jax.random.normal(k, shape, f32)

    return {
        'x_prompt': nrm(ks[0], (BATCH, SEQ, D_MODEL), 1.0),
        'x_sample': nrm(ks[1], (DEC_BATCH, DEC_SEQ, D_MODEL), 1.0),
        'g_pre_mix': gain(ks[2], (DEPTH, D_MODEL)),
        'g_post_mix': gain(ks[3], (DEPTH, D_MODEL)),
        'w_in': nrm(ks[4], (DEPTH, D_MODEL, IN_WIDTH), D_MODEL ** -0.5),
        'pool_w': nrm(ks[5], (DEPTH, POOL_GROUPS, POOL_GC, POOL_GC), POOL_GC ** -0.5),
        'pool_scale': gain(ks[6], (DEPTH, POOL_WIDTH)),
        'q_norm': gain(ks[7], (DEPTH, HEAD_DIM)),
        'k_norm': gain(ks[8], (DEPTH, HEAD_DIM)),
        'fourier_w': nrm(ks[9], (DEPTH, FOURIER_GROUPS, FOURIER_GC, FOURIER_GC), FOURIER_GC ** -0.5),
        'w_out': nrm(ks[10], (DEPTH, MIX_WIDTH, D_MODEL), MIX_WIDTH ** -0.5),
        'g_pre_ffn': gain(ks[11], (DEPTH, D_MODEL)),
        'g_post_ffn': gain(ks[12], (DEPTH, D_MODEL)),
        'w_up': nrm(ks[13], (DEPTH, D_MODEL, 2 * D_FF), D_MODEL ** -0.5),
        'conv_w': nrm(ks[14], (DEPTH, CONV_WIDTH, 2 * D_FF), CONV_WIDTH ** -0.5),
        'conv_b': nrm(ks[15], (DEPTH, 2 * D_FF), 0.02),
        'w_down': nrm(ks[16], (DEPTH, D_FF, D_MODEL), D_FF ** -0.5),
    }


def reference(x_prompt, x_sample, g_pre_mix, g_post_mix, w_in, pool_w, pool_scale, q_norm,
              k_norm, fourier_w, w_out, g_pre_ffn, g_post_ffn, w_up, conv_w, conv_b, w_down):
    y_prompt = run_trunk(x_prompt, g_pre_mix, g_post_mix, w_in, pool_w, pool_scale, q_norm,
                         k_norm, fourier_w, w_out, g_pre_ffn, g_post_ffn, w_up, conv_w,
                         conv_b, w_down)
    y_sample = run_trunk(x_sample, g_pre_mix, g_post_mix, w_in, pool_w, pool_scale, q_norm,
                         k_norm, fourier_w, w_out, g_pre_ffn, g_post_ffn, w_up, conv_w,
                         conv_b, w_down)
    return (y_prompt, y_sample)
```

```python
import functools
import math

import jax
import jax.numpy as jnp
import numpy as np
from jax import lax
from jax.experimental import pallas as pl
from jax.experimental.pallas import tpu as pltpu

F32 = jnp.float32
BF16 = jnp.bfloat16

NORM_EPS = 1e-6
GRID_W = 64
HEAD_DIM = 128
N_KV_HEADS = 2
GQA_GROUP = 4
POOL_WINDOWS = (2, 4, 8, 16)
POOL_HALO = 8
GROUP_CH = 128
N_GROUPS = 4
ROPE_THETA = 10000.0
CONV_HALO = 16
FOURIER_S2_PER_STEP = 8

V7X_VMEM_BYTES = 64 * 1024 * 1024
VMEM_LIMIT = V7X_VMEM_BYTES - 8 * 1024 * 1024


def _tiles(seq_len):
    return dict(
        tm_proj=min(512, seq_len),
        tm_mix=min(512, seq_len),
        tm_ffn=min(512, seq_len),
        ff_chunk=512,
        tq=min(256, seq_len),
        tk=min(512, seq_len),
    )


def _params(*sem):
    return pltpu.CompilerParams(dimension_semantics=sem, vmem_limit_bytes=VMEM_LIMIT)


def _rms(x, g):
    return x * lax.rsqrt(jnp.mean(x * x, axis=-1, keepdims=True) + NORM_EPS) * g


def _in_proj_kernel(x_ref, g_ref, w_ref, cc_ref, ss_ref, qn_ref, kn_ref,
                    pool_ref, q_ref, k_ref, v_ref, four_ref, *, widths, scale):
    pool_w, q_w, kv_w, four_w = widths
    h = _rms(x_ref[...], g_ref[...]).astype(BF16)
    cc = cc_ref[...]
    ss = ss_ref[...]

    def seg(lo, width):
        return jnp.dot(h, w_ref[:, lo:lo + width], preferred_element_type=F32)

    def norm_rope(zh, gain, out_scale):
        y = zh * lax.rsqrt(jnp.mean(zh * zh, axis=-1, keepdims=True) + NORM_EPS) * gain
        y = y * cc + pltpu.roll(y, HEAD_DIM // 2, axis=1) * ss
        return y * out_scale

    pool_ref[...] = seg(0, pool_w)
    pair = 2 * HEAD_DIM
    for p in range(q_w // pair):
        z = seg(pool_w + p * pair, pair)
        for s in range(2):
            zh = z[:, s * HEAD_DIM:(s + 1) * HEAD_DIM]
            lo = p * pair + s * HEAD_DIM
            q_ref[:, lo:lo + HEAD_DIM] = norm_rope(zh, qn_ref[...], scale).astype(BF16)
    z = seg(pool_w + q_w, kv_w)
    for s in range(kv_w // HEAD_DIM):
        zh = z[:, s * HEAD_DIM:(s + 1) * HEAD_DIM]
        k_ref[:, s * HEAD_DIM:(s + 1) * HEAD_DIM] = norm_rope(zh, kn_ref[...], 1.0).astype(BF16)
    v_ref[...] = seg(pool_w + q_w + kv_w, kv_w).astype(BF16)
    four_ref[...] = seg(pool_w + q_w + 2 * kv_w, four_w).astype(BF16)


def _in_proj(x, g, w, cc, ss, qn, kn, widths, tm):
    B, L, D = x.shape
    pool_w, q_w, kv_w, four_w = widths
    nt = L // tm
    row = lambda b, i: (b, i, 0)
    const2 = lambda b, i: (0, 0)
    kern = functools.partial(_in_proj_kernel, widths=widths, scale=HEAD_DIM ** -0.5)
    return pl.pallas_call(
        kern,
        grid=(B, nt),
        in_specs=[
            pl.BlockSpec((None, tm, D), row),
            pl.BlockSpec((1, D), const2),
            pl.BlockSpec(w.shape, const2),
            pl.BlockSpec((tm, HEAD_DIM), lambda b, i: (i, 0)),
            pl.BlockSpec((tm, HEAD_DIM), lambda b, i: (i, 0)),
            pl.BlockSpec((1, HEAD_DIM), const2),
            pl.BlockSpec((1, HEAD_DIM), const2),
        ],
        out_specs=[
            pl.BlockSpec((None, tm, pool_w), row),
            pl.BlockSpec((None, tm, q_w), row),
            pl.BlockSpec((None, tm, kv_w), row),
            pl.BlockSpec((None, tm, kv_w), row),
            pl.BlockSpec((None, tm, four_w), row),
        ],
        out_shape=[
            jax.ShapeDtypeStruct((B, L, pool_w), F32),
            jax.ShapeDtypeStruct((B, L, q_w), BF16),
            jax.ShapeDtypeStruct((B, L, kv_w), BF16),
            jax.ShapeDtypeStruct((B, L, kv_w), BF16),
            jax.ShapeDtypeStruct((B, L, four_w), BF16),
        ],
        compiler_params=_params("parallel", "parallel"),
        name="in_proj",
    )(x, g, w, cc, ss, qn, kn)


def _attn_kernel(q_ref, k_ref, v_ref, o_ref, qs_ref, m_ref, l_ref, acc_ref, *, tq, tk, nk):
    for g in range(GQA_GROUP):
        qs_ref[g * tq:(g + 1) * tq, :] = q_ref[:, g * HEAD_DIM:(g + 1) * HEAD_DIM]
    m_ref[...] = jnp.full(m_ref.shape, -jnp.inf, F32)
    l_ref[...] = jnp.zeros(l_ref.shape, F32)
    acc_ref[...] = jnp.zeros(acc_ref.shape, F32)

    def body(c, carry):
        off = pl.multiple_of(c * tk, tk)
        kc = k_ref[pl.ds(off, tk), :]
        vc = v_ref[pl.ds(off, tk), :]
        s = lax.dot_general(qs_ref[...], kc, (((1,), (1,)), ((), ())),
                            preferred_element_type=F32)
        m_old = m_ref[...]
        m_new = jnp.maximum(m_old, jnp.max(s, axis=1, keepdims=True))
        alpha = jnp.exp(m_old - m_new)
        p = jnp.exp(s - m_new)
        l_ref[...] = alpha * l_ref[...] + jnp.sum(p, axis=1, keepdims=True)
        acc_ref[...] = alpha * acc_ref[...] + jnp.dot(p.astype(BF16), vc,
                                                      preferred_element_type=F32)
        m_ref[...] = m_new
        return carry

    lax.fori_loop(0, nk, body, 0)
    o = acc_ref[...] / l_ref[...]
    for g in range(GQA_GROUP):
        o_ref[:, g * HEAD_DIM:(g + 1) * HEAD_DIM] = o[g * tq:(g + 1) * tq].astype(o_ref.dtype)


def _attention(q, k, v, tq, tk):
    B, L, q_w = q.shape
    gw = GQA_GROUP * HEAD_DIM
    kern = functools.partial(_attn_kernel, tq=tq, tk=tk, nk=L // tk)
    return pl.pallas_call(
        kern,
        grid=(B, N_KV_HEADS, L // tq),
        in_specs=[
            pl.BlockSpec((None, tq, gw), lambda b, j, i: (b, i, j)),
            pl.BlockSpec((None, L, HEAD_DIM), lambda b, j, i: (b, 0, j)),
            pl.BlockSpec((None, L, HEAD_DIM), lambda b, j, i: (b, 0, j)),
        ],
        out_specs=pl.BlockSpec((None, tq, gw), lambda b, j, i: (b, i, j)),
        out_shape=jax.ShapeDtypeStruct((B, L, q_w), BF16),
        scratch_shapes=[
            pltpu.VMEM((GQA_GROUP * tq, HEAD_DIM), BF16),
            pltpu.VMEM((GQA_GROUP * tq, 1), F32),
            pltpu.VMEM((GQA_GROUP * tq, 1), F32),
            pltpu.VMEM((GQA_GROUP * tq, HEAD_DIM), F32),
        ],
        compiler_params=_params("parallel", "parallel", "arbitrary"),
        name="attention",
    )(q, k, v)


def _dft_cos_sin(n):
    idx = np.arange(n)
    ang = 2.0 * np.pi * ((idx[:, None] * idx[None, :]) % n) / n
    return np.cos(ang), np.sin(ang)


def _fourier_a_kernel(x_ref, f1_ref, twc_ref, tws_ref, zr_ref, zi_ref, *, l2, k, width):
    z = jnp.dot(f1_ref[...], x_ref[...], preferred_element_type=F32)
    zr = z[:l2]
    zi = z[l2:]
    for t in range(k):
        c = twc_ref[:, t:t + 1]
        s = tws_ref[:, t:t + 1]
        a = zr[:, t * width:(t + 1) * width]
        b = zi[:, t * width:(t + 1) * width]
        zr_ref[:, t * width:(t + 1) * width] = (a * c + b * s).astype(zr_ref.dtype)
        zi_ref[:, t * width:(t + 1) * width] = (b * c - a * s).astype(zi_ref.dtype)


def _fourier_b_kernel(zr_ref, zi_ref, f2_ref, fc_ref, fw_ref, o_ref, *, l1, n_s2, norm):
    for a in range(n_s2):
        zz = jnp.concatenate([zr_ref[a * l1:(a + 1) * l1, :], zi_ref[a * l1:(a + 1) * l1, :]], axis=0)
        hh = jnp.dot(f2_ref[...], zz, preferred_element_type=F32)
        hr = hh[:l1].astype(BF16)
        hi = hh[l1:].astype(BF16)
        for g in range(N_GROUPS):
            sl = slice(g * GROUP_CH, (g + 1) * GROUP_CH)
            hg = jnp.concatenate([hr[:, sl], hi[:, sl]], axis=1)
            f = jnp.dot(hg, fc_ref[...], preferred_element_type=F32) * norm
            y = jnp.dot(f.astype(BF16), fw_ref[g], preferred_element_type=F32)
            o_ref[:, a, sl] = y


def _fourier(u, fw, l1, l2):
    B, L, width = u.shape
    k = min(FOURIER_S2_PER_STEP, l1)
    n_s2 = min(FOURIER_S2_PER_STEP, l2)
    c2, s2 = _dft_cos_sin(l2)
    f1 = jnp.asarray(np.concatenate([c2, -s2], axis=0), BF16)
    c1, s1 = _dft_cos_sin(l1)
    f2 = jnp.asarray(np.block([[c1, s1], [-s1, c1]]), BF16)
    cc, sc = _dft_cos_sin(GROUP_CH)
    fc = jnp.asarray(np.concatenate([cc, sc], axis=0), BF16)
    ang = 2.0 * np.pi * ((np.arange(l2)[:, None] * np.arange(l1)[None, :]) % L) / L
    tw = lambda t: jnp.asarray(t.reshape(l2, l1 // k, k).transpose(1, 0, 2), F32)
    twc, tws = tw(np.cos(ang)), tw(np.sin(ang))

    x = u.reshape(B, l2, l1 * width)
    col = lambda b, j: (b, 0, j)
    const2 = lambda b, j: (0, 0)
    zr, zi = pl.pallas_call(
        functools.partial(_fourier_a_kernel, l2=l2, k=k, width=width),
        grid=(B, l1 // k),
        in_specs=[
            pl.BlockSpec((None, l2, k * width), col),
            pl.BlockSpec(f1.shape, const2),
            pl.BlockSpec((None, l2, k), lambda b, j: (j, 0, 0)),
            pl.BlockSpec((None, l2, k), lambda b, j: (j, 0, 0)),
        ],
        out_specs=[pl.BlockSpec((None, l2, k * width), col)] * 2,
        out_shape=[jax.ShapeDtypeStruct((B, l2, l1 * width), BF16)] * 2,
        compiler_params=_params("parallel", "parallel"),
        name="fourier_a",
    )(x, f1, twc, tws)

    zr = zr.reshape(B, L, width)
    zi = zi.reshape(B, L, width)
    rows = lambda b, j: (b, j, 0)
    out = pl.pallas_call(
        functools.partial(_fourier_b_kernel, l1=l1, n_s2=n_s2, norm=1.0 / math.sqrt(L * GROUP_CH)),
        grid=(B, l2 // n_s2),
        in_specs=[
            pl.BlockSpec((None, n_s2 * l1, width), rows),
            pl.BlockSpec((None, n_s2 * l1, width), rows),
            pl.BlockSpec(f2.shape, const2),
            pl.BlockSpec(fc.shape, const2),
            pl.BlockSpec(fw.shape, lambda b, j: (0, 0, 0)),
        ],
        out_specs=pl.BlockSpec((None, l1, n_s2, width), lambda b, j: (b, 0, j, 0)),
        out_shape=jax.ShapeDtypeStruct((B, l1, l2, width), F32),
        compiler_params=_params("parallel", "parallel"),
        name="fourier_b",
    )(zr, zi, f2, fc, fw)
    return out.reshape(B, L, width)


def _mix_out_kernel(up_ref, uc_ref, un_ref, attn_ref, four_ref, x_ref, wo_ref, pw_ref, ps_ref,
                    g_ref, o_ref, ext_ref, *, tm, seq_len, widths):
    pool_w, attn_w, four_w = widths
    i = pl.program_id(1)
    last = pl.num_programs(1) - 1
    ext_ref[0:POOL_HALO, :] = jnp.where(i > 0, up_ref[...], 0.0)
    ext_ref[POOL_HALO:POOL_HALO + tm, :] = uc_ref[...]
    ext_ref[POOL_HALO + tm:, :] = jnp.where(i < last, un_ref[...], 0.0)

    t = i * tm + lax.broadcasted_iota(jnp.int32, (tm, 1), 0)
    m = jnp.zeros((tm, o_ref.shape[-1]), F32)
    for g, w in enumerate(POOL_WINDOWS):
        sl = slice(g * GROUP_CH, (g + 1) * GROUP_CH)
        lo = jnp.maximum(t - w // 2, 0)
        hi = jnp.minimum(t + (w - 1 - w // 2), seq_len - 1)
        cnt = (hi - lo + 1).astype(F32)
        win = ext_ref[POOL_HALO - w // 2:POOL_HALO - w // 2 + tm, sl]
        for d in range(1 - w // 2, w - w // 2):
            win = win + ext_ref[POOL_HALO + d:POOL_HALO + d + tm, sl]
        dg = win / cnt - uc_ref[:, sl]
        yg = jnp.dot(dg.astype(BF16), pw_ref[g], preferred_element_type=F32) * ps_ref[:, sl]
        m = m + jnp.dot(yg.astype(BF16), wo_ref[sl, :], preferred_element_type=F32)
    m = m + jnp.dot(attn_ref[...], wo_ref[pool_w:pool_w + attn_w, :], preferred_element_type=F32)
    m = m + jnp.dot(four_ref[...].astype(BF16), wo_ref[pool_w + attn_w:, :],
                    preferred_element_type=F32)
    o_ref[...] = x_ref[...] + _rms(m, g_ref[...])


def _mix_out(x, u_pool, attn, four, wo, pw, ps, g, tm):
    B, L, D = x.shape
    widths = (u_pool.shape[-1], attn.shape[-1], four.shape[-1])
    hb = tm // POOL_HALO
    n_hb = L // POOL_HALO
    row = lambda b, i: (b, i, 0)
    const2 = lambda b, i: (0, 0)
    kern = functools.partial(_mix_out_kernel, tm=tm, seq_len=L, widths=widths)
    return pl.pallas_call(
        kern,
        grid=(B, L // tm),
        in_specs=[
            pl.BlockSpec((None, POOL_HALO, widths[0]), lambda b, i: (b, jnp.maximum(i * hb - 1, 0), 0)),
            pl.BlockSpec((None, tm, widths[0]), row),
            pl.BlockSpec((None, POOL_HALO, widths[0]),
                         lambda b, i: (b, jnp.minimum((i + 1) * hb, n_hb - 1), 0)),
            pl.BlockSpec((None, tm, widths[1]), row),
            pl.BlockSpec((None, tm, widths[2]), row),
            pl.BlockSpec((None, tm, D), row),
            pl.BlockSpec(wo.shape, const2),
            pl.BlockSpec(pw.shape, lambda b, i: (0, 0, 0)),
            pl.BlockSpec((1, widths[0]), const2),
            pl.BlockSpec((1, D), const2),
        ],
        out_specs=pl.BlockSpec((None, tm, D), row),
        out_shape=jax.ShapeDtypeStruct((B, L, D), F32),
        scratch_shapes=[pltpu.VMEM((tm + 2 * POOL_HALO, widths[0]), F32)],
        compiler_params=_params("parallel", "parallel"),
        name="mix_out",
    )(u_pool, u_pool, u_pool, attn, four, x, wo, pw, ps, g)


def _gelu_tanh(x):
    return 0.5 * x * (1.0 + jnp.tanh(math.sqrt(2.0 / math.pi) * (x + 0.044715 * (x * x * x))))


def _ffn_kernel(xp_ref, x_ref, xn_ref, gpre_ref, wg_ref, wv_ref, cwg_ref, cwv_ref, cbg_ref, cbv_ref,
                wd_ref, gpost_ref, o_ref, h_ref, acc_ref, *, tm):
    i = pl.program_id(1)
    j = pl.program_id(2)
    rows = tm + 2 * CONV_HALO

    @pl.when(j == 0)
    def _():
        g = gpre_ref[...]
        hp = jnp.where(i > 0, _rms(xp_ref[...], g), 0.0)
        hn = jnp.where(i < pl.num_programs(1) - 1, _rms(xn_ref[...], g), 0.0)
        h_ref[0:CONV_HALO, :] = hp.astype(BF16)
        h_ref[CONV_HALO:CONV_HALO + tm, :] = _rms(x_ref[...], g).astype(BF16)
        h_ref[CONV_HALO + tm:, :] = hn.astype(BF16)
        acc_ref[...] = jnp.zeros(acc_ref.shape, F32)

    def conv(w_ref, cw_ref, cb_ref):
        u = jnp.dot(h_ref[...], w_ref[...], preferred_element_type=F32)
        prev = pltpu.roll(u, 1, axis=0)[CONV_HALO:CONV_HALO + tm]
        nxt = pltpu.roll(u, rows - 1, axis=0)[CONV_HALO:CONV_HALO + tm]
        cur = u[CONV_HALO:CONV_HALO + tm]
        return prev * cw_ref[0:1, :] + cur * cw_ref[1:2, :] + nxt * cw_ref[2:3, :] + cb_ref[...]

    gate = conv(wg_ref, cwg_ref, cbg_ref)
    val = conv(wv_ref, cwv_ref, cbv_ref)
    act = (_gelu_tanh(gate) * val).astype(BF16)
    acc_ref[...] += jnp.dot(act, wd_ref[...], preferred_element_type=F32)

    @pl.when(j == pl.num_programs(2) - 1)
    def _():
        o_ref[...] = x_ref[...] + _rms(acc_ref[...], gpost_ref[...])


def _ffn(x, gpre, w_up, conv_w, conv_b, w_down, gpost, tm, chunk):
    B, L, D = x.shape
    d_ff = w_down.shape[0]
    nc = d_ff // chunk
    hb = tm // CONV_HALO
    n_hb = L // CONV_HALO
    const2 = lambda b, i, j: (0, 0)
    gate_col = lambda b, i, j: (0, j)
    val_col = lambda b, i, j: (0, nc + j)
    kern = functools.partial(_ffn_kernel, tm=tm)
    return pl.pallas_call(
        kern,
        grid=(B, L // tm, nc),
        in_specs=[
            pl.BlockSpec((None, CONV_HALO, D), lambda b, i, j: (b, jnp.maximum(i * hb - 1, 0), 0)),
            pl.BlockSpec((None, tm, D), lambda b, i, j: (b, i, 0)),
            pl.BlockSpec((None, CONV_HALO, D),
                         lambda b, i, j: (b, jnp.minimum((i + 1) * hb, n_hb - 1), 0)),
            pl.BlockSpec((1, D), const2),
            pl.BlockSpec((D, chunk), gate_col),
            pl.BlockSpec((D, chunk), val_col),
            pl.BlockSpec((3, chunk), gate_col),
            pl.BlockSpec((3, chunk), val_col),
            pl.BlockSpec((1, chunk), gate_col),
            pl.BlockSpec((1, chunk), val_col),
            pl.BlockSpec((chunk, D), lambda b, i, j: (j, 0)),
            pl.BlockSpec((1, D), const2),
        ],
        out_specs=pl.BlockSpec((None, tm, D), lambda b, i, j: (b, i, 0)),
        out_shape=jax.ShapeDtypeStruct((B, L, D), F32),
        scratch_shapes=[
            pltpu.VMEM((tm + 2 * CONV_HALO, D), BF16),
            pltpu.VMEM((tm, D), F32),
        ],
        compiler_params=_params("parallel", "parallel", "arbitrary"),
        name="ffn",
    )(x, x, x, gpre, w_up, w_up, conv_w, conv_w, conv_b, conv_b, w_down, gpost)


def _rope_tables(seq_len):
    quarter = HEAD_DIM // 4
    t = jnp.arange(seq_len, dtype=jnp.int32)
    row = (t // GRID_W).astype(F32)
    col = (t % GRID_W).astype(F32)
    inv_freq = 1.0 / (ROPE_THETA ** (jnp.arange(quarter, dtype=F32) / quarter))
    ang = jnp.concatenate([row[:, None] * inv_freq[None, :], col[:, None] * inv_freq[None, :]], axis=-1)
    cos, sin = jnp.cos(ang), jnp.sin(ang)
    return jnp.concatenate([cos, cos], axis=-1), jnp.concatenate([-sin, sin], axis=-1)


def _fourier_split(seq_len):
    l2 = 1 << (int(math.log2(seq_len)) // 2)
    return seq_len // l2, l2


def _trunk(x, layers):
    B, L, D = x.shape
    ts = _tiles(L)
    cc, ss = _rope_tables(L)
    l1, l2 = _fourier_split(L)
    for p in layers:
        u_pool, q, k, v, u_four = _in_proj(x, p["g_pre_mix"], p["w_in"], cc, ss, p["q_norm"],
                                           p["k_norm"], p["widths"], ts["tm_proj"])
        attn = _attention(q, k, v, ts["tq"], ts["tk"])
        four = _fourier(u_four, p["fourier_w"], l1, l2)
        x = _mix_out(x, u_pool, attn, four, p["w_out"], p["pool_w"], p["pool_scale"],
                     p["g_post_mix"], ts["tm_mix"])
        x = _ffn(x, p["g_pre_ffn"], p["w_up"], p["conv_w"], p["conv_b"], p["w_down"],
                 p["g_post_ffn"], ts["tm_ffn"], ts["ff_chunk"])
    return x


def kernel(x_prompt, x_sample, g_pre_mix, g_post_mix, w_in, pool_w, pool_scale, q_norm, k_norm,
           fourier_w, w_out, g_pre_ffn, g_post_ffn, w_up, conv_w, conv_b, w_down):
    depth = w_in.shape[0]
    pool_width = pool_scale.shape[-1]
    four_width = fourier_w.shape[1] * fourier_w.shape[2]
    kv_width = N_KV_HEADS * HEAD_DIM
    q_width = w_in.shape[-1] - pool_width - four_width - 2 * kv_width
    layers = []
    for l in range(depth):
        layers.append(dict(
            widths=(pool_width, q_width, kv_width, four_width),
            g_pre_mix=g_pre_mix[l][None, :], g_post_mix=g_post_mix[l][None, :],
            w_in=w_in[l].astype(BF16), pool_w=pool_w[l].astype(BF16),
            pool_scale=pool_scale[l][None, :], q_norm=q_norm[l][None, :], k_norm=k_norm[l][None, :],
            fourier_w=fourier_w[l].astype(BF16), w_out=w_out[l].astype(BF16),
            g_pre_ffn=g_pre_ffn[l][None, :], g_post_ffn=g_post_ffn[l][None, :],
            w_up=w_up[l].astype(BF16), conv_w=conv_w[l], conv_b=conv_b[l][None, :],
            w_down=w_down[l].astype(BF16)))
    return _trunk(x_prompt, layers), _trunk(x_sample, layers)
```

```python
import functools
import math

import jax
import jax.numpy as jnp
import numpy as np
from jax import lax
from jax.experimental import pallas as pl
from jax.experimental.pallas import tpu as pltpu

F32 = jnp.float32
BF16 = jnp.bfloat16

NORM_EPS = 1e-6
GRID_W = 64
HEAD_DIM = 128
N_KV_HEADS = 2
GQA_GROUP = 4
POOL_WINDOWS = (2, 4, 8, 16)
POOL_HALO = 8
GROUP_CH = 128
N_GROUPS = 4
ROPE_THETA = 10000.0
CONV_HALO = 16
FOURIER_S2_PER_STEP = 8
V_ONES_ROWS = 16

V7X_VMEM_BYTES = 64 * 1024 * 1024
VMEM_LIMIT = V7X_VMEM_BYTES - 8 * 1024 * 1024


def _tiles(seq_len):
    return dict(
        tm_proj=min(512, seq_len),
        tm_mix=min(512, seq_len),
        tm_ffn=min(512, seq_len),
        ff_chunk=512,
        tq=min(256, seq_len),
    )


def _params(*sem):
    return pltpu.CompilerParams(dimension_semantics=sem, vmem_limit_bytes=VMEM_LIMIT)


def _rms(x, g):
    return x * lax.rsqrt(jnp.mean(x * x, axis=-1, keepdims=True) + NORM_EPS) * g


def _in_proj_kernel(x_ref, g_ref, w_ref, cc_ref, ss_ref, qn_ref, kn_ref,
                    pool_ref, q_ref, k_ref, v_ref, four_ref, *, widths, scale):
    pool_w, q_w, kv_w, four_w = widths
    h = _rms(x_ref[...], g_ref[...]).astype(BF16)
    cc = cc_ref[...]
    ss = ss_ref[...]

    def seg(lo, width):
        return jnp.dot(h, w_ref[:, lo:lo + width], preferred_element_type=F32)

    def norm_rope(zh, gain, out_scale):
        y = zh * lax.rsqrt(jnp.mean(zh * zh, axis=-1, keepdims=True) + NORM_EPS) * gain
        y = y * cc + pltpu.roll(y, HEAD_DIM // 2, axis=1) * ss
        return y * out_scale

    pool_ref[...] = seg(0, pool_w)
    pair = 2 * HEAD_DIM
    for p in range(q_w // pair):
        z = seg(pool_w + p * pair, pair)
        for s in range(2):
            zh = z[:, s * HEAD_DIM:(s + 1) * HEAD_DIM]
            q_ref[2 * p + s] = norm_rope(zh, qn_ref[...], scale).T.astype(BF16)
    z = seg(pool_w + q_w, kv_w)
    for s in range(kv_w // HEAD_DIM):
        zh = z[:, s * HEAD_DIM:(s + 1) * HEAD_DIM]
        k_ref[:, s * HEAD_DIM:(s + 1) * HEAD_DIM] = norm_rope(zh, kn_ref[...], 1.0).astype(BF16)
    z = seg(pool_w + q_w + kv_w, kv_w)
    for s in range(kv_w // HEAD_DIM):
        v_ref[s, 0:HEAD_DIM, :] = z[:, s * HEAD_DIM:(s + 1) * HEAD_DIM].T.astype(BF16)
        v_ref[s, HEAD_DIM:, :] = jnp.ones((V_ONES_ROWS, z.shape[0]), BF16)
    four_ref[...] = seg(pool_w + q_w + 2 * kv_w, four_w).astype(BF16)


def _in_proj(x, g, w, cc, ss, qn, kn, widths, tm):
    B, L, D = x.shape
    pool_w, q_w, kv_w, four_w = widths
    nt = L // tm
    row = lambda b, i: (b, i, 0)
    const2 = lambda b, i: (0, 0)
    n_q = q_w // HEAD_DIM
    v_rows = HEAD_DIM + V_ONES_ROWS
    kern = functools.partial(_in_proj_kernel, widths=widths, scale=HEAD_DIM ** -0.5 * math.log2(math.e))
    return pl.pallas_call(
        kern,
        grid=(B, nt),
        in_specs=[
            pl.BlockSpec((None, tm, D), row),
            pl.BlockSpec((1, D), const2),
            pl.BlockSpec(w.shape, const2),
            pl.BlockSpec((tm, HEAD_DIM), lambda b, i: (i, 0)),
            pl.BlockSpec((tm, HEAD_DIM), lambda b, i: (i, 0)),
            pl.BlockSpec((1, HEAD_DIM), const2),
            pl.BlockSpec((1, HEAD_DIM), const2),
        ],
        out_specs=[
            pl.BlockSpec((None, tm, pool_w), row),
            pl.BlockSpec((None, n_q, HEAD_DIM, tm), lambda b, i: (b, 0, 0, i)),
            pl.BlockSpec((None, tm, kv_w), row),
            pl.BlockSpec((None, N_KV_HEADS, None, v_rows, tm), lambda b, i: (b, 0, i, 0, 0)),
            pl.BlockSpec((None, tm, four_w), row),
        ],
        out_shape=[
            jax.ShapeDtypeStruct((B, L, pool_w), F32),
            jax.ShapeDtypeStruct((B, n_q, HEAD_DIM, L), BF16),
            jax.ShapeDtypeStruct((B, L, kv_w), BF16),
            jax.ShapeDtypeStruct((B, N_KV_HEADS, nt, v_rows, tm), BF16),
            jax.ShapeDtypeStruct((B, L, four_w), BF16),
        ],
        compiler_params=_params("parallel", "parallel"),
        name="in_proj",
    )(x, g, w, cc, ss, qn, kn)


def _attn_kernel(qt_ref, k_ref, vt_ref, o_ref, m_ref, acc_ref, sa_ref, sb_ref, ma_ref, mb_ref,
                 *, tk, nk):
    assert nk == 1 or nk % 2 == 0
    m_ref[...] = jnp.full(m_ref.shape, -jnp.inf, F32)
    acc_ref[...] = jnp.zeros(acc_ref.shape, F32)

    def scores(c, s_ref, cm_ref):
        kc = k_ref[pl.ds(pl.multiple_of(c * tk, tk), tk), :]
        for g in range(GQA_GROUP):
            st = jnp.dot(kc, qt_ref[g], preferred_element_type=F32)
            s_ref[g] = st
            cm_ref[g] = jnp.max(st, axis=0, keepdims=True)

    def accumulate(c, s_ref, cm_ref):
        vc = vt_ref[c]
        for g in range(GQA_GROUP):
            m_old = m_ref[g]
            m_new = jnp.maximum(m_old, cm_ref[g])
            alpha = jnp.exp2(m_old - m_new)
            p = jnp.exp2(s_ref[g] - m_new).astype(BF16)
            acc_ref[g] = alpha * acc_ref[g] + jnp.dot(vc, p, preferred_element_type=F32)
            m_ref[g] = m_new

    scores(0, sa_ref, ma_ref)
    if nk > 1:
        def pair(i, carry):
            scores(2 * i + 1, sb_ref, mb_ref)
            accumulate(2 * i, sa_ref, ma_ref)
            scores(2 * i + 2, sa_ref, ma_ref)
            accumulate(2 * i + 1, sb_ref, mb_ref)
            return carry

        lax.fori_loop(0, nk // 2 - 1, pair, 0)
        scores(nk - 1, sb_ref, mb_ref)
        accumulate(nk - 2, sa_ref, ma_ref)
        accumulate(nk - 1, sb_ref, mb_ref)
    else:
        accumulate(0, sa_ref, ma_ref)
    for g in range(GQA_GROUP):
        a = acc_ref[g]
        o = a[:HEAD_DIM] / a[HEAD_DIM:HEAD_DIM + 1]
        o_ref[:, g * HEAD_DIM:(g + 1) * HEAD_DIM] = o.T.astype(o_ref.dtype)


def _attention(qt, k, vt, tq):
    B, n_q, _, L = qt.shape
    _, _, nk, v_rows, tk = vt.shape
    gw = GQA_GROUP * HEAD_DIM
    kern = functools.partial(_attn_kernel, tk=tk, nk=nk)
    return pl.pallas_call(
        kern,
        grid=(B, N_KV_HEADS, L // tq),
        in_specs=[
            pl.BlockSpec((None, GQA_GROUP, HEAD_DIM, tq), lambda b, j, i: (b, j, 0, i)),
            pl.BlockSpec((None, L, HEAD_DIM), lambda b, j, i: (b, 0, j)),
            pl.BlockSpec((None, None, nk, v_rows, tk), lambda b, j, i: (b, j, 0, 0, 0)),
        ],
        out_specs=pl.BlockSpec((None, tq, gw), lambda b, j, i: (b, i, j)),
        out_shape=jax.ShapeDtypeStruct((B, L, n_q * HEAD_DIM), BF16),
        scratch_shapes=[
            pltpu.VMEM((GQA_GROUP, 1, tq), F32),
            pltpu.VMEM((GQA_GROUP, v_rows, tq), F32),
            pltpu.VMEM((GQA_GROUP, tk, tq), F32),
            pltpu.VMEM((GQA_GROUP, tk, tq), F32),
            pltpu.VMEM((GQA_GROUP, 1, tq), F32),
            pltpu.VMEM((GQA_GROUP, 1, tq), F32),
        ],
        compiler_params=_params("parallel", "parallel", "arbitrary"),
        name="attention",
    )(qt, k, vt)


def _dft_cos_sin(n):
    idx = np.arange(n)
    ang = 2.0 * np.pi * ((idx[:, None] * idx[None, :]) % n) / n
    return np.cos(ang), np.sin(ang)


def _fourier_a_kernel(x_ref, f1_ref, twc_ref, tws_ref, zr_ref, zi_ref, *, l2, k, width):
    z = jnp.dot(f1_ref[...], x_ref[...], preferred_element_type=F32)
    zr = z[:l2]
    zi = z[l2:]
    for t in range(k):
        c = twc_ref[:, t:t + 1]
        s = tws_ref[:, t:t + 1]
        a = zr[:, t * width:(t + 1) * width]
        b = zi[:, t * width:(t + 1) * width]
        zr_ref[:, t * width:(t + 1) * width] = (a * c + b * s).astype(zr_ref.dtype)
        zi_ref[:, t * width:(t + 1) * width] = (b * c - a * s).astype(zi_ref.dtype)


def _fourier_b_kernel(zr_ref, zi_ref, f2_ref, fc_ref, fw_ref, o_ref, *, l1, n_s2, norm):
    for a in range(n_s2):
        zz = jnp.concatenate([zr_ref[a * l1:(a + 1) * l1, :], zi_ref[a * l1:(a + 1) * l1, :]], axis=0)
        hh = jnp.dot(f2_ref[...], zz, preferred_element_type=F32)
        hr = hh[:l1].astype(BF16)
        hi = hh[l1:].astype(BF16)
        for g in range(N_GROUPS):
            sl = slice(g * GROUP_CH, (g + 1) * GROUP_CH)
            hg = jnp.concatenate([hr[:, sl], hi[:, sl]], axis=1)
            f = jnp.dot(hg, fc_ref[...], preferred_element_type=F32) * norm
            y = jnp.dot(f.astype(BF16), fw_ref[g], preferred_element_type=F32)
            o_ref[:, a, sl] = y


def _fourier(u, fw, l1, l2):
    B, L, width = u.shape
    k = min(FOURIER_S2_PER_STEP, l1)
    n_s2 = min(FOURIER_S2_PER_STEP, l2)
    c2, s2 = _dft_cos_sin(l2)
    f1 = jnp.asarray(np.concatenate([c2, -s2], axis=0), BF16)
    c1, s1 = _dft_cos_sin(l1)
    f2 = jnp.asarray(np.block([[c1, s1], [-s1, c1]]), BF16)
    cc, sc = _dft_cos_sin(GROUP_CH)
    fc = jnp.asarray(np.concatenate([cc, sc], axis=0), BF16)
    ang = 2.0 * np.pi * ((np.arange(l2)[:, None] * np.arange(l1)[None, :]) % L) / L
    tw = lambda t: jnp.asarray(t.reshape(l2, l1 // k, k).transpose(1, 0, 2), F32)
    twc, tws = tw(np.cos(ang)), tw(np.sin(ang))

    x = u.reshape(B, l2, l1 * width)
    col = lambda b, j: (b, 0, j)
    const2 = lambda b, j: (0, 0)
    zr, zi = pl.pallas_call(
        functools.partial(_fourier_a_kernel, l2=l2, k=k, width=width),
        grid=(B, l1 // k),
        in_specs=[
            pl.BlockSpec((None, l2, k * width), col),
            pl.BlockSpec(f1.shape, const2),
            pl.BlockSpec((None, l2, k), lambda b, j: (j, 0, 0)),
            pl.BlockSpec((None, l2, k), lambda b, j: (j, 0, 0)),
        ],
        out_specs=[pl.BlockSpec((None, l2, k * width), col)] * 2,
        out_shape=[jax.ShapeDtypeStruct((B, l2, l1 * width), BF16)] * 2,
        compiler_params=_params("parallel", "parallel"),
        name="fourier_a",
    )(x, f1, twc, tws)

    zr = zr.reshape(B, L, width)
    zi = zi.reshape(B, L, width)
    rows = lambda b, j: (b, j, 0)
    out = pl.pallas_call(
        functools.partial(_fourier_b_kernel, l1=l1, n_s2=n_s2, norm=1.0 / math.sqrt(L * GROUP_CH)),
        grid=(B, l2 // n_s2),
        in_specs=[
            pl.BlockSpec((None, n_s2 * l1, width), rows),
            pl.BlockSpec((None, n_s2 * l1, width), rows),
            pl.BlockSpec(f2.shape, const2),
            pl.BlockSpec(fc.shape, const2),
            pl.BlockSpec(fw.shape, lambda b, j: (0, 0, 0)),
        ],
        out_specs=pl.BlockSpec((None, l1, n_s2, width), lambda b, j: (b, 0, j, 0)),
        out_shape=jax.ShapeDtypeStruct((B, l1, l2, width), F32),
        compiler_params=_params("parallel", "parallel"),
        name="fourier_b",
    )(zr, zi, f2, fc, fw)
    return out.reshape(B, L, width)


def _mix_out_kernel(up_ref, uc_ref, un_ref, attn_ref, four_ref, x_ref, wo_ref, pw_ref, ps_ref,
                    g_ref, o_ref, ext_ref, *, tm, seq_len, widths):
    pool_w, attn_w, four_w = widths
    i = pl.program_id(1)
    last = pl.num_programs(1) - 1
    ext_ref[0:POOL_HALO, :] = jnp.where(i > 0, up_ref[...], 0.0)
    ext_ref[POOL_HALO:POOL_HALO + tm, :] = uc_ref[...]
    ext_ref[POOL_HALO + tm:, :] = jnp.where(i < last, un_ref[...], 0.0)

    t = i * tm + lax.broadcasted_iota(jnp.int32, (tm, 1), 0)
    m = jnp.zeros((tm, o_ref.shape[-1]), F32)
    for g, w in enumerate(POOL_WINDOWS):
        sl = slice(g * GROUP_CH, (g + 1) * GROUP_CH)
        lo = jnp.maximum(t - w // 2, 0)
        hi = jnp.minimum(t + (w - 1 - w // 2), seq_len - 1)
        cnt = (hi - lo + 1).astype(F32)
        win = ext_ref[POOL_HALO - w // 2:POOL_HALO - w // 2 + tm, sl]
        for d in range(1 - w // 2, w - w // 2):
            win = win + ext_ref[POOL_HALO + d:POOL_HALO + d + tm, sl]
        dg = win / cnt - uc_ref[:, sl]
        yg = jnp.dot(dg.astype(BF16), pw_ref[g], preferred_element_type=F32) * ps_ref[:, sl]
        m = m + jnp.dot(yg.astype(BF16), wo_ref[sl, :], preferred_element_type=F32)
    m = m + jnp.dot(attn_ref[...], wo_ref[pool_w:pool_w + attn_w, :], preferred_element_type=F32)
    m = m + jnp.dot(four_ref[...].astype(BF16), wo_ref[pool_w + attn_w:, :],
                    preferred_element_type=F32)
    o_ref[...] = x_ref[...] + _rms(m, g_ref[...])


def _mix_out(x, u_pool, attn, four, wo, pw, ps, g, tm):
    B, L, D = x.shape
    widths = (u_pool.shape[-1], attn.shape[-1], four.shape[-1])
    hb = tm // POOL_HALO
    n_hb = L // POOL_HALO
    row = lambda b, i: (b, i, 0)
    const2 = lambda b, i: (0, 0)
    kern = functools.partial(_mix_out_kernel, tm=tm, seq_len=L, widths=widths)
    return pl.pallas_call(
        kern,
        grid=(B, L // tm),
        in_specs=[
            pl.BlockSpec((None, POOL_HALO, widths[0]), lambda b, i: (b, jnp.maximum(i * hb - 1, 0), 0)),
            pl.BlockSpec((None, tm, widths[0]), row),
            pl.BlockSpec((None, POOL_HALO, widths[0]),
                         lambda b, i: (b, jnp.minimum((i + 1) * hb, n_hb - 1), 0)),
            pl.BlockSpec((None, tm, widths[1]), row),
            pl.BlockSpec((None, tm, widths[2]), row),
            pl.BlockSpec((None, tm, D), row),
            pl.BlockSpec(wo.shape, const2),
            pl.BlockSpec(pw.shape, lambda b, i: (0, 0, 0)),
            pl.BlockSpec((1, widths[0]), const2),
            pl.BlockSpec((1, D), const2),
        ],
        out_specs=pl.BlockSpec((None, tm, D), row),
        out_shape=jax.ShapeDtypeStruct((B, L, D), F32),
        scratch_shapes=[pltpu.VMEM((tm + 2 * POOL_HALO, widths[0]), F32)],
        compiler_params=_params("parallel", "parallel"),
        name="mix_out",
    )(u_pool, u_pool, u_pool, attn, four, x, wo, pw, ps, g)


def _gelu_tanh(x):
    return 0.5 * x * (1.0 + jnp.tanh(math.sqrt(2.0 / math.pi) * (x + 0.044715 * (x * x * x))))


def _ffn_kernel(xp_ref, x_ref, xn_ref, gpre_ref, wg_ref, wv_ref, cwg_ref, cwv_ref, cbg_ref, cbv_ref,
                wd_ref, gpost_ref, o_ref, h_ref, acc_ref, *, tm):
    i = pl.program_id(1)
    j = pl.program_id(2)
    rows = tm + 2 * CONV_HALO

    @pl.when(j == 0)
    def _():
        g = gpre_ref[...]
        hp = jnp.where(i > 0, _rms(xp_ref[...], g), 0.0)
        hn = jnp.where(i < pl.num_programs(1) - 1, _rms(xn_ref[...], g), 0.0)
        h_ref[0:CONV_HALO, :] = hp.astype(BF16)
        h_ref[CONV_HALO:CONV_HALO + tm, :] = _rms(x_ref[...], g).astype(BF16)
        h_ref[CONV_HALO + tm:, :] = hn.astype(BF16)
        acc_ref[...] = jnp.zeros(acc_ref.shape, F32)

    def conv(w_ref, cw_ref, cb_ref):
        u = jnp.dot(h_ref[...], w_ref[...], preferred_element_type=F32)
        prev = pltpu.roll(u, 1, axis=0)[CONV_HALO:CONV_HALO + tm]
        nxt = pltpu.roll(u, rows - 1, axis=0)[CONV_HALO:CONV_HALO + tm]
        cur = u[CONV_HALO:CONV_HALO + tm]
        return prev * cw_ref[0:1, :] + cur * cw_ref[1:2, :] + nxt * cw_ref[2:3, :] + cb_ref[...]

    gate = conv(wg_ref, cwg_ref, cbg_ref)
    val = conv(wv_ref, cwv_ref, cbv_ref)
    act = (_gelu_tanh(gate) * val).astype(BF16)
    acc_ref[...] += jnp.dot(act, wd_ref[...], preferred_element_type=F32)

    @pl.when(j == pl.num_programs(2) - 1)
    def _():
        o_ref[...] = x_ref[...] + _rms(acc_ref[...], gpost_ref[...])


def _ffn(x, gpre, w_up, conv_w, conv_b, w_down, gpost, tm, chunk):
    B, L, D = x.shape
    d_ff = w_down.shape[0]
    nc = d_ff // chunk
    hb = tm // CONV_HALO
    n_hb = L // CONV_HALO
    const2 = lambda b, i, j: (0, 0)
    gate_col = lambda b, i, j: (0, j)
    val_col = lambda b, i, j: (0, nc + j)
    kern = functools.partial(_ffn_kernel, tm=tm)
    return pl.pallas_call(
        kern,
        grid=(B, L // tm, nc),
        in_specs=[
            pl.BlockSpec((None, CONV_HALO, D), lambda b, i, j: (b, jnp.maximum(i * hb - 1, 0), 0)),
            pl.BlockSpec((None, tm, D), lambda b, i, j: (b, i, 0)),
            pl.BlockSpec((None, CONV_HALO, D),
                         lambda b, i, j: (b, jnp.minimum((i + 1) * hb, n_hb - 1), 0)),
            pl.BlockSpec((1, D), const2),
            pl.BlockSpec((D, chunk), gate_col),
            pl.BlockSpec((D, chunk), val_col),
            pl.BlockSpec((3, chunk), gate_col),
            pl.BlockSpec((3, chunk), val_col),
            pl.BlockSpec((1, chunk), gate_col),
            pl.BlockSpec((1, chunk), val_col),
            pl.BlockSpec((chunk, D), lambda b, i, j: (j, 0)),
            pl.BlockSpec((1, D), const2),
        ],
        out_specs=pl.BlockSpec((None, tm, D), lambda b, i, j: (b, i, 0)),
        out_shape=jax.ShapeDtypeStruct((B, L, D), F32),
        scratch_shapes=[
            pltpu.VMEM((tm + 2 * CONV_HALO, D), BF16),
            pltpu.VMEM((tm, D), F32),
        ],
        compiler_params=_params("parallel", "parallel", "arbitrary"),
        name="ffn",
    )(x, x, x, gpre, w_up, w_up, conv_w, conv_w, conv_b, conv_b, w_down, gpost)


def _rope_tables(seq_len):
    quarter = HEAD_DIM // 4
    t = jnp.arange(seq_len, dtype=jnp.int32)
    row = (t // GRID_W).astype(F32)
    col = (t % GRID_W).astype(F32)
    inv_freq = 1.0 / (ROPE_THETA ** (jnp.arange(quarter, dtype=F32) / quarter))
    ang = jnp.concatenate([row[:, None] * inv_freq[None, :], col[:, None] * inv_freq[None, :]], axis=-1)
    cos, sin = jnp.cos(ang), jnp.sin(ang)
    return jnp.concatenate([cos, cos], axis=-1), jnp.concatenate([-sin, sin], axis=-1)


def _fourier_split(seq_len):
    l2 = 1 << (int(math.log2(seq_len)) // 2)
    return seq_len // l2, l2


def _trunk(x, layers):
    B, L, D = x.shape
    ts = _tiles(L)
    cc, ss = _rope_tables(L)
    l1, l2 = _fourier_split(L)
    for p in layers:
        u_pool, qt, k, vt, u_four = _in_proj(x, p["g_pre_mix"], p["w_in"], cc, ss, p["q_norm"],
                                             p["k_norm"], p["widths"], ts["tm_proj"])
        attn = _attention(qt, k, vt, ts["tq"])
        four = _fourier(u_four, p["fourier_w"], l1, l2)
        x = _mix_out(x, u_pool, attn, four, p["w_out"], p["pool_w"], p["pool_scale"],
                     p["g_post_mix"], ts["tm_mix"])
        x = _ffn(x, p["g_pre_ffn"], p["w_up"], p["conv_w"], p["conv_b"], p["w_down"],
                 p["g_post_ffn"], ts["tm_ffn"], ts["ff_chunk"])
    return x


def kernel(x_prompt, x_sample, g_pre_mix, g_post_mix, w_in, pool_w, pool_scale, q_norm, k_norm,
           fourier_w, w_out, g_pre_ffn, g_post_ffn, w_up, conv_w, conv_b, w_down):
    depth = w_in.shape[0]
    pool_width = pool_scale.shape[-1]
    four_width = fourier_w.shape[1] * fourier_w.shape[2]
    kv_width = N_KV_HEADS * HEAD_DIM
    q_width = w_in.shape[-1] - pool_width - four_width - 2 * kv_width
    layers = []
    for l in range(depth):
        layers.append(dict(
            widths=(pool_width, q_width, kv_width, four_width),
            g_pre_mix=g_pre_mix[l][None, :], g_post_mix=g_post_mix[l][None, :],
            w_in=w_in[l].astype(BF16), pool_w=pool_w[l].astype(BF16),
            pool_scale=pool_scale[l][None, :], q_norm=q_norm[l][None, :], k_norm=k_norm[l][None, :],
            fourier_w=fourier_w[l].astype(BF16), w_out=w_out[l].astype(BF16),
            g_pre_ffn=g_pre_ffn[l][None, :], g_post_ffn=g_post_ffn[l][None, :],
            w_up=w_up[l].astype(BF16), conv_w=conv_w[l], conv_b=conv_b[l][None, :],
            w_down=w_down[l].astype(BF16)))
    return _trunk(x_prompt, layers), _trunk(x_sample, layers)
```

```python
import functools
import math

import jax
import jax.numpy as jnp
import numpy as np
from jax import lax
from jax.experimental import pallas as pl
from jax.experimental.pallas import tpu as pltpu

F32 = jnp.float32
BF16 = jnp.bfloat16

NORM_EPS = 1e-6
GRID_W = 64
HEAD_DIM = 128
N_KV_HEADS = 2
GQA_GROUP = 4
POOL_WINDOWS = (2, 4, 8, 16)
POOL_HALO = 8
GROUP_CH = 128
N_GROUPS = 4
ROPE_THETA = 10000.0
CONV_HALO = 16
FOURIER_S2_PER_STEP = 8
FFN_SUB_CHUNK = 256
V_ONES_ROWS = 16

V7X_VMEM_BYTES = 64 * 1024 * 1024
VMEM_LIMIT = V7X_VMEM_BYTES - 8 * 1024 * 1024


def _tiles(seq_len):
    return dict(
        tm_proj=min(512, seq_len),
        tm_mix=min(512, seq_len),
        tm_ffn=min(512, seq_len),
        ff_chunk=512,
        tq=min(512, seq_len),
    )


def _params(*sem):
    return pltpu.CompilerParams(dimension_semantics=sem, vmem_limit_bytes=VMEM_LIMIT)


def _rms(x, g):
    return x * lax.rsqrt(jnp.mean(x * x, axis=-1, keepdims=True) + NORM_EPS) * g


def _in_proj_kernel(x_ref, g_ref, w_ref, cc_ref, ss_ref, qn_ref, kn_ref,
                    pool_ref, q_ref, k_ref, v_ref, four_ref, *, widths, scale):
    pool_w, q_w, kv_w, four_w = widths
    h = _rms(x_ref[...], g_ref[...]).astype(BF16)
    cc = cc_ref[...]
    ss = ss_ref[...]

    def seg(lo, width):
        return jnp.dot(h, w_ref[:, lo:lo + width], preferred_element_type=F32)

    def norm_rope(zh, gain, out_scale):
        y = zh * lax.rsqrt(jnp.mean(zh * zh, axis=-1, keepdims=True) + NORM_EPS) * gain
        y = y * cc + pltpu.roll(y, HEAD_DIM // 2, axis=1) * ss
        return y * out_scale

    pool_ref[...] = seg(0, pool_w)
    pair = 2 * HEAD_DIM
    for p in range(q_w // pair):
        z = seg(pool_w + p * pair, pair)
        for s in range(2):
            zh = z[:, s * HEAD_DIM:(s + 1) * HEAD_DIM]
            q_ref[2 * p + s] = norm_rope(zh, qn_ref[...], scale).T.astype(BF16)
    z = seg(pool_w + q_w, kv_w)
    for s in range(kv_w // HEAD_DIM):
        zh = z[:, s * HEAD_DIM:(s + 1) * HEAD_DIM]
        k_ref[:, s * HEAD_DIM:(s + 1) * HEAD_DIM] = norm_rope(zh, kn_ref[...], 1.0).astype(BF16)
    z = seg(pool_w + q_w + kv_w, kv_w)
    for s in range(kv_w // HEAD_DIM):
        v_ref[s, 0:HEAD_DIM, :] = z[:, s * HEAD_DIM:(s + 1) * HEAD_DIM].T.astype(BF16)
        v_ref[s, HEAD_DIM:, :] = jnp.ones((V_ONES_ROWS, z.shape[0]), BF16)
    four_ref[...] = seg(pool_w + q_w + 2 * kv_w, four_w).astype(BF16)


def _in_proj(x, g, w, cc, ss, qn, kn, widths, tm):
    B, L, D = x.shape
    pool_w, q_w, kv_w, four_w = widths
    nt = L // tm
    row = lambda b, i: (b, i, 0)
    const2 = lambda b, i: (0, 0)
    n_q = q_w // HEAD_DIM
    v_rows = HEAD_DIM + V_ONES_ROWS
    kern = functools.partial(_in_proj_kernel, widths=widths, scale=HEAD_DIM ** -0.5 * math.log2(math.e))
    return pl.pallas_call(
        kern,
        grid=(B, nt),
        in_specs=[
            pl.BlockSpec((None, tm, D), row),
            pl.BlockSpec((1, D), const2),
            pl.BlockSpec(w.shape, const2),
            pl.BlockSpec((tm, HEAD_DIM), lambda b, i: (i, 0)),
            pl.BlockSpec((tm, HEAD_DIM), lambda b, i: (i, 0)),
            pl.BlockSpec((1, HEAD_DIM), const2),
            pl.BlockSpec((1, HEAD_DIM), const2),
        ],
        out_specs=[
            pl.BlockSpec((None, tm, pool_w), row),
            pl.BlockSpec((None, n_q, HEAD_DIM, tm), lambda b, i: (b, 0, 0, i)),
            pl.BlockSpec((None, tm, kv_w), row),
            pl.BlockSpec((None, N_KV_HEADS, None, v_rows, tm), lambda b, i: (b, 0, i, 0, 0)),
            pl.BlockSpec((None, tm, four_w), row),
        ],
        out_shape=[
            jax.ShapeDtypeStruct((B, L, pool_w), F32),
            jax.ShapeDtypeStruct((B, n_q, HEAD_DIM, L), BF16),
            jax.ShapeDtypeStruct((B, L, kv_w), BF16),
            jax.ShapeDtypeStruct((B, N_KV_HEADS, nt, v_rows, tm), BF16),
            jax.ShapeDtypeStruct((B, L, four_w), BF16),
        ],
        compiler_params=_params("parallel", "parallel"),
        name="in_proj",
    )(x, g, w, cc, ss, qn, kn)


def _attn_kernel(qt_ref, k_ref, vt_ref, o_ref, m_ref, acc_ref, sa_ref, sb_ref, ma_ref, mb_ref,
                 *, tk, nk):
    assert nk == 1 or nk % 2 == 0
    m_ref[...] = jnp.full(m_ref.shape, -jnp.inf, F32)
    acc_ref[...] = jnp.zeros(acc_ref.shape, F32)

    def scores(c, s_ref, cm_ref):
        kc = k_ref[pl.ds(pl.multiple_of(c * tk, tk), tk), :]
        for g in range(GQA_GROUP):
            st = jnp.dot(kc, qt_ref[g], preferred_element_type=F32)
            s_ref[g] = st
            cm_ref[g] = jnp.max(st, axis=0, keepdims=True)

    def accumulate(c, s_ref, cm_ref):
        vc = vt_ref[c]
        for g in range(GQA_GROUP):
            m_old = m_ref[g]
            m_new = jnp.maximum(m_old, cm_ref[g])
            alpha = jnp.exp2(m_old - m_new)
            p = jnp.exp2(s_ref[g] - m_new).astype(BF16)
            acc_ref[g] = alpha * acc_ref[g] + jnp.dot(vc, p, preferred_element_type=F32)
            m_ref[g] = m_new

    scores(0, sa_ref, ma_ref)
    if nk > 1:
        def pair(i, carry):
            scores(2 * i + 1, sb_ref, mb_ref)
            accumulate(2 * i, sa_ref, ma_ref)
            scores(2 * i + 2, sa_ref, ma_ref)
            accumulate(2 * i + 1, sb_ref, mb_ref)
            return carry

        lax.fori_loop(0, nk // 2 - 1, pair, 0)
        scores(nk - 1, sb_ref, mb_ref)
        accumulate(nk - 2, sa_ref, ma_ref)
        accumulate(nk - 1, sb_ref, mb_ref)
    else:
        accumulate(0, sa_ref, ma_ref)
    for g in range(GQA_GROUP):
        a = acc_ref[g]
        o = a[:HEAD_DIM] / a[HEAD_DIM:HEAD_DIM + 1]
        o_ref[:, g * HEAD_DIM:(g + 1) * HEAD_DIM] = o.T.astype(o_ref.dtype)


def _attention(qt, k, vt, tq):
    B, n_q, _, L = qt.shape
    _, _, nk, v_rows, tk = vt.shape
    gw = GQA_GROUP * HEAD_DIM
    kern = functools.partial(_attn_kernel, tk=tk, nk=nk)
    return pl.pallas_call(
        kern,
        grid=(B, N_KV_HEADS, L // tq),
        in_specs=[
            pl.BlockSpec((None, GQA_GROUP, HEAD_DIM, tq), lambda b, j, i: (b, j, 0, i)),
            pl.BlockSpec((None, L, HEAD_DIM), lambda b, j, i: (b, 0, j)),
            pl.BlockSpec((None, None, nk, v_rows, tk), lambda b, j, i: (b, j, 0, 0, 0)),
        ],
        out_specs=pl.BlockSpec((None, tq, gw), lambda b, j, i: (b, i, j)),
        out_shape=jax.ShapeDtypeStruct((B, L, n_q * HEAD_DIM), BF16),
        scratch_shapes=[
            pltpu.VMEM((GQA_GROUP, 1, tq), F32),
            pltpu.VMEM((GQA_GROUP, v_rows, tq), F32),
            pltpu.VMEM((GQA_GROUP, tk, tq), F32),
            pltpu.VMEM((GQA_GROUP, tk, tq), F32),
            pltpu.VMEM((GQA_GROUP, 1, tq), F32),
            pltpu.VMEM((GQA_GROUP, 1, tq), F32),
        ],
        compiler_params=_params("parallel", "parallel", "arbitrary"),
        name="attention",
    )(qt, k, vt)


def _dft_cos_sin(n):
    idx = np.arange(n)
    ang = 2.0 * np.pi * ((idx[:, None] * idx[None, :]) % n) / n
    return np.cos(ang), np.sin(ang)


def _fourier_a_kernel(x_ref, f1_ref, twc_ref, tws_ref, zr_ref, zi_ref, *, l2, k, width):
    z = jnp.dot(f1_ref[...], x_ref[...], preferred_element_type=F32)
    zr = z[:l2]
    zi = z[l2:]
    for t in range(k):
        c = twc_ref[:, t:t + 1]
        s = tws_ref[:, t:t + 1]
        a = zr[:, t * width:(t + 1) * width]
        b = zi[:, t * width:(t + 1) * width]
        zr_ref[:, t * width:(t + 1) * width] = (a * c + b * s).astype(zr_ref.dtype)
        zi_ref[:, t * width:(t + 1) * width] = (b * c - a * s).astype(zi_ref.dtype)


def _fourier_b_kernel(zr_ref, zi_ref, f2_ref, fc_ref, fw_ref, o_ref, hr_ref, hi_ref, *, l1, n_s2, norm):
    for a in range(n_s2):
        rows = slice(a * l1, (a + 1) * l1)
        zz = jnp.concatenate([zr_ref[rows, :], zi_ref[rows, :]], axis=0)
        hh = jnp.dot(f2_ref[...], zz, preferred_element_type=F32)
        hr_ref[rows, :] = hh[:l1].astype(BF16)
        hi_ref[rows, :] = hh[l1:].astype(BF16)
    for g in range(N_GROUPS):
        sl = slice(g * GROUP_CH, (g + 1) * GROUP_CH)
        hg = jnp.concatenate([hr_ref[:, sl], hi_ref[:, sl]], axis=1)
        f = jnp.dot(hg, fc_ref[...], preferred_element_type=F32) * norm
        y = jnp.dot(f.astype(BF16), fw_ref[g], preferred_element_type=F32)
        for a in range(n_s2):
            o_ref[:, a, sl] = y[a * l1:(a + 1) * l1]


def _fourier(u, fw, l1, l2):
    B, L, width = u.shape
    k = min(FOURIER_S2_PER_STEP, l1)
    n_s2 = min(FOURIER_S2_PER_STEP, l2)
    c2, s2 = _dft_cos_sin(l2)
    f1 = jnp.asarray(np.concatenate([c2, -s2], axis=0), BF16)
    c1, s1 = _dft_cos_sin(l1)
    f2 = jnp.asarray(np.block([[c1, s1], [-s1, c1]]), BF16)
    cc, sc = _dft_cos_sin(GROUP_CH)
    fc = jnp.asarray(np.concatenate([cc, sc], axis=0), BF16)
    ang = 2.0 * np.pi * ((np.arange(l2)[:, None] * np.arange(l1)[None, :]) % L) / L
    tw = lambda t: jnp.asarray(t.reshape(l2, l1 // k, k).transpose(1, 0, 2), F32)
    twc, tws = tw(np.cos(ang)), tw(np.sin(ang))

    x = u.reshape(B, l2, l1 * width)
    col = lambda b, j: (b, 0, j)
    const2 = lambda b, j: (0, 0)
    zr, zi = pl.pallas_call(
        functools.partial(_fourier_a_kernel, l2=l2, k=k, width=width),
        grid=(B, l1 // k),
        in_specs=[
            pl.BlockSpec((None, l2, k * width), col),
            pl.BlockSpec(f1.shape, const2),
            pl.BlockSpec((None, l2, k), lambda b, j: (j, 0, 0)),
            pl.BlockSpec((None, l2, k), lambda b, j: (j, 0, 0)),
        ],
        out_specs=[pl.BlockSpec((None, l2, k * width), col)] * 2,
        out_shape=[jax.ShapeDtypeStruct((B, l2, l1 * width), BF16)] * 2,
        compiler_params=_params("parallel", "parallel"),
        name="fourier_a",
    )(x, f1, twc, tws)

    zr = zr.reshape(B, L, width)
    zi = zi.reshape(B, L, width)
    rows = lambda b, j: (b, j, 0)
    out = pl.pallas_call(
        functools.partial(_fourier_b_kernel, l1=l1, n_s2=n_s2, norm=1.0 / math.sqrt(L * GROUP_CH)),
        grid=(B, l2 // n_s2),
        in_specs=[
            pl.BlockSpec((None, n_s2 * l1, width), rows),
            pl.BlockSpec((None, n_s2 * l1, width), rows),
            pl.BlockSpec(f2.shape, const2),
            pl.BlockSpec(fc.shape, const2),
            pl.BlockSpec(fw.shape, lambda b, j: (0, 0, 0)),
        ],
        out_specs=pl.BlockSpec((None, l1, n_s2, width), lambda b, j: (b, 0, j, 0)),
        out_shape=jax.ShapeDtypeStruct((B, l1, l2, width), F32),
        scratch_shapes=[pltpu.VMEM((n_s2 * l1, width), BF16)] * 2,
        compiler_params=_params("parallel", "parallel"),
        name="fourier_b",
    )(zr, zi, f2, fc, fw)
    return out.reshape(B, L, width)


def _mix_out_kernel(up_ref, uc_ref, un_ref, attn_ref, four_ref, x_ref, wo_ref, pw_ref, ps_ref,
                    g_ref, o_ref, ext_ref, *, tm, seq_len, widths):
    pool_w, attn_w, four_w = widths
    i = pl.program_id(1)
    last = pl.num_programs(1) - 1
    ext_ref[0:POOL_HALO, :] = jnp.where(i > 0, up_ref[...], 0.0)
    ext_ref[POOL_HALO:POOL_HALO + tm, :] = uc_ref[...]
    ext_ref[POOL_HALO + tm:, :] = jnp.where(i < last, un_ref[...], 0.0)

    t = i * tm + lax.broadcasted_iota(jnp.int32, (tm, 1), 0)
    m = jnp.zeros((tm, o_ref.shape[-1]), F32)
    for g, w in enumerate(POOL_WINDOWS):
        sl = slice(g * GROUP_CH, (g + 1) * GROUP_CH)
        lo = jnp.maximum(t - w // 2, 0)
        hi = jnp.minimum(t + (w - 1 - w // 2), seq_len - 1)
        cnt = (hi - lo + 1).astype(F32)
        win = ext_ref[POOL_HALO - w // 2:POOL_HALO - w // 2 + tm, sl]
        for d in range(1 - w // 2, w - w // 2):
            win = win + ext_ref[POOL_HALO + d:POOL_HALO + d + tm, sl]
        dg = win / cnt - uc_ref[:, sl]
        yg = jnp.dot(dg.astype(BF16), pw_ref[g], preferred_element_type=F32) * ps_ref[:, sl]
        m = m + jnp.dot(yg.astype(BF16), wo_ref[sl, :], preferred_element_type=F32)
    m = m + jnp.dot(attn_ref[...], wo_ref[pool_w:pool_w + attn_w, :], preferred_element_type=F32)
    m = m + jnp.dot(four_ref[...].astype(BF16), wo_ref[pool_w + attn_w:, :],
                    preferred_element_type=F32)
    o_ref[...] = x_ref[...] + _rms(m, g_ref[...])


def _mix_out(x, u_pool, attn, four, wo, pw, ps, g, tm):
    B, L, D = x.shape
    widths = (u_pool.shape[-1], attn.shape[-1], four.shape[-1])
    hb = tm // POOL_HALO
    n_hb = L // POOL_HALO
    row = lambda b, i: (b, i, 0)
    const2 = lambda b, i: (0, 0)
    kern = functools.partial(_mix_out_kernel, tm=tm, seq_len=L, widths=widths)
    return pl.pallas_call(
        kern,
        grid=(B, L // tm),
        in_specs=[
            pl.BlockSpec((None, POOL_HALO, widths[0]), lambda b, i: (b, jnp.maximum(i * hb - 1, 0), 0)),
            pl.BlockSpec((None, tm, widths[0]), row),
            pl.BlockSpec((None, POOL_HALO, widths[0]),
                         lambda b, i: (b, jnp.minimum((i + 1) * hb, n_hb - 1), 0)),
            pl.BlockSpec((None, tm, widths[1]), row),
            pl.BlockSpec((None, tm, widths[2]), row),
            pl.BlockSpec((None, tm, D), row),
            pl.BlockSpec(wo.shape, const2),
            pl.BlockSpec(pw.shape, lambda b, i: (0, 0, 0)),
            pl.BlockSpec((1, widths[0]), const2),
            pl.BlockSpec((1, D), const2),
        ],
        out_specs=pl.BlockSpec((None, tm, D), row),
        out_shape=jax.ShapeDtypeStruct((B, L, D), F32),
        scratch_shapes=[pltpu.VMEM((tm + 2 * POOL_HALO, widths[0]), F32)],
        compiler_params=_params("parallel", "parallel"),
        name="mix_out",
    )(u_pool, u_pool, u_pool, attn, four, x, wo, pw, ps, g)


def _gelu_tanh(x):
    return 0.5 * x * (1.0 + jnp.tanh(math.sqrt(2.0 / math.pi) * (x + 0.044715 * (x * x * x))))


def _ffn_kernel(xp_ref, x_ref, xn_ref, gpre_ref, wg_ref, wv_ref, cwg_ref, cwv_ref, cbg_ref, cbv_ref,
                wd_ref, gpost_ref, o_ref, h_ref, acc_ref, ug_ref, uv_ref, *, tm, sub):
    i = pl.program_id(1)
    j = pl.program_id(2)
    rows = tm + 2 * CONV_HALO

    @pl.when(j == 0)
    def _():
        g = gpre_ref[...]
        hp = jnp.where(i > 0, _rms(xp_ref[...], g), 0.0)
        hn = jnp.where(i < pl.num_programs(1) - 1, _rms(xn_ref[...], g), 0.0)
        h_ref[0:CONV_HALO, :] = hp.astype(BF16)
        h_ref[CONV_HALO:CONV_HALO + tm, :] = _rms(x_ref[...], g).astype(BF16)
        h_ref[CONV_HALO + tm:, :] = hn.astype(BF16)
        acc_ref[...] = jnp.zeros(acc_ref.shape, F32)

    def conv(u_ref, cw_ref, cb_ref):
        prev = u_ref[CONV_HALO - 1:CONV_HALO - 1 + tm, :]
        cur = u_ref[CONV_HALO:CONV_HALO + tm, :]
        nxt = u_ref[CONV_HALO + 1:CONV_HALO + 1 + tm, :]
        return prev * cw_ref[0:1, :] + cur * cw_ref[1:2, :] + nxt * cw_ref[2:3, :] + cb_ref[...]

    h = h_ref[...]
    n_sub = wg_ref.shape[1] // sub
    for s in range(n_sub):
        sl = slice(s * sub, (s + 1) * sub)
        ug_ref[s] = jnp.dot(h, wg_ref[:, sl], preferred_element_type=F32)
        uv_ref[s] = jnp.dot(h, wv_ref[:, sl], preferred_element_type=F32)
    for s in range(n_sub):
        sl = slice(s * sub, (s + 1) * sub)
        gate = conv(ug_ref.at[s], cwg_ref.at[:, sl], cbg_ref.at[:, sl])
        val = conv(uv_ref.at[s], cwv_ref.at[:, sl], cbv_ref.at[:, sl])
        act = (_gelu_tanh(gate) * val).astype(BF16)
        acc_ref[...] += jnp.dot(act, wd_ref[sl, :], preferred_element_type=F32)

    @pl.when(j == pl.num_programs(2) - 1)
    def _():
        o_ref[...] = x_ref[...] + _rms(acc_ref[...], gpost_ref[...])


def _ffn(x, gpre, w_up, conv_w, conv_b, w_down, gpost, tm, chunk):
    B, L, D = x.shape
    d_ff = w_down.shape[0]
    nc = d_ff // chunk
    hb = tm // CONV_HALO
    n_hb = L // CONV_HALO
    const2 = lambda b, i, j: (0, 0)
    gate_col = lambda b, i, j: (0, j)
    val_col = lambda b, i, j: (0, nc + j)
    sub = min(FFN_SUB_CHUNK, chunk)
    kern = functools.partial(_ffn_kernel, tm=tm, sub=sub)
    return pl.pallas_call(
        kern,
        grid=(B, L // tm, nc),
        in_specs=[
            pl.BlockSpec((None, CONV_HALO, D), lambda b, i, j: (b, jnp.maximum(i * hb - 1, 0), 0)),
            pl.BlockSpec((None, tm, D), lambda b, i, j: (b, i, 0)),
            pl.BlockSpec((None, CONV_HALO, D),
                         lambda b, i, j: (b, jnp.minimum((i + 1) * hb, n_hb - 1), 0)),
            pl.BlockSpec((1, D), const2),
            pl.BlockSpec((D, chunk), gate_col),
            pl.BlockSpec((D, chunk), val_col),
            pl.BlockSpec((3, chunk), gate_col),
            pl.BlockSpec((3, chunk), val_col),
            pl.BlockSpec((1, chunk), gate_col),
            pl.BlockSpec((1, chunk), val_col),
            pl.BlockSpec((chunk, D), lambda b, i, j: (j, 0)),
            pl.BlockSpec((1, D), const2),
        ],
        out_specs=pl.BlockSpec((None, tm, D), lambda b, i, j: (b, i, 0)),
        out_shape=jax.ShapeDtypeStruct((B, L, D), F32),
        scratch_shapes=[
            pltpu.VMEM((tm + 2 * CONV_HALO, D), BF16),
            pltpu.VMEM((tm, D), F32),
            pltpu.VMEM((chunk // sub, tm + 2 * CONV_HALO, sub), F32),
            pltpu.VMEM((chunk // sub, tm + 2 * CONV_HALO, sub), F32),
        ],
        compiler_params=_params("parallel", "parallel", "arbitrary"),
        name="ffn",
    )(x, x, x, gpre, w_up, w_up, conv_w, conv_w, conv_b, conv_b, w_down, gpost)


def _rope_tables(seq_len):
    quarter = HEAD_DIM // 4
    t = jnp.arange(seq_len, dtype=jnp.int32)
    row = (t // GRID_W).astype(F32)
    col = (t % GRID_W).astype(F32)
    inv_freq = 1.0 / (ROPE_THETA ** (jnp.arange(quarter, dtype=F32) / quarter))
    ang = jnp.concatenate([row[:, None] * inv_freq[None, :], col[:, None] * inv_freq[None, :]], axis=-1)
    cos, sin = jnp.cos(ang), jnp.sin(ang)
    return jnp.concatenate([cos, cos], axis=-1), jnp.concatenate([-sin, sin], axis=-1)


def _fourier_split(seq_len):
    l2 = 1 << (int(math.log2(seq_len)) // 2)
    return seq_len // l2, l2


def _trunk(x, layers):
    B, L, D = x.shape
    ts = _tiles(L)
    cc, ss = _rope_tables(L)
    l1, l2 = _fourier_split(L)
    for p in layers:
        u_pool, qt, k, vt, u_four = _in_proj(x, p["g_pre_mix"], p["w_in"], cc, ss, p["q_norm"],
                                             p["k_norm"], p["widths"], ts["tm_proj"])
        attn = _attention(qt, k, vt, ts["tq"])
        four = _fourier(u_four, p["fourier_w"], l1, l2)
        x = _mix_out(x, u_pool, attn, four, p["w_out"], p["pool_w"], p["pool_scale"],
                     p["g_post_mix"], ts["tm_mix"])
        x = _ffn(x, p["g_pre_ffn"], p["w_up"], p["conv_w"], p["conv_b"], p["w_down"],
                 p["g_post_ffn"], ts["tm_ffn"], ts["ff_chunk"])
    return x


def kernel(x_prompt, x_sample, g_pre_mix, g_post_mix, w_in, pool_w, pool_scale, q_norm, k_norm,
           fourier_w, w_out, g_pre_ffn, g_post_ffn, w_up, conv_w, conv_b, w_down):
    depth = w_in.shape[0]
    pool_width = pool_scale.shape[-1]
    four_width = fourier_w.shape[1] * fourier_w.shape[2]
    kv_width = N_KV_HEADS * HEAD_DIM
    q_width = w_in.shape[-1] - pool_width - four_width - 2 * kv_width
    layers = []
    for l in range(depth):
        layers.append(dict(
            widths=(pool_width, q_width, kv_width, four_width),
            g_pre_mix=g_pre_mix[l][None, :], g_post_mix=g_post_mix[l][None, :],
            w_in=w_in[l].astype(BF16), pool_w=pool_w[l].astype(BF16),
            pool_scale=pool_scale[l][None, :], q_norm=q_norm[l][None, :], k_norm=k_norm[l][None, :],
            fourier_w=fourier_w[l].astype(BF16), w_out=w_out[l].astype(BF16),
            g_pre_ffn=g_pre_ffn[l][None, :], g_post_ffn=g_post_ffn[l][None, :],
            w_up=w_up[l].astype(BF16), conv_w=conv_w[l], conv_b=conv_b[l][None, :],
            w_down=w_down[l].astype(BF16)))
    return _trunk(x_prompt, layers), _trunk(x_sample, layers)
```

```python
import functools
import math

import jax
import jax.numpy as jnp
import numpy as np
from jax import lax
from jax.experimental import pallas as pl
from jax.experimental.pallas import tpu as pltpu

F32 = jnp.float32
BF16 = jnp.bfloat16

NORM_EPS = 1e-6
GRID_W = 64
HEAD_DIM = 128
N_KV_HEADS = 2
GQA_GROUP = 4
POOL_WINDOWS = (2, 4, 8, 16)
POOL_HALO = 8
GROUP_CH = 128
N_GROUPS = 4
ROPE_THETA = 10000.0
CONV_HALO = 16
FOURIER_S2_PER_STEP = 8
FFN_SUB_CHUNK = 256
V_ONES_ROWS = 16

V7X_VMEM_BYTES = 64 * 1024 * 1024
VMEM_LIMIT = V7X_VMEM_BYTES - 8 * 1024 * 1024


def _tiles(seq_len):
    return dict(
        tm_proj=min(512, seq_len),
        tm_mix=min(512, seq_len),
        tm_ffn=min(512, seq_len),
        ff_chunk=512,
        tq=min(512, seq_len),
    )


def _params(*sem):
    return pltpu.CompilerParams(dimension_semantics=sem, vmem_limit_bytes=VMEM_LIMIT)


def _rms(x, g):
    return x * lax.rsqrt(jnp.mean(x * x, axis=-1, keepdims=True) + NORM_EPS) * g


def _in_proj_kernel(x_ref, g_ref, w_ref, cc_ref, ss_ref, qn_ref, kn_ref,
                    pool_ref, q_ref, k_ref, v_ref, four_ref, *, widths, scale):
    pool_w, q_w, kv_w, four_w = widths
    h = _rms(x_ref[...], g_ref[...]).astype(BF16)
    cc = cc_ref[...]
    ss = ss_ref[...]

    def seg(lo, width):
        return jnp.dot(h, w_ref[:, lo:lo + width], preferred_element_type=F32)

    def norm_rope(zh, gain, out_scale):
        y = zh * lax.rsqrt(jnp.mean(zh * zh, axis=-1, keepdims=True) + NORM_EPS) * gain
        y = y * cc + pltpu.roll(y, HEAD_DIM // 2, axis=1) * ss
        return y * out_scale

    pair = 2 * HEAD_DIM
    for p in range(q_w // pair):
        z = seg(pool_w + p * pair, pair)
        for s in range(2):
            zh = z[:, s * HEAD_DIM:(s + 1) * HEAD_DIM]
            q_ref[2 * p + s] = norm_rope(zh, qn_ref[...], scale).T.astype(BF16)
    z = seg(pool_w + q_w, kv_w)
    for s in range(kv_w // HEAD_DIM):
        zh = z[:, s * HEAD_DIM:(s + 1) * HEAD_DIM]
        k_ref[:, s * HEAD_DIM:(s + 1) * HEAD_DIM] = norm_rope(zh, kn_ref[...], 1.0).astype(BF16)
    z = seg(pool_w + q_w + kv_w, kv_w)
    for s in range(kv_w // HEAD_DIM):
        v_ref[s, 0:HEAD_DIM, :] = z[:, s * HEAD_DIM:(s + 1) * HEAD_DIM].T.astype(BF16)
        v_ref[s, HEAD_DIM:, :] = jnp.ones((V_ONES_ROWS, z.shape[0]), BF16)
    four_ref[...] = seg(pool_w + q_w + 2 * kv_w, four_w).astype(BF16)
    pool_ref[...] = seg(0, pool_w)


def _in_proj(x, g, w, cc, ss, qn, kn, widths, tm):
    B, L, D = x.shape
    pool_w, q_w, kv_w, four_w = widths
    nt = L // tm
    row = lambda b, i: (b, i, 0)
    const2 = lambda b, i: (0, 0)
    n_q = q_w // HEAD_DIM
    v_rows = HEAD_DIM + V_ONES_ROWS
    kern = functools.partial(_in_proj_kernel, widths=widths, scale=HEAD_DIM ** -0.5 * math.log2(math.e))
    return pl.pallas_call(
        kern,
        grid=(B, nt),
        in_specs=[
            pl.BlockSpec((None, tm, D), row),
            pl.BlockSpec((1, D), const2),
            pl.BlockSpec(w.shape, const2),
            pl.BlockSpec((tm, HEAD_DIM), lambda b, i: (i, 0)),
            pl.BlockSpec((tm, HEAD_DIM), lambda b, i: (i, 0)),
            pl.BlockSpec((1, HEAD_DIM), const2),
            pl.BlockSpec((1, HEAD_DIM), const2),
        ],
        out_specs=[
            pl.BlockSpec((None, tm, pool_w), row),
            pl.BlockSpec((None, n_q, HEAD_DIM, tm), lambda b, i: (b, 0, 0, i)),
            pl.BlockSpec((None, tm, kv_w), row),
            pl.BlockSpec((None, N_KV_HEADS, None, v_rows, tm), lambda b, i: (b, 0, i, 0, 0)),
            pl.BlockSpec((None, tm, four_w), row),
        ],
        out_shape=[
            jax.ShapeDtypeStruct((B, L, pool_w), F32),
            jax.ShapeDtypeStruct((B, n_q, HEAD_DIM, L), BF16),
            jax.ShapeDtypeStruct((B, L, kv_w), BF16),
            jax.ShapeDtypeStruct((B, N_KV_HEADS, nt, v_rows, tm), BF16),
            jax.ShapeDtypeStruct((B, L, four_w), BF16),
        ],
        compiler_params=_params("parallel", "parallel"),
        name="in_proj",
    )(x, g, w, cc, ss, qn, kn)


def _attn_kernel(qt_ref, k_ref, vt_ref, o_ref, m_ref, acc_ref, sa_ref, sb_ref, ma_ref, mb_ref,
                 *, tk, nk):
    assert nk == 1 or nk % 2 == 0
    m_ref[...] = jnp.full(m_ref.shape, -jnp.inf, F32)
    acc_ref[...] = jnp.zeros(acc_ref.shape, F32)

    def scores(c, s_ref, cm_ref):
        kc = k_ref[pl.ds(pl.multiple_of(c * tk, tk), tk), :]
        for g in range(GQA_GROUP):
            st = jnp.dot(kc, qt_ref[g], preferred_element_type=F32)
            s_ref[g] = st
            cm_ref[g] = jnp.max(st, axis=0, keepdims=True)

    def accumulate(c, s_ref, cm_ref):
        vc = vt_ref[c]
        for g in range(GQA_GROUP):
            m_old = m_ref[g]
            m_new = jnp.maximum(m_old, cm_ref[g])
            alpha = jnp.exp2(m_old - m_new)
            p = jnp.exp2(s_ref[g] - m_new).astype(BF16)
            acc_ref[g] = alpha * acc_ref[g] + jnp.dot(vc, p, preferred_element_type=F32)
            m_ref[g] = m_new

    scores(0, sa_ref, ma_ref)
    if nk > 1:
        def pair(i, carry):
            scores(2 * i + 1, sb_ref, mb_ref)
            accumulate(2 * i, sa_ref, ma_ref)
            scores(2 * i + 2, sa_ref, ma_ref)
            accumulate(2 * i + 1, sb_ref, mb_ref)
            return carry

        lax.fori_loop(0, nk // 2 - 1, pair, 0)
        scores(nk - 1, sb_ref, mb_ref)
        accumulate(nk - 2, sa_ref, ma_ref)
        accumulate(nk - 1, sb_ref, mb_ref)
    else:
        accumulate(0, sa_ref, ma_ref)
    for g in range(GQA_GROUP):
        a = acc_ref[g]
        o = a[:HEAD_DIM] / a[HEAD_DIM:HEAD_DIM + 1]
        o_ref[:, g * HEAD_DIM:(g + 1) * HEAD_DIM] = o.T.astype(o_ref.dtype)


def _attention(qt, k, vt, tq):
    B, n_q, _, L = qt.shape
    _, _, nk, v_rows, tk = vt.shape
    gw = GQA_GROUP * HEAD_DIM
    kern = functools.partial(_attn_kernel, tk=tk, nk=nk)
    return pl.pallas_call(
        kern,
        grid=(B, N_KV_HEADS, L // tq),
        in_specs=[
            pl.BlockSpec((None, GQA_GROUP, HEAD_DIM, tq), lambda b, j, i: (b, j, 0, i)),
            pl.BlockSpec((None, L, HEAD_DIM), lambda b, j, i: (b, 0, j)),
            pl.BlockSpec((None, None, nk, v_rows, tk), lambda b, j, i: (b, j, 0, 0, 0)),
        ],
        out_specs=pl.BlockSpec((None, tq, gw), lambda b, j, i: (b, i, j)),
        out_shape=jax.ShapeDtypeStruct((B, L, n_q * HEAD_DIM), BF16),
        scratch_shapes=[
            pltpu.VMEM((GQA_GROUP, 1, tq), F32),
            pltpu.VMEM((GQA_GROUP, v_rows, tq), F32),
            pltpu.VMEM((GQA_GROUP, tk, tq), F32),
            pltpu.VMEM((GQA_GROUP, tk, tq), F32),
            pltpu.VMEM((GQA_GROUP, 1, tq), F32),
            pltpu.VMEM((GQA_GROUP, 1, tq), F32),
        ],
        compiler_params=_params("parallel", "parallel", "arbitrary"),
        name="attention",
    )(qt, k, vt)


def _dft_cos_sin(n):
    idx = np.arange(n)
    ang = 2.0 * np.pi * ((idx[:, None] * idx[None, :]) % n) / n
    return np.cos(ang), np.sin(ang)


def _fourier_a_kernel(x_ref, f1_ref, twc_ref, tws_ref, zr_ref, zi_ref, *, l2, k, width):
    z = jnp.dot(f1_ref[...], x_ref[...], preferred_element_type=F32)
    zr = z[:l2]
    zi = z[l2:]
    for t in range(k):
        c = twc_ref[:, t:t + 1]
        s = tws_ref[:, t:t + 1]
        a = zr[:, t * width:(t + 1) * width]
        b = zi[:, t * width:(t + 1) * width]
        zr_ref[:, t * width:(t + 1) * width] = (a * c + b * s).astype(zr_ref.dtype)
        zi_ref[:, t * width:(t + 1) * width] = (b * c - a * s).astype(zi_ref.dtype)


def _fourier_b_kernel(zr_ref, zi_ref, f2_ref, fc_ref, fw_ref, o_ref, hr_ref, hi_ref, *, l1, n_s2, norm):
    for a in range(n_s2):
        rows = slice(a * l1, (a + 1) * l1)
        zz = jnp.concatenate([zr_ref[rows, :], zi_ref[rows, :]], axis=0)
        hh = jnp.dot(f2_ref[...], zz, preferred_element_type=F32)
        hr_ref[rows, :] = hh[:l1].astype(BF16)
        hi_ref[rows, :] = hh[l1:].astype(BF16)
    for g in range(N_GROUPS):
        sl = slice(g * GROUP_CH, (g + 1) * GROUP_CH)
        hg = jnp.concatenate([hr_ref[:, sl], hi_ref[:, sl]], axis=1)
        f = jnp.dot(hg, fc_ref[...], preferred_element_type=F32) * norm
        y = jnp.dot(f.astype(BF16), fw_ref[g], preferred_element_type=F32)
        for a in range(n_s2):
            o_ref[:, a, sl] = y[a * l1:(a + 1) * l1]


def _fourier(u, fw, l1, l2):
    B, L, width = u.shape
    k = min(FOURIER_S2_PER_STEP, l1)
    n_s2 = min(FOURIER_S2_PER_STEP, l2)
    c2, s2 = _dft_cos_sin(l2)
    f1 = jnp.asarray(np.concatenate([c2, -s2], axis=0), BF16)
    c1, s1 = _dft_cos_sin(l1)
    f2 = jnp.asarray(np.block([[c1, s1], [-s1, c1]]), BF16)
    cc, sc = _dft_cos_sin(GROUP_CH)
    fc = jnp.asarray(np.concatenate([cc, sc], axis=0), BF16)
    ang = 2.0 * np.pi * ((np.arange(l2)[:, None] * np.arange(l1)[None, :]) % L) / L
    tw = lambda t: jnp.asarray(t.reshape(l2, l1 // k, k).transpose(1, 0, 2), F32)
    twc, tws = tw(np.cos(ang)), tw(np.sin(ang))

    x = u.reshape(B, l2, l1 * width)
    col = lambda b, j: (b, 0, j)
    const2 = lambda b, j: (0, 0)
    zr, zi = pl.pallas_call(
        functools.partial(_fourier_a_kernel, l2=l2, k=k, width=width),
        grid=(B, l1 // k),
        in_specs=[
            pl.BlockSpec((None, l2, k * width), col),
            pl.BlockSpec(f1.shape, const2),
            pl.BlockSpec((None, l2, k), lambda b, j: (j, 0, 0)),
            pl.BlockSpec((None, l2, k), lambda b, j: (j, 0, 0)),
        ],
        out_specs=[pl.BlockSpec((None, l2, k * width), col)] * 2,
        out_shape=[jax.ShapeDtypeStruct((B, l2, l1 * width), BF16)] * 2,
        compiler_params=_params("parallel", "parallel"),
        name="fourier_a",
    )(x, f1, twc, tws)

    zr = zr.reshape(B, L, width)
    zi = zi.reshape(B, L, width)
    rows = lambda b, j: (b, j, 0)
    out = pl.pallas_call(
        functools.partial(_fourier_b_kernel, l1=l1, n_s2=n_s2, norm=1.0 / math.sqrt(L * GROUP_CH)),
        grid=(B, l2 // n_s2),
        in_specs=[
            pl.BlockSpec((None, n_s2 * l1, width), rows),
            pl.BlockSpec((None, n_s2 * l1, width), rows),
            pl.BlockSpec(f2.shape, const2),
            pl.BlockSpec(fc.shape, const2),
            pl.BlockSpec(fw.shape, lambda b, j: (0, 0, 0)),
        ],
        out_specs=pl.BlockSpec((None, l1, n_s2, width), lambda b, j: (b, 0, j, 0)),
        out_shape=jax.ShapeDtypeStruct((B, l1, l2, width), F32),
        scratch_shapes=[pltpu.VMEM((n_s2 * l1, width), BF16)] * 2,
        compiler_params=_params("parallel", "parallel"),
        name="fourier_b",
    )(zr, zi, f2, fc, fw)
    return out.reshape(B, L, width)


def _mix_out_kernel(up_ref, uc_ref, un_ref, attn_ref, four_ref, x_ref, wo_ref, pw_ref, ps_ref,
                    g_ref, o_ref, ext_ref, m_ref, pooled_ref, *, tm, seq_len, widths):
    pool_w, attn_w, four_w = widths
    i = pl.program_id(1)
    last = pl.num_programs(1) - 1
    ext_ref[0:POOL_HALO, :] = jnp.where(i > 0, up_ref[...], 0.0)
    ext_ref[POOL_HALO:POOL_HALO + tm, :] = uc_ref[...]
    ext_ref[POOL_HALO + tm:, :] = jnp.where(i < last, un_ref[...], 0.0)

    halves = [slice(r * (tm // 2), (r + 1) * (tm // 2)) for r in range(2)]
    t = i * tm + lax.broadcasted_iota(jnp.int32, (tm, 1), 0)

    def project(rows):
        m_ref[rows, :] = (
            jnp.dot(attn_ref[rows, :], wo_ref[pool_w:pool_w + attn_w, :], preferred_element_type=F32)
            + jnp.dot(four_ref[rows, :].astype(BF16), wo_ref[pool_w + attn_w:, :],
                      preferred_element_type=F32))

    def pool(g):
        w = POOL_WINDOWS[g]
        sl = slice(g * GROUP_CH, (g + 1) * GROUP_CH)
        lo = jnp.maximum(t - w // 2, 0)
        hi = jnp.minimum(t + (w - 1 - w // 2), seq_len - 1)
        cnt = (hi - lo + 1).astype(F32)
        win = ext_ref[POOL_HALO - w // 2:POOL_HALO - w // 2 + tm, sl]
        for d in range(1 - w // 2, w - w // 2):
            win = win + ext_ref[POOL_HALO + d:POOL_HALO + d + tm, sl]
        dg = win / cnt - uc_ref[:, sl]
        pooled_ref[:, sl] = dg.astype(BF16)

    for g in range(len(POOL_WINDOWS)):
        pool(g)
    for rows in halves:
        project(rows)
    for g in range(len(POOL_WINDOWS)):
        sl = slice(g * GROUP_CH, (g + 1) * GROUP_CH)
        yg = jnp.dot(pooled_ref[:, sl], pw_ref[g], preferred_element_type=F32) * ps_ref[:, sl]
        pooled_ref[:, sl] = yg.astype(BF16)
    for rows in halves:
        m = m_ref[rows, :] + jnp.dot(pooled_ref[rows, :], wo_ref[0:pool_w, :],
                                     preferred_element_type=F32)
        o_ref[rows, :] = x_ref[rows, :] + _rms(m, g_ref[...])


def _mix_out(x, u_pool, attn, four, wo, pw, ps, g, tm):
    B, L, D = x.shape
    widths = (u_pool.shape[-1], attn.shape[-1], four.shape[-1])
    hb = tm // POOL_HALO
    n_hb = L // POOL_HALO
    row = lambda b, i: (b, i, 0)
    const2 = lambda b, i: (0, 0)
    kern = functools.partial(_mix_out_kernel, tm=tm, seq_len=L, widths=widths)
    return pl.pallas_call(
        kern,
        grid=(B, L // tm),
        in_specs=[
            pl.BlockSpec((None, POOL_HALO, widths[0]), lambda b, i: (b, jnp.maximum(i * hb - 1, 0), 0)),
            pl.BlockSpec((None, tm, widths[0]), row),
            pl.BlockSpec((None, POOL_HALO, widths[0]),
                         lambda b, i: (b, jnp.minimum((i + 1) * hb, n_hb - 1), 0)),
            pl.BlockSpec((None, tm, widths[1]), row),
            pl.BlockSpec((None, tm, widths[2]), row),
            pl.BlockSpec((None, tm, D), row),
            pl.BlockSpec(wo.shape, const2),
            pl.BlockSpec(pw.shape, lambda b, i: (0, 0, 0)),
            pl.BlockSpec((1, widths[0]), const2),
            pl.BlockSpec((1, D), const2),
        ],
        out_specs=pl.BlockSpec((None, tm, D), row),
        out_shape=jax.ShapeDtypeStruct((B, L, D), F32),
        scratch_shapes=[
            pltpu.VMEM((tm + 2 * POOL_HALO, widths[0]), F32),
            pltpu.VMEM((tm, D), F32),
            pltpu.VMEM((tm, widths[0]), BF16),
        ],
        compiler_params=_params("parallel", "parallel"),
        name="mix_out",
    )(u_pool, u_pool, u_pool, attn, four, x, wo, pw, ps, g)


def _gelu_tanh(x):
    return 0.5 * x * (1.0 + jnp.tanh(math.sqrt(2.0 / math.pi) * (x + 0.044715 * (x * x * x))))


def _ffn_kernel(xp_ref, x_ref, xn_ref, gpre_ref, wg_ref, wv_ref, cwg_ref, cwv_ref, cbg_ref, cbv_ref,
                wd_ref, gpost_ref, o_ref, h_ref, acc_ref, ug_ref, uv_ref, *, tm, sub):
    i = pl.program_id(1)
    j = pl.program_id(2)
    rows = tm + 2 * CONV_HALO

    @pl.when(j == 0)
    def _():
        g = gpre_ref[...]
        hp = jnp.where(i > 0, _rms(xp_ref[...], g), 0.0)
        hn = jnp.where(i < pl.num_programs(1) - 1, _rms(xn_ref[...], g), 0.0)
        h_ref[0:CONV_HALO, :] = hp.astype(BF16)
        h_ref[CONV_HALO:CONV_HALO + tm, :] = _rms(x_ref[...], g).astype(BF16)
        h_ref[CONV_HALO + tm:, :] = hn.astype(BF16)
        acc_ref[...] = jnp.zeros(acc_ref.shape, F32)

    def conv(u_ref, cw_ref, cb_ref):
        prev = u_ref[CONV_HALO - 1:CONV_HALO - 1 + tm, :]
        cur = u_ref[CONV_HALO:CONV_HALO + tm, :]
        nxt = u_ref[CONV_HALO + 1:CONV_HALO + 1 + tm, :]
        return prev * cw_ref[0:1, :] + cur * cw_ref[1:2, :] + nxt * cw_ref[2:3, :] + cb_ref[...]

    h = h_ref[...]
    n_sub = wg_ref.shape[1] // sub
    for s in range(n_sub):
        sl = slice(s * sub, (s + 1) * sub)
        ug_ref[s] = jnp.dot(h, wg_ref[:, sl], preferred_element_type=F32)
        uv_ref[s] = jnp.dot(h, wv_ref[:, sl], preferred_element_type=F32)
    for s in range(n_sub):
        sl = slice(s * sub, (s + 1) * sub)
        gate = conv(ug_ref.at[s], cwg_ref.at[:, sl], cbg_ref.at[:, sl])
        val = conv(uv_ref.at[s], cwv_ref.at[:, sl], cbv_ref.at[:, sl])
        act = (_gelu_tanh(gate) * val).astype(BF16)
        acc_ref[...] += jnp.dot(act, wd_ref[sl, :], preferred_element_type=F32)

    @pl.when(j == pl.num_programs(2) - 1)
    def _():
        o_ref[...] = x_ref[...] + _rms(acc_ref[...], gpost_ref[...])


def _ffn(x, gpre, w_up, conv_w, conv_b, w_down, gpost, tm, chunk):
    B, L, D = x.shape
    d_ff = w_down.shape[0]
    nc = d_ff // chunk
    hb = tm // CONV_HALO
    n_hb = L // CONV_HALO
    const2 = lambda b, i, j: (0, 0)
    gate_col = lambda b, i, j: (0, j)
    val_col = lambda b, i, j: (0, nc + j)
    sub = min(FFN_SUB_CHUNK, chunk)
    kern = functools.partial(_ffn_kernel, tm=tm, sub=sub)
    return pl.pallas_call(
        kern,
        grid=(B, L // tm, nc),
        in_specs=[
            pl.BlockSpec((None, CONV_HALO, D), lambda b, i, j: (b, jnp.maximum(i * hb - 1, 0), 0)),
            pl.BlockSpec((None, tm, D), lambda b, i, j: (b, i, 0)),
            pl.BlockSpec((None, CONV_HALO, D),
                         lambda b, i, j: (b, jnp.minimum((i + 1) * hb, n_hb - 1), 0)),
            pl.BlockSpec((1, D), const2),
            pl.BlockSpec((D, chunk), gate_col),
            pl.BlockSpec((D, chunk), val_col),
            pl.BlockSpec((3, chunk), gate_col),
            pl.BlockSpec((3, chunk), val_col),
            pl.BlockSpec((1, chunk), gate_col),
            pl.BlockSpec((1, chunk), val_col),
            pl.BlockSpec((chunk, D), lambda b, i, j: (j, 0)),
            pl.BlockSpec((1, D), const2),
        ],
        out_specs=pl.BlockSpec((None, tm, D), lambda b, i, j: (b, i, 0)),
        out_shape=jax.ShapeDtypeStruct((B, L, D), F32),
        scratch_shapes=[
            pltpu.VMEM((tm + 2 * CONV_HALO, D), BF16),
            pltpu.VMEM((tm, D), F32),
            pltpu.VMEM((chunk // sub, tm + 2 * CONV_HALO, sub), F32),
            pltpu.VMEM((chunk // sub, tm + 2 * CONV_HALO, sub), F32),
        ],
        compiler_params=_params("parallel", "parallel", "arbitrary"),
        name="ffn",
    )(x, x, x, gpre, w_up, w_up, conv_w, conv_w, conv_b, conv_b, w_down, gpost)


def _rope_tables(seq_len):
    quarter = HEAD_DIM // 4
    t = jnp.arange(seq_len, dtype=jnp.int32)
    row = (t // GRID_W).astype(F32)
    col = (t % GRID_W).astype(F32)
    inv_freq = 1.0 / (ROPE_THETA ** (jnp.arange(quarter, dtype=F32) / quarter))
    ang = jnp.concatenate([row[:, None] * inv_freq[None, :], col[:, None] * inv_freq[None, :]], axis=-1)
    cos, sin = jnp.cos(ang), jnp.sin(ang)
    return jnp.concatenate([cos, cos], axis=-1), jnp.concatenate([-sin, sin], axis=-1)


def _fourier_split(seq_len):
    l2 = 1 << (int(math.log2(seq_len)) // 2)
    return seq_len // l2, l2


def _trunk(x, layers):
    B, L, D = x.shape
    ts = _tiles(L)
    cc, ss = _rope_tables(L)
    l1, l2 = _fourier_split(L)
    for p in layers:
        u_pool, qt, k, vt, u_four = _in_proj(x, p["g_pre_mix"], p["w_in"], cc, ss, p["q_norm"],
                                             p["k_norm"], p["widths"], ts["tm_proj"])
        attn = _attention(qt, k, vt, ts["tq"])
        four = _fourier(u_four, p["fourier_w"], l1, l2)
        x = _mix_out(x, u_pool, attn, four, p["w_out"], p["pool_w"], p["pool_scale"],
                     p["g_post_mix"], ts["tm_mix"])
        x = _ffn(x, p["g_pre_ffn"], p["w_up"], p["conv_w"], p["conv_b"], p["w_down"],
                 p["g_post_ffn"], ts["tm_ffn"], ts["ff_chunk"])
    return x


def kernel(x_prompt, x_sample, g_pre_mix, g_post_mix, w_in, pool_w, pool_scale, q_norm, k_norm,
           fourier_w, w_out, g_pre_ffn, g_post_ffn, w_up, conv_w, conv_b, w_down):
    depth = w_in.shape[0]
    pool_width = pool_scale.shape[-1]
    four_width = fourier_w.shape[1] * fourier_w.shape[2]
    kv_width = N_KV_HEADS * HEAD_DIM
    q_width = w_in.shape[-1] - pool_width - four_width - 2 * kv_width
    layers = []
    for l in range(depth):
        layers.append(dict(
            widths=(pool_width, q_width, kv_width, four_width),
            g_pre_mix=g_pre_mix[l][None, :], g_post_mix=g_post_mix[l][None, :],
            w_in=w_in[l].astype(BF16), pool_w=pool_w[l].astype(BF16),
            pool_scale=pool_scale[l][None, :], q_norm=q_norm[l][None, :], k_norm=k_norm[l][None, :],
            fourier_w=fourier_w[l].astype(BF16), w_out=w_out[l].astype(BF16),
            g_pre_ffn=g_pre_ffn[l][None, :], g_post_ffn=g_post_ffn[l][None, :],
            w_up=w_up[l].astype(BF16), conv_w=conv_w[l], conv_b=conv_b[l][None, :],
            w_down=w_down[l].astype(BF16)))
    return _trunk(x_prompt, layers), _trunk(x_sample, layers)
```

```python
import functools
import math

import jax
import jax.numpy as jnp
import numpy as np
from jax import lax
from jax.experimental import pallas as pl
from jax.experimental.pallas import tpu as pltpu

F32 = jnp.float32
BF16 = jnp.bfloat16

NORM_EPS = 1e-6
GRID_W = 64
HEAD_DIM = 128
N_KV_HEADS = 2
GQA_GROUP = 4
POOL_WINDOWS = (2, 4, 8, 16)
POOL_HALO = 8
GROUP_CH = 128
N_GROUPS = 4
ROPE_THETA = 10000.0
CONV_HALO = 16
ATTN_CHUNKS_PER_TRIP = 4
FOURIER_S2_PER_STEP = 8
FFN_SUB_CHUNK = 256

V7X_VMEM_BYTES = 64 * 1024 * 1024
VMEM_LIMIT = V7X_VMEM_BYTES - 8 * 1024 * 1024


def _tiles(seq_len):
    return dict(
        tm_proj=min(512, seq_len),
        tm_mix=min(512, seq_len),
        tm_ffn=min(512, seq_len),
        ff_chunk=512,
        tq=min(512, seq_len),
    )


def _params(*sem):
    return pltpu.CompilerParams(dimension_semantics=sem, vmem_limit_bytes=VMEM_LIMIT)


def _rms(x, g):
    return x * lax.rsqrt(jnp.mean(x * x, axis=-1, keepdims=True) + NORM_EPS) * g


def _in_proj_kernel(x_ref, g_ref, w_ref, cc_ref, ss_ref, qn_ref, kn_ref,
                    pool_ref, q_ref, k_ref, v_ref, four_ref, *, widths, scale):
    pool_w, q_w, kv_w, four_w = widths
    h = _rms(x_ref[...], g_ref[...]).astype(BF16)
    cc = cc_ref[...]
    ss = ss_ref[...]

    def seg(lo, width):
        return jnp.dot(h, w_ref[:, lo:lo + width], preferred_element_type=F32)

    def norm_rope(zh, gain, out_scale):
        y = zh * lax.rsqrt(jnp.mean(zh * zh, axis=-1, keepdims=True) + NORM_EPS) * gain
        y = y * cc + pltpu.roll(y, HEAD_DIM // 2, axis=1) * ss
        return y * out_scale

    pair = 2 * HEAD_DIM
    for p in range(q_w // pair):
        z = seg(pool_w + p * pair, pair)
        for s in range(2):
            zh = z[:, s * HEAD_DIM:(s + 1) * HEAD_DIM]
            q_ref[2 * p + s] = norm_rope(zh, qn_ref[...], scale).T.astype(BF16)
    z = seg(pool_w + q_w, kv_w)
    for s in range(kv_w // HEAD_DIM):
        zh = z[:, s * HEAD_DIM:(s + 1) * HEAD_DIM]
        k_ref[:, s * HEAD_DIM:(s + 1) * HEAD_DIM] = norm_rope(zh, kn_ref[...], 1.0).astype(BF16)
    z = seg(pool_w + q_w + kv_w, kv_w)
    for s in range(kv_w // HEAD_DIM):
        v_ref[s] = z[:, s * HEAD_DIM:(s + 1) * HEAD_DIM].T.astype(BF16)
    four_ref[...] = seg(pool_w + q_w + 2 * kv_w, four_w)
    pool_ref[...] = seg(0, pool_w)


def _in_proj(x, g, w, cc, ss, qn, kn, widths, tm):
    B, L, D = x.shape
    pool_w, q_w, kv_w, four_w = widths
    nt = L // tm
    row = lambda b, i: (b, i, 0)
    const2 = lambda b, i: (0, 0)
    n_q = q_w // HEAD_DIM
    v_rows = HEAD_DIM
    kern = functools.partial(_in_proj_kernel, widths=widths, scale=HEAD_DIM ** -0.5 * math.log2(math.e))
    return pl.pallas_call(
        kern,
        grid=(B, nt),
        in_specs=[
            pl.BlockSpec((None, tm, D), row),
            pl.BlockSpec((1, D), const2),
            pl.BlockSpec(w.shape, const2),
            pl.BlockSpec((tm, HEAD_DIM), lambda b, i: (i, 0)),
            pl.BlockSpec((tm, HEAD_DIM), lambda b, i: (i, 0)),
            pl.BlockSpec((1, HEAD_DIM), const2),
            pl.BlockSpec((1, HEAD_DIM), const2),
        ],
        out_specs=[
            pl.BlockSpec((None, tm, pool_w), row),
            pl.BlockSpec((None, n_q, HEAD_DIM, tm), lambda b, i: (b, 0, 0, i)),
            pl.BlockSpec((None, tm, kv_w), row),
            pl.BlockSpec((None, N_KV_HEADS, None, v_rows, tm), lambda b, i: (b, 0, i, 0, 0)),
            pl.BlockSpec((None, tm, four_w), row),
        ],
        out_shape=[
            jax.ShapeDtypeStruct((B, L, pool_w), F32),
            jax.ShapeDtypeStruct((B, n_q, HEAD_DIM, L), BF16),
            jax.ShapeDtypeStruct((B, L, kv_w), BF16),
            jax.ShapeDtypeStruct((B, N_KV_HEADS, nt, v_rows, tm), BF16),
            jax.ShapeDtypeStruct((B, L, four_w), F32),
        ],
        compiler_params=_params("parallel", "parallel"),
        name="in_proj",
    )(x, g, w, cc, ss, qn, kn)


def _attn_kernel(qt_ref, k_ref, vt_ref, o_ref, m_ref, l_ref, acc_ref, sa_ref, sb_ref, ma_ref, mb_ref,
                 *, tk, nk, unroll):
    assert nk == 1 or (unroll % 2 == 0 and nk % unroll == 0)
    m_ref[...] = jnp.full(m_ref.shape, -jnp.inf, F32)
    l_ref[...] = jnp.zeros(l_ref.shape, F32)
    acc_ref[...] = jnp.zeros(acc_ref.shape, F32)

    def scores(c, g, s_ref, cm_ref):
        kc = k_ref[pl.ds(pl.multiple_of(c * tk, tk), tk), :]
        st = jnp.dot(kc, qt_ref[g], preferred_element_type=F32)
        s_ref[g] = st
        cm_ref[g] = jnp.max(st, axis=0, keepdims=True)

    def accumulate(c, g, s_ref, cm_ref):
        m_old = m_ref[g]
        m_new = jnp.maximum(m_old, cm_ref[g])
        alpha = jnp.exp2(m_old - m_new)
        p = jnp.exp2(s_ref[g] - m_new)
        l_ref[g] = alpha * l_ref[g] + jnp.sum(p, axis=0, keepdims=True)
        acc_ref[g] = alpha * acc_ref[g] + jnp.dot(vt_ref[c], p.astype(BF16),
                                                  preferred_element_type=F32)
        m_ref[g] = m_new

    def step(c_scores, s_next, m_next, c_acc, s_cur, m_cur):
        for g in range(GQA_GROUP):
            if c_scores is not None:
                scores(c_scores, g, s_next, m_next)
            if c_acc is not None:
                accumulate(c_acc, g, s_cur, m_cur)

    bufs = ((sa_ref, ma_ref), (sb_ref, mb_ref))
    step(0, *bufs[0], None, None, None)
    if nk > 1:
        def group(i, carry):
            for u in range(unroll):
                c = unroll * i + u
                step(c + 1, *bufs[(u + 1) % 2], c, *bufs[u % 2])
            return carry

        lax.fori_loop(0, nk // unroll - 1, group, 0)
        for c in range(nk - unroll, nk - 1):
            step(c + 1, *bufs[(c + 1) % 2], c, *bufs[c % 2])
    step(None, None, None, nk - 1, *bufs[(nk - 1) % 2])
    for g in range(GQA_GROUP):
        o = acc_ref[g] / l_ref[g]
        o_ref[:, g * HEAD_DIM:(g + 1) * HEAD_DIM] = o.T.astype(o_ref.dtype)


def _attention(qt, k, vt, tq):
    B, n_q, _, L = qt.shape
    _, _, nk, v_rows, tk = vt.shape
    gw = GQA_GROUP * HEAD_DIM
    unroll = ATTN_CHUNKS_PER_TRIP if nk % ATTN_CHUNKS_PER_TRIP == 0 and nk >= 4 * ATTN_CHUNKS_PER_TRIP else 2
    kern = functools.partial(_attn_kernel, tk=tk, nk=nk, unroll=unroll)
    return pl.pallas_call(
        kern,
        grid=(B, N_KV_HEADS, L // tq),
        in_specs=[
            pl.BlockSpec((None, GQA_GROUP, HEAD_DIM, tq), lambda b, j, i: (b, j, 0, i)),
            pl.BlockSpec((None, L, HEAD_DIM), lambda b, j, i: (b, 0, j)),
            pl.BlockSpec((None, None, nk, v_rows, tk), lambda b, j, i: (b, j, 0, 0, 0)),
        ],
        out_specs=pl.BlockSpec((None, tq, gw), lambda b, j, i: (b, i, j)),
        out_shape=jax.ShapeDtypeStruct((B, L, n_q * HEAD_DIM), BF16),
        scratch_shapes=[
            pltpu.VMEM((GQA_GROUP, 1, tq), F32),
            pltpu.VMEM((GQA_GROUP, 1, tq), F32),
            pltpu.VMEM((GQA_GROUP, v_rows, tq), F32),
            pltpu.VMEM((GQA_GROUP, tk, tq), F32),
            pltpu.VMEM((GQA_GROUP, tk, tq), F32),
            pltpu.VMEM((GQA_GROUP, 1, tq), F32),
            pltpu.VMEM((GQA_GROUP, 1, tq), F32),
        ],
        compiler_params=_params("parallel", "parallel", "arbitrary"),
        name="attention",
    )(qt, k, vt)


def _dft_cos_sin(n):
    idx = np.arange(n)
    ang = 2.0 * np.pi * ((idx[:, None] * idx[None, :]) % n) / n
    return np.cos(ang), np.sin(ang)


def _fourier_a_kernel(x_ref, f1_ref, twc_ref, tws_ref, zr_ref, zi_ref, *, l2, k):
    for t in range(k):
        z = jnp.dot(f1_ref[...], x_ref[:, t, :].astype(BF16), preferred_element_type=F32)
        a = z[:l2]
        b = z[l2:]
        c = twc_ref[:, t:t + 1]
        s = tws_ref[:, t:t + 1]
        zr_ref[:, t, :] = a * c + b * s
        zi_ref[:, t, :] = b * c - a * s


def _fourier_b_kernel(zr_ref, zi_ref, f2_ref, fc_ref, fw_ref, o_ref, hr_ref, hi_ref, *, l1, n_s2, norm):
    for a in range(n_s2):
        rows = slice(a * l1, (a + 1) * l1)
        zz = jnp.concatenate([zr_ref[rows, :].astype(BF16), zi_ref[rows, :].astype(BF16)], axis=0)
        hh = jnp.dot(f2_ref[...], zz, preferred_element_type=F32)
        hr_ref[rows, :] = hh[:l1].astype(BF16)
        hi_ref[rows, :] = hh[l1:].astype(BF16)
    for g in range(N_GROUPS):
        sl = slice(g * GROUP_CH, (g + 1) * GROUP_CH)
        hg = jnp.concatenate([hr_ref[:, sl], hi_ref[:, sl]], axis=1)
        f = jnp.dot(hg, fc_ref[...], preferred_element_type=F32) * norm
        y = jnp.dot(f.astype(BF16), fw_ref[g], preferred_element_type=F32)
        for a in range(n_s2):
            o_ref[:, a, sl] = y[a * l1:(a + 1) * l1]


def _fourier(u, fw, l1, l2):
    B, L, width = u.shape
    k = min(FOURIER_S2_PER_STEP, l1)
    n_s2 = min(FOURIER_S2_PER_STEP, l2)
    c2, s2 = _dft_cos_sin(l2)
    f1 = jnp.asarray(np.concatenate([c2, -s2], axis=0), BF16)
    c1, s1 = _dft_cos_sin(l1)
    f2 = jnp.asarray(np.block([[c1, s1], [-s1, c1]]), BF16)
    cc, sc = _dft_cos_sin(GROUP_CH)
    fc = jnp.asarray(np.concatenate([cc, sc], axis=0), BF16)
    ang = 2.0 * np.pi * ((np.arange(l2)[:, None] * np.arange(l1)[None, :]) % L) / L
    tw = lambda t: jnp.asarray(t.reshape(l2, l1 // k, k).transpose(1, 0, 2), F32)
    twc, tws = tw(np.cos(ang)), tw(np.sin(ang))

    x = u.reshape(B, l2, l1, width)
    col = lambda b, j: (b, 0, j, 0)
    const2 = lambda b, j: (0, 0)
    zr, zi = pl.pallas_call(
        functools.partial(_fourier_a_kernel, l2=l2, k=k),
        grid=(B, l1 // k),
        in_specs=[
            pl.BlockSpec((None, l2, k, width), col),
            pl.BlockSpec(f1.shape, const2),
            pl.BlockSpec((None, l2, k), lambda b, j: (j, 0, 0)),
            pl.BlockSpec((None, l2, k), lambda b, j: (j, 0, 0)),
        ],
        out_specs=[pl.BlockSpec((None, l2, k, width), col)] * 2,
        out_shape=[jax.ShapeDtypeStruct((B, l2, l1, width), F32)] * 2,
        compiler_params=_params("parallel", "parallel"),
        name="fourier_a",
    )(x, f1, twc, tws)

    zr = zr.reshape(B, L, width)
    zi = zi.reshape(B, L, width)
    rows = lambda b, j: (b, j, 0)
    out = pl.pallas_call(
        functools.partial(_fourier_b_kernel, l1=l1, n_s2=n_s2, norm=1.0 / math.sqrt(L * GROUP_CH)),
        grid=(B, l2 // n_s2),
        in_specs=[
            pl.BlockSpec((None, n_s2 * l1, width), rows),
            pl.BlockSpec((None, n_s2 * l1, width), rows),
            pl.BlockSpec(f2.shape, const2),
            pl.BlockSpec(fc.shape, const2),
            pl.BlockSpec(fw.shape, lambda b, j: (0, 0, 0)),
        ],
        out_specs=pl.BlockSpec((None, l1, n_s2, width), lambda b, j: (b, 0, j, 0)),
        out_shape=jax.ShapeDtypeStruct((B, l1, l2, width), F32),
        scratch_shapes=[pltpu.VMEM((n_s2 * l1, width), BF16)] * 2,
        compiler_params=_params("parallel", "parallel"),
        name="fourier_b",
    )(zr, zi, f2, fc, fw)
    return out.reshape(B, L, width)


def _mix_out_kernel(up_ref, uc_ref, un_ref, attn_ref, four_ref, x_ref, wo_ref, pw_ref, ps_ref,
                    g_ref, o_ref, ext_ref, m_ref, pooled_ref, *, tm, seq_len, widths):
    pool_w, attn_w, four_w = widths
    i = pl.program_id(1)
    last = pl.num_programs(1) - 1
    ext_ref[0:POOL_HALO, :] = jnp.where(i > 0, up_ref[...], 0.0)
    ext_ref[POOL_HALO:POOL_HALO + tm, :] = uc_ref[...]
    ext_ref[POOL_HALO + tm:, :] = jnp.where(i < last, un_ref[...], 0.0)

    halves = [slice(r * (tm // 2), (r + 1) * (tm // 2)) for r in range(2)]
    t = i * tm + lax.broadcasted_iota(jnp.int32, (tm, 1), 0)

    def project(rows):
        m_ref[rows, :] = (
            jnp.dot(attn_ref[rows, :], wo_ref[pool_w:pool_w + attn_w, :], preferred_element_type=F32)
            + jnp.dot(four_ref[rows, :].astype(BF16), wo_ref[pool_w + attn_w:, :],
                      preferred_element_type=F32))

    def pool(g):
        w = POOL_WINDOWS[g]
        sl = slice(g * GROUP_CH, (g + 1) * GROUP_CH)
        lo = jnp.maximum(t - w // 2, 0)
        hi = jnp.minimum(t + (w - 1 - w // 2), seq_len - 1)
        cnt = (hi - lo + 1).astype(F32)
        win = ext_ref[POOL_HALO - w // 2:POOL_HALO - w // 2 + tm, sl]
        for d in range(1 - w // 2, w - w // 2):
            win = win + ext_ref[POOL_HALO + d:POOL_HALO + d + tm, sl]
        dg = win / cnt - uc_ref[:, sl]
        pooled_ref[:, sl] = dg.astype(BF16)

    for g in range(len(POOL_WINDOWS)):
        pool(g)
    for rows in halves:
        project(rows)
    for g in range(len(POOL_WINDOWS)):
        sl = slice(g * GROUP_CH, (g + 1) * GROUP_CH)
        yg = jnp.dot(pooled_ref[:, sl], pw_ref[g], preferred_element_type=F32) * ps_ref[:, sl]
        pooled_ref[:, sl] = yg.astype(BF16)
    for rows in halves:
        m = m_ref[rows, :] + jnp.dot(pooled_ref[rows, :], wo_ref[0:pool_w, :],
                                     preferred_element_type=F32)
        o_ref[rows, :] = x_ref[rows, :] + _rms(m, g_ref[...])


def _mix_out(x, u_pool, attn, four, wo, pw, ps, g, tm):
    B, L, D = x.shape
    widths = (u_pool.shape[-1], attn.shape[-1], four.shape[-1])
    hb = tm // POOL_HALO
    n_hb = L // POOL_HALO
    row = lambda b, i: (b, i, 0)
    const2 = lambda b, i: (0, 0)
    kern = functools.partial(_mix_out_kernel, tm=tm, seq_len=L, widths=widths)
    return pl.pallas_call(
        kern,
        grid=(B, L // tm),
        in_specs=[
            pl.BlockSpec((None, POOL_HALO, widths[0]), lambda b, i: (b, jnp.maximum(i * hb - 1, 0), 0)),
            pl.BlockSpec((None, tm, widths[0]), row),
            pl.BlockSpec((None, POOL_HALO, widths[0]),
                         lambda b, i: (b, jnp.minimum((i + 1) * hb, n_hb - 1), 0)),
            pl.BlockSpec((None, tm, widths[1]), row),
            pl.BlockSpec((None, tm, widths[2]), row),
            pl.BlockSpec((None, tm, D), row),
            pl.BlockSpec(wo.shape, const2),
            pl.BlockSpec(pw.shape, lambda b, i: (0, 0, 0)),
            pl.BlockSpec((1, widths[0]), const2),
            pl.BlockSpec((1, D), const2),
        ],
        out_specs=pl.BlockSpec((None, tm, D), row),
        out_shape=jax.ShapeDtypeStruct((B, L, D), F32),
        scratch_shapes=[
            pltpu.VMEM((tm + 2 * POOL_HALO, widths[0]), F32),
            pltpu.VMEM((tm, D), F32),
            pltpu.VMEM((tm, widths[0]), BF16),
        ],
        compiler_params=_params("parallel", "parallel"),
        name="mix_out",
    )(u_pool, u_pool, u_pool, attn, four, x, wo, pw, ps, g)


def _gelu_tanh(x):
    return 0.5 * x * (1.0 + jnp.tanh(math.sqrt(2.0 / math.pi) * (x + 0.044715 * (x * x * x))))


def _ffn_kernel(xp_ref, x_ref, xn_ref, gpre_ref, wg_ref, wv_ref, cwg_ref, cwv_ref, cbg_ref, cbv_ref,
                wd_ref, gpost_ref, o_ref, h_ref, acc_ref, ug_ref, uv_ref, *, tm, sub):
    i = pl.program_id(1)
    j = pl.program_id(2)
    rows = tm + 2 * CONV_HALO

    @pl.when(j == 0)
    def _():
        g = gpre_ref[...]
        hp = jnp.where(i > 0, _rms(xp_ref[...], g), 0.0)
        hn = jnp.where(i < pl.num_programs(1) - 1, _rms(xn_ref[...], g), 0.0)
        h_ref[0:CONV_HALO, :] = hp.astype(BF16)
        h_ref[CONV_HALO:CONV_HALO + tm, :] = _rms(x_ref[...], g).astype(BF16)
        h_ref[CONV_HALO + tm:, :] = hn.astype(BF16)
        acc_ref[...] = jnp.zeros(acc_ref.shape, F32)

    def conv(u_ref, cw_ref, cb_ref):
        prev = u_ref[CONV_HALO - 1:CONV_HALO - 1 + tm, :]
        cur = u_ref[CONV_HALO:CONV_HALO + tm, :]
        nxt = u_ref[CONV_HALO + 1:CONV_HALO + 1 + tm, :]
        return prev * cw_ref[0:1, :] + cur * cw_ref[1:2, :] + nxt * cw_ref[2:3, :] + cb_ref[...]

    h = h_ref[...]
    n_sub = wg_ref.shape[1] // sub
    for s in range(n_sub):
        sl = slice(s * sub, (s + 1) * sub)
        ug_ref[s] = jnp.dot(h, wg_ref[:, sl], preferred_element_type=F32)
        uv_ref[s] = jnp.dot(h, wv_ref[:, sl], preferred_element_type=F32)
    for s in range(n_sub):
        sl = slice(s * sub, (s + 1) * sub)
        gate = conv(ug_ref.at[s], cwg_ref.at[:, sl], cbg_ref.at[:, sl])
        val = conv(uv_ref.at[s], cwv_ref.at[:, sl], cbv_ref.at[:, sl])
        act = (_gelu_tanh(gate) * val).astype(BF16)
        acc_ref[...] += jnp.dot(act, wd_ref[sl, :], preferred_element_type=F32)

    @pl.when(j == pl.num_programs(2) - 1)
    def _():
        o_ref[...] = x_ref[...] + _rms(acc_ref[...], gpost_ref[...])


def _ffn(x, gpre, w_up, conv_w, conv_b, w_down, gpost, tm, chunk):
    B, L, D = x.shape
    d_ff = w_down.shape[0]
    nc = d_ff // chunk
    hb = tm // CONV_HALO
    n_hb = L // CONV_HALO
    const2 = lambda b, i, j: (0, 0)
    gate_col = lambda b, i, j: (0, j)
    val_col = lambda b, i, j: (0, nc + j)
    sub = min(FFN_SUB_CHUNK, chunk)
    kern = functools.partial(_ffn_kernel, tm=tm, sub=sub)
    return pl.pallas_call(
        kern,
        grid=(B, L // tm, nc),
        in_specs=[
            pl.BlockSpec((None, CONV_HALO, D), lambda b, i, j: (b, jnp.maximum(i * hb - 1, 0), 0)),
            pl.BlockSpec((None, tm, D), lambda b, i, j: (b, i, 0)),
            pl.BlockSpec((None, CONV_HALO, D),
                         lambda b, i, j: (b, jnp.minimum((i + 1) * hb, n_hb - 1), 0)),
            pl.BlockSpec((1, D), const2),
            pl.BlockSpec((D, chunk), gate_col),
            pl.BlockSpec((D, chunk), val_col),
            pl.BlockSpec((3, chunk), gate_col),
            pl.BlockSpec((3, chunk), val_col),
            pl.BlockSpec((1, chunk), gate_col),
            pl.BlockSpec((1, chunk), val_col),
            pl.BlockSpec((chunk, D), lambda b, i, j: (j, 0)),
            pl.BlockSpec((1, D), const2),
        ],
        out_specs=pl.BlockSpec((None, tm, D), lambda b, i, j: (b, i, 0)),
        out_shape=jax.ShapeDtypeStruct((B, L, D), F32),
        scratch_shapes=[
            pltpu.VMEM((tm + 2 * CONV_HALO, D), BF16),
            pltpu.VMEM((tm, D), F32),
            pltpu.VMEM((chunk // sub, tm + 2 * CONV_HALO, sub), F32),
            pltpu.VMEM((chunk // sub, tm + 2 * CONV_HALO, sub), F32),
        ],
        compiler_params=_params("parallel", "parallel", "arbitrary"),
        name="ffn",
    )(x, x, x, gpre, w_up, w_up, conv_w, conv_w, conv_b, conv_b, w_down, gpost)


def _rope_tables(seq_len):
    quarter = HEAD_DIM // 4
    t = jnp.arange(seq_len, dtype=jnp.int32)
    row = (t // GRID_W).astype(F32)
    col = (t % GRID_W).astype(F32)
    inv_freq = 1.0 / (ROPE_THETA ** (jnp.arange(quarter, dtype=F32) / quarter))
    ang = jnp.concatenate([row[:, None] * inv_freq[None, :], col[:, None] * inv_freq[None, :]], axis=-1)
    cos, sin = jnp.cos(ang), jnp.sin(ang)
    return jnp.concatenate([cos, cos], axis=-1), jnp.concatenate([-sin, sin], axis=-1)


def _fourier_split(seq_len):
    l2 = 1 << (int(math.log2(seq_len)) // 2)
    return seq_len // l2, l2


def _trunk(x, layers):
    B, L, D = x.shape
    ts = _tiles(L)
    cc, ss = _rope_tables(L)
    l1, l2 = _fourier_split(L)
    for p in layers:
        u_pool, qt, k, vt, u_four = _in_proj(x, p["g_pre_mix"], p["w_in"], cc, ss, p["q_norm"],
                                             p["k_norm"], p["widths"], ts["tm_proj"])
        attn = _attention(qt, k, vt, ts["tq"])
        four = _fourier(u_four, p["fourier_w"], l1, l2)
        x = _mix_out(x, u_pool, attn, four, p["w_out"], p["pool_w"], p["pool_scale"],
                     p["g_post_mix"], ts["tm_mix"])
        x = _ffn(x, p["g_pre_ffn"], p["w_up"], p["conv_w"], p["conv_b"], p["w_down"],
                 p["g_post_ffn"], ts["tm_ffn"], ts["ff_chunk"])
    return x


def kernel(x_prompt, x_sample, g_pre_mix, g_post_mix, w_in, pool_w, pool_scale, q_norm, k_norm,
           fourier_w, w_out, g_pre_ffn, g_post_ffn, w_up, conv_w, conv_b, w_down):
    depth = w_in.shape[0]
    pool_width = pool_scale.shape[-1]
    four_width = fourier_w.shape[1] * fourier_w.shape[2]
    kv_width = N_KV_HEADS * HEAD_DIM
    q_width = w_in.shape[-1] - pool_width - four_width - 2 * kv_width
    layers = []
    for l in range(depth):
        layers.append(dict(
            widths=(pool_width, q_width, kv_width, four_width),
            g_pre_mix=g_pre_mix[l][None, :], g_post_mix=g_post_mix[l][None, :],
            w_in=w_in[l].astype(BF16), pool_w=pool_w[l].astype(BF16),
            pool_scale=pool_scale[l][None, :], q_norm=q_norm[l][None, :], k_norm=k_norm[l][None, :],
            fourier_w=fourier_w[l].astype(BF16), w_out=w_out[l].astype(BF16),
            g_pre_ffn=g_pre_ffn[l][None, :], g_post_ffn=g_post_ffn[l][None, :],
            w_up=w_up[l].astype(BF16), conv_w=conv_w[l], conv_b=conv_b[l][None, :],
            w_down=w_down[l].astype(BF16)))
    return _trunk(x_prompt, layers), _trunk(x_sample, layers)
```

```python
import functools
import math

import jax
import jax.numpy as jnp
import numpy as np
from jax import lax
from jax.experimental import pallas as pl
from jax.experimental.pallas import tpu as pltpu

F32 = jnp.float32
BF16 = jnp.bfloat16

NORM_EPS = 1e-6
GRID_W = 64
HEAD_DIM = 128
N_KV_HEADS = 2
GQA_GROUP = 4
POOL_WINDOWS = (2, 4, 8, 16)
POOL_HALO = 8
GROUP_CH = 128
N_GROUPS = 4
ROPE_THETA = 10000.0
CONV_HALO = 16
ATTN_CHUNKS_PER_TRIP = 4
FOURIER_S2_PER_STEP = 8
FFN_SUB_CHUNK = 256

V7X_VMEM_BYTES = 64 * 1024 * 1024
VMEM_LIMIT = V7X_VMEM_BYTES - 4 * 1024 * 1024


def _tiles(seq_len):
    return dict(
        tm_proj=min(512, seq_len),
        tm_mix=min(512, seq_len),
        tm_ffn=min(1024, seq_len),
        ff_chunk=512,
        tq=min(512, seq_len),
    )


def _params(*sem):
    return pltpu.CompilerParams(dimension_semantics=sem, vmem_limit_bytes=VMEM_LIMIT)


def _rms(x, g):
    return x * lax.rsqrt(jnp.mean(x * x, axis=-1, keepdims=True) + NORM_EPS) * g


def _in_proj_kernel(x_ref, g_ref, w_ref, cc_ref, ss_ref, qn_ref, kn_ref,
                    pool_ref, q_ref, k_ref, v_ref, four_ref, *, widths, scale):
    pool_w, q_w, kv_w, four_w = widths
    h = _rms(x_ref[...], g_ref[...]).astype(BF16)
    cc = cc_ref[...]
    ss = ss_ref[...]

    def seg(lo, width):
        return jnp.dot(h, w_ref[:, lo:lo + width], preferred_element_type=F32)

    def norm_rope(zh, gain, out_scale):
        y = zh * lax.rsqrt(jnp.mean(zh * zh, axis=-1, keepdims=True) + NORM_EPS) * gain
        y = y * cc + pltpu.roll(y, HEAD_DIM // 2, axis=1) * ss
        return y * out_scale

    pair = 2 * HEAD_DIM
    for p in range(q_w // pair):
        z = seg(pool_w + p * pair, pair)
        for s in range(2):
            zh = z[:, s * HEAD_DIM:(s + 1) * HEAD_DIM]
            q_ref[2 * p + s] = norm_rope(zh, qn_ref[...], scale).T.astype(BF16)
    z = seg(pool_w + q_w, kv_w)
    for s in range(kv_w // HEAD_DIM):
        zh = z[:, s * HEAD_DIM:(s + 1) * HEAD_DIM]
        k_ref[:, s * HEAD_DIM:(s + 1) * HEAD_DIM] = norm_rope(zh, kn_ref[...], 1.0).astype(BF16)
    z = seg(pool_w + q_w + kv_w, kv_w)
    for s in range(kv_w // HEAD_DIM):
        v_ref[s] = z[:, s * HEAD_DIM:(s + 1) * HEAD_DIM].T.astype(BF16)
    four_ref[...] = seg(pool_w + q_w + 2 * kv_w, four_w)
    pool_ref[...] = seg(0, pool_w)


def _in_proj(x, g, w, cc, ss, qn, kn, widths, tm):
    B, L, D = x.shape
    pool_w, q_w, kv_w, four_w = widths
    nt = L // tm
    row = lambda b, i: (b, i, 0)
    const2 = lambda b, i: (0, 0)
    n_q = q_w // HEAD_DIM
    v_rows = HEAD_DIM
    kern = functools.partial(_in_proj_kernel, widths=widths, scale=HEAD_DIM ** -0.5 * math.log2(math.e))
    return pl.pallas_call(
        kern,
        grid=(B, nt),
        in_specs=[
            pl.BlockSpec((None, tm, D), row),
            pl.BlockSpec((1, D), const2),
            pl.BlockSpec(w.shape, const2),
            pl.BlockSpec((tm, HEAD_DIM), lambda b, i: (i, 0)),
            pl.BlockSpec((tm, HEAD_DIM), lambda b, i: (i, 0)),
            pl.BlockSpec((1, HEAD_DIM), const2),
            pl.BlockSpec((1, HEAD_DIM), const2),
        ],
        out_specs=[
            pl.BlockSpec((None, tm, pool_w), row),
            pl.BlockSpec((None, n_q, HEAD_DIM, tm), lambda b, i: (b, 0, 0, i)),
            pl.BlockSpec((None, tm, kv_w), row),
            pl.BlockSpec((None, N_KV_HEADS, None, v_rows, tm), lambda b, i: (b, 0, i, 0, 0)),
            pl.BlockSpec((None, tm, four_w), row),
        ],
        out_shape=[
            jax.ShapeDtypeStruct((B, L, pool_w), F32),
            jax.ShapeDtypeStruct((B, n_q, HEAD_DIM, L), BF16),
            jax.ShapeDtypeStruct((B, L, kv_w), BF16),
            jax.ShapeDtypeStruct((B, N_KV_HEADS, nt, v_rows, tm), BF16),
            jax.ShapeDtypeStruct((B, L, four_w), F32),
        ],
        compiler_params=_params("parallel", "parallel"),
        name="in_proj",
    )(x, g, w, cc, ss, qn, kn)


def _attn_kernel(qt_ref, k_ref, vt_ref, o_ref, m_ref, l_ref, acc_ref, sa_ref, sb_ref, ma_ref, mb_ref,
                 *, tk, nk, unroll):
    assert nk == 1 or (unroll % 2 == 0 and nk % unroll == 0)
    m_ref[...] = jnp.full(m_ref.shape, -jnp.inf, F32)
    l_ref[...] = jnp.zeros(l_ref.shape, F32)
    acc_ref[...] = jnp.zeros(acc_ref.shape, F32)

    def scores(c, g, s_ref, cm_ref):
        kc = k_ref[pl.ds(pl.multiple_of(c * tk, tk), tk), :]
        st = jnp.dot(kc, qt_ref[g], preferred_element_type=F32)
        s_ref[g] = st
        cm_ref[g] = jnp.max(st, axis=0, keepdims=True)

    def accumulate(c, g, s_ref, cm_ref):
        m_old = m_ref[g]
        m_new = jnp.maximum(m_old, cm_ref[g])
        alpha = jnp.exp2(m_old - m_new)
        p = jnp.exp2(s_ref[g] - m_new)
        l_ref[g] = alpha * l_ref[g] + jnp.sum(p, axis=0, keepdims=True)
        acc_ref[g] = alpha * acc_ref[g] + jnp.dot(vt_ref[c], p.astype(BF16),
                                                  preferred_element_type=F32)
        m_ref[g] = m_new

    def step(c_scores, s_next, m_next, c_acc, s_cur, m_cur):
        for g in range(GQA_GROUP):
            if c_scores is not None:
                scores(c_scores, g, s_next, m_next)
            if c_acc is not None:
                accumulate(c_acc, g, s_cur, m_cur)

    bufs = ((sa_ref, ma_ref), (sb_ref, mb_ref))
    step(0, *bufs[0], None, None, None)
    if nk > 1:
        def group(i, carry):
            for u in range(unroll):
                c = unroll * i + u
                step(c + 1, *bufs[(u + 1) % 2], c, *bufs[u % 2])
            return carry

        lax.fori_loop(0, nk // unroll - 1, group, 0)
        for c in range(nk - unroll, nk - 1):
            step(c + 1, *bufs[(c + 1) % 2], c, *bufs[c % 2])
    step(None, None, None, nk - 1, *bufs[(nk - 1) % 2])
    for g in range(GQA_GROUP):
        o = acc_ref[g] / l_ref[g]
        o_ref[:, g * HEAD_DIM:(g + 1) * HEAD_DIM] = o.T.astype(o_ref.dtype)


def _attention(qt, k, vt, tq):
    B, n_q, _, L = qt.shape
    _, _, nk, v_rows, tk = vt.shape
    gw = GQA_GROUP * HEAD_DIM
    unroll = ATTN_CHUNKS_PER_TRIP if nk % ATTN_CHUNKS_PER_TRIP == 0 and nk >= 4 * ATTN_CHUNKS_PER_TRIP else 2
    kern = functools.partial(_attn_kernel, tk=tk, nk=nk, unroll=unroll)
    return pl.pallas_call(
        kern,
        grid=(B, N_KV_HEADS, L // tq),
        in_specs=[
            pl.BlockSpec((None, GQA_GROUP, HEAD_DIM, tq), lambda b, j, i: (b, j, 0, i)),
            pl.BlockSpec((None, L, HEAD_DIM), lambda b, j, i: (b, 0, j)),
            pl.BlockSpec((None, None, nk, v_rows, tk), lambda b, j, i: (b, j, 0, 0, 0)),
        ],
        out_specs=pl.BlockSpec((None, tq, gw), lambda b, j, i: (b, i, j)),
        out_shape=jax.ShapeDtypeStruct((B, L, n_q * HEAD_DIM), BF16),
        scratch_shapes=[
            pltpu.VMEM((GQA_GROUP, 1, tq), F32),
            pltpu.VMEM((GQA_GROUP, 1, tq), F32),
            pltpu.VMEM((GQA_GROUP, v_rows, tq), F32),
            pltpu.VMEM((GQA_GROUP, tk, tq), F32),
            pltpu.VMEM((GQA_GROUP, tk, tq), F32),
            pltpu.VMEM((GQA_GROUP, 1, tq), F32),
            pltpu.VMEM((GQA_GROUP, 1, tq), F32),
        ],
        compiler_params=_params("parallel", "parallel", "arbitrary"),
        name="attention",
    )(qt, k, vt)


def _dft_cos_sin(n):
    idx = np.arange(n)
    ang = 2.0 * np.pi * ((idx[:, None] * idx[None, :]) % n) / n
    return np.cos(ang), np.sin(ang)


def _fourier_a_kernel(x_ref, f1_ref, twc_ref, tws_ref, zr_ref, zi_ref, *, l2, k):
    for t in range(k):
        z = jnp.dot(f1_ref[...], x_ref[:, t, :].astype(BF16), preferred_element_type=F32)
        a = z[:l2]
        b = z[l2:]
        c = twc_ref[:, t:t + 1]
        s = tws_ref[:, t:t + 1]
        zr_ref[:, t, :] = a * c + b * s
        zi_ref[:, t, :] = b * c - a * s


def _fourier_b_kernel(zr_ref, zi_ref, f2_ref, fc_ref, fw_ref, o_ref, hr_ref, hi_ref, *, l1, n_s2, norm):
    for a in range(n_s2):
        rows = slice(a * l1, (a + 1) * l1)
        zz = jnp.concatenate([zr_ref[rows, :].astype(BF16), zi_ref[rows, :].astype(BF16)], axis=0)
        hh = jnp.dot(f2_ref[...], zz, preferred_element_type=F32)
        hr_ref[rows, :] = hh[:l1].astype(BF16)
        hi_ref[rows, :] = hh[l1:].astype(BF16)
    for g in range(N_GROUPS):
        sl = slice(g * GROUP_CH, (g + 1) * GROUP_CH)
        hg = jnp.concatenate([hr_ref[:, sl], hi_ref[:, sl]], axis=1)
        f = jnp.dot(hg, fc_ref[...], preferred_element_type=F32) * norm
        y = jnp.dot(f.astype(BF16), fw_ref[g], preferred_element_type=F32)
        for a in range(n_s2):
            o_ref[:, a, sl] = y[a * l1:(a + 1) * l1]


def _fourier(u, fw, l1, l2):
    B, L, width = u.shape
    k = min(FOURIER_S2_PER_STEP, l1)
    n_s2 = min(FOURIER_S2_PER_STEP, l2)
    c2, s2 = _dft_cos_sin(l2)
    f1 = jnp.asarray(np.concatenate([c2, -s2], axis=0), BF16)
    c1, s1 = _dft_cos_sin(l1)
    f2 = jnp.asarray(np.block([[c1, s1], [-s1, c1]]), BF16)
    cc, sc = _dft_cos_sin(GROUP_CH)
    fc = jnp.asarray(np.concatenate([cc, sc], axis=0), BF16)
    ang = 2.0 * np.pi * ((np.arange(l2)[:, None] * np.arange(l1)[None, :]) % L) / L
    tw = lambda t: jnp.asarray(t.reshape(l2, l1 // k, k).transpose(1, 0, 2), F32)
    twc, tws = tw(np.cos(ang)), tw(np.sin(ang))

    x = u.reshape(B, l2, l1, width)
    col = lambda b, j: (b, 0, j, 0)
    const2 = lambda b, j: (0, 0)
    zr, zi = pl.pallas_call(
        functools.partial(_fourier_a_kernel, l2=l2, k=k),
        grid=(B, l1 // k),
        in_specs=[
            pl.BlockSpec((None, l2, k, width), col),
            pl.BlockSpec(f1.shape, const2),
            pl.BlockSpec((None, l2, k), lambda b, j: (j, 0, 0)),
            pl.BlockSpec((None, l2, k), lambda b, j: (j, 0, 0)),
        ],
        out_specs=[pl.BlockSpec((None, l2, k, width), col)] * 2,
        out_shape=[jax.ShapeDtypeStruct((B, l2, l1, width), F32)] * 2,
        compiler_params=_params("parallel", "parallel"),
        name="fourier_a",
    )(x, f1, twc, tws)

    zr = zr.reshape(B, L, width)
    zi = zi.reshape(B, L, width)
    rows = lambda b, j: (b, j, 0)
    out = pl.pallas_call(
        functools.partial(_fourier_b_kernel, l1=l1, n_s2=n_s2, norm=1.0 / math.sqrt(L * GROUP_CH)),
        grid=(B, l2 // n_s2),
        in_specs=[
            pl.BlockSpec((None, n_s2 * l1, width), rows),
            pl.BlockSpec((None, n_s2 * l1, width), rows),
            pl.BlockSpec(f2.shape, const2),
            pl.BlockSpec(fc.shape, const2),
            pl.BlockSpec(fw.shape, lambda b, j: (0, 0, 0)),
        ],
        out_specs=pl.BlockSpec((None, l1, n_s2, width), lambda b, j: (b, 0, j, 0)),
        out_shape=jax.ShapeDtypeStruct((B, l1, l2, width), F32),
        scratch_shapes=[pltpu.VMEM((n_s2 * l1, width), BF16)] * 2,
        compiler_params=_params("parallel", "parallel"),
        name="fourier_b",
    )(zr, zi, f2, fc, fw)
    return out.reshape(B, L, width)


def _mix_out_kernel(up_ref, uc_ref, un_ref, attn_ref, four_ref, x_ref, wo_ref, pw_ref, ps_ref,
                    g_ref, o_ref, ext_ref, m_ref, pooled_ref, *, tm, seq_len, widths):
    pool_w, attn_w, four_w = widths
    i = pl.program_id(1)
    last = pl.num_programs(1) - 1
    ext_ref[0:POOL_HALO, :] = jnp.where(i > 0, up_ref[...], 0.0)
    ext_ref[POOL_HALO:POOL_HALO + tm, :] = uc_ref[...]
    ext_ref[POOL_HALO + tm:, :] = jnp.where(i < last, un_ref[...], 0.0)

    halves = [slice(r * (tm // 2), (r + 1) * (tm // 2)) for r in range(2)]
    t = i * tm + lax.broadcasted_iota(jnp.int32, (tm, 1), 0)

    def project(rows):
        m_ref[rows, :] = (
            jnp.dot(attn_ref[rows, :], wo_ref[pool_w:pool_w + attn_w, :], preferred_element_type=F32)
            + jnp.dot(four_ref[rows, :].astype(BF16), wo_ref[pool_w + attn_w:, :],
                      preferred_element_type=F32))

    def pool(g):
        w = POOL_WINDOWS[g]
        sl = slice(g * GROUP_CH, (g + 1) * GROUP_CH)
        lo = jnp.maximum(t - w // 2, 0)
        hi = jnp.minimum(t + (w - 1 - w // 2), seq_len - 1)
        cnt = (hi - lo + 1).astype(F32)
        win = ext_ref[POOL_HALO - w // 2:POOL_HALO - w // 2 + tm, sl]
        for d in range(1 - w // 2, w - w // 2):
            win = win + ext_ref[POOL_HALO + d:POOL_HALO + d + tm, sl]
        dg = win / cnt - uc_ref[:, sl]
        pooled_ref[:, sl] = dg.astype(BF16)

    for g in range(len(POOL_WINDOWS)):
        pool(g)
    for rows in halves:
        project(rows)
    for g in range(len(POOL_WINDOWS)):
        sl = slice(g * GROUP_CH, (g + 1) * GROUP_CH)
        yg = jnp.dot(pooled_ref[:, sl], pw_ref[g], preferred_element_type=F32) * ps_ref[:, sl]
        pooled_ref[:, sl] = yg.astype(BF16)
    for rows in halves:
        m = m_ref[rows, :] + jnp.dot(pooled_ref[rows, :], wo_ref[0:pool_w, :],
                                     preferred_element_type=F32)
        o_ref[rows, :] = x_ref[rows, :] + _rms(m, g_ref[...])


def _mix_out(x, u_pool, attn, four, wo, pw, ps, g, tm):
    B, L, D = x.shape
    widths = (u_pool.shape[-1], attn.shape[-1], four.shape[-1])
    hb = tm // POOL_HALO
    n_hb = L // POOL_HALO
    row = lambda b, i: (b, i, 0)
    const2 = lambda b, i: (0, 0)
    kern = functools.partial(_mix_out_kernel, tm=tm, seq_len=L, widths=widths)
    return pl.pallas_call(
        kern,
        grid=(B, L // tm),
        in_specs=[
            pl.BlockSpec((None, POOL_HALO, widths[0]), lambda b, i: (b, jnp.maximum(i * hb - 1, 0), 0)),
            pl.BlockSpec((None, tm, widths[0]), row),
            pl.BlockSpec((None, POOL_HALO, widths[0]),
                         lambda b, i: (b, jnp.minimum((i + 1) * hb, n_hb - 1), 0)),
            pl.BlockSpec((None, tm, widths[1]), row),
            pl.BlockSpec((None, tm, widths[2]), row),
            pl.BlockSpec((None, tm, D), row),
            pl.BlockSpec(wo.shape, const2),
            pl.BlockSpec(pw.shape, lambda b, i: (0, 0, 0)),
            pl.BlockSpec((1, widths[0]), const2),
            pl.BlockSpec((1, D), const2),
        ],
        out_specs=pl.BlockSpec((None, tm, D), row),
        out_shape=jax.ShapeDtypeStruct((B, L, D), F32),
        scratch_shapes=[
            pltpu.VMEM((tm + 2 * POOL_HALO, widths[0]), F32),
            pltpu.VMEM((tm, D), F32),
            pltpu.VMEM((tm, widths[0]), BF16),
        ],
        compiler_params=_params("parallel", "parallel"),
        name="mix_out",
    )(u_pool, u_pool, u_pool, attn, four, x, wo, pw, ps, g)


def _gelu_tanh(x):
    return 0.5 * x * (1.0 + jnp.tanh(math.sqrt(2.0 / math.pi) * (x + 0.044715 * (x * x * x))))


def _ffn_kernel(xp_ref, x_ref, xn_ref, gpre_ref, wg_ref, wv_ref, cwg_ref, cwv_ref, cbg_ref, cbv_ref,
                wd_ref, gpost_ref, o_ref, h_ref, ug_ref, uv_ref, *, tm, sub):
    i = pl.program_id(1)
    j = pl.program_id(2)
    rows = tm + 2 * CONV_HALO

    @pl.when(j == 0)
    def _():
        g = gpre_ref[...]
        hp = jnp.where(i > 0, _rms(xp_ref[...], g), 0.0)
        hn = jnp.where(i < pl.num_programs(1) - 1, _rms(xn_ref[...], g), 0.0)
        h_ref[0:CONV_HALO, :] = hp.astype(BF16)
        h_ref[CONV_HALO:CONV_HALO + tm, :] = _rms(x_ref[...], g).astype(BF16)
        h_ref[CONV_HALO + tm:, :] = hn.astype(BF16)
        o_ref[...] = jnp.zeros(o_ref.shape, F32)

    def conv(u_ref, cw_ref, cb_ref):
        prev = u_ref[CONV_HALO - 1:CONV_HALO - 1 + tm, :]
        cur = u_ref[CONV_HALO:CONV_HALO + tm, :]
        nxt = u_ref[CONV_HALO + 1:CONV_HALO + 1 + tm, :]
        return prev * cw_ref[0:1, :] + cur * cw_ref[1:2, :] + nxt * cw_ref[2:3, :] + cb_ref[...]

    h = h_ref[...]
    n_sub = wg_ref.shape[1] // sub
    for s in range(n_sub):
        sl = slice(s * sub, (s + 1) * sub)
        ug_ref[s] = jnp.dot(h, wg_ref[:, sl], preferred_element_type=F32)
        uv_ref[s] = jnp.dot(h, wv_ref[:, sl], preferred_element_type=F32)
    for s in range(n_sub):
        sl = slice(s * sub, (s + 1) * sub)
        gate = conv(ug_ref.at[s], cwg_ref.at[:, sl], cbg_ref.at[:, sl])
        val = conv(uv_ref.at[s], cwv_ref.at[:, sl], cbv_ref.at[:, sl])
        act = (_gelu_tanh(gate) * val).astype(BF16)
        o_ref[...] += jnp.dot(act, wd_ref[sl, :], preferred_element_type=F32)

    @pl.when(j == pl.num_programs(2) - 1)
    def _():
        o_ref[...] = x_ref[...] + _rms(o_ref[...], gpost_ref[...])


def _ffn(x, gpre, w_up, conv_w, conv_b, w_down, gpost, tm, chunk):
    B, L, D = x.shape
    d_ff = w_down.shape[0]
    nc = d_ff // chunk
    hb = tm // CONV_HALO
    n_hb = L // CONV_HALO
    const2 = lambda b, i, j: (0, 0)
    gate_col = lambda b, i, j: (0, j)
    val_col = lambda b, i, j: (0, nc + j)
    sub = min(FFN_SUB_CHUNK, chunk)
    kern = functools.partial(_ffn_kernel, tm=tm, sub=sub)
    return pl.pallas_call(
        kern,
        grid=(B, L // tm, nc),
        in_specs=[
            pl.BlockSpec((None, CONV_HALO, D), lambda b, i, j: (b, jnp.maximum(i * hb - 1, 0), 0)),
            pl.BlockSpec((None, tm, D), lambda b, i, j: (b, i, 0), pipeline_mode=pl.Buffered(1)),
            pl.BlockSpec((None, CONV_HALO, D),
                         lambda b, i, j: (b, jnp.minimum((i + 1) * hb, n_hb - 1), 0)),
            pl.BlockSpec((1, D), const2),
            pl.BlockSpec((D, chunk), gate_col),
            pl.BlockSpec((D, chunk), val_col),
            pl.BlockSpec((3, chunk), gate_col),
            pl.BlockSpec((3, chunk), val_col),
            pl.BlockSpec((1, chunk), gate_col),
            pl.BlockSpec((1, chunk), val_col),
            pl.BlockSpec((chunk, D), lambda b, i, j: (j, 0)),
            pl.BlockSpec((1, D), const2),
        ],
        out_specs=pl.BlockSpec((None, tm, D), lambda b, i, j: (b, i, 0)),
        out_shape=jax.ShapeDtypeStruct((B, L, D), F32),
        scratch_shapes=[
            pltpu.VMEM((tm + 2 * CONV_HALO, D), BF16),
            pltpu.VMEM((chunk // sub, tm + 2 * CONV_HALO, sub), F32),
            pltpu.VMEM((chunk // sub, tm + 2 * CONV_HALO, sub), F32),
        ],
        compiler_params=_params("parallel", "parallel", "arbitrary"),
        name="ffn",
    )(x, x, x, gpre, w_up, w_up, conv_w, conv_w, conv_b, conv_b, w_down, gpost)


def _rope_tables(seq_len):
    quarter = HEAD_DIM // 4
    t = jnp.arange(seq_len, dtype=jnp.int32)
    row = (t // GRID_W).astype(F32)
    col = (t % GRID_W).astype(F32)
    inv_freq = 1.0 / (ROPE_THETA ** (jnp.arange(quarter, dtype=F32) / quarter))
    ang = jnp.concatenate([row[:, None] * inv_freq[None, :], col[:, None] * inv_freq[None, :]], axis=-1)
    cos, sin = jnp.cos(ang), jnp.sin(ang)
    return jnp.concatenate([cos, cos], axis=-1), jnp.concatenate([-sin, sin], axis=-1)


def _fourier_split(seq_len):
    l2 = 1 << (int(math.log2(seq_len)) // 2)
    return seq_len // l2, l2


def _trunk(x, layers):
    B, L, D = x.shape
    ts = _tiles(L)
    cc, ss = _rope_tables(L)
    l1, l2 = _fourier_split(L)
    for p in layers:
        u_pool, qt, k, vt, u_four = _in_proj(x, p["g_pre_mix"], p["w_in"], cc, ss, p["q_norm"],
                                             p["k_norm"], p["widths"], ts["tm_proj"])
        attn = _attention(qt, k, vt, ts["tq"])
        four = _fourier(u_four, p["fourier_w"], l1, l2)
        x = _mix_out(x, u_pool, attn, four, p["w_out"], p["pool_w"], p["pool_scale"],
                     p["g_post_mix"], ts["tm_mix"])
        x = _ffn(x, p["g_pre_ffn"], p["w_up"], p["conv_w"], p["conv_b"], p["w_down"],
                 p["g_post_ffn"], ts["tm_ffn"], ts["ff_chunk"])
    return x


def kernel(x_prompt, x_sample, g_pre_mix, g_post_mix, w_in, pool_w, pool_scale, q_norm, k_norm,
           fourier_w, w_out, g_pre_ffn, g_post_ffn, w_up, conv_w, conv_b, w_down):
    depth = w_in.shape[0]
    pool_width = pool_scale.shape[-1]
    four_width = fourier_w.shape[1] * fourier_w.shape[2]
    kv_width = N_KV_HEADS * HEAD_DIM
    q_width = w_in.shape[-1] - pool_width - four_width - 2 * kv_width
    layers = []
    for l in range(depth):
        layers.append(dict(
            widths=(pool_width, q_width, kv_width, four_width),
            g_pre_mix=g_pre_mix[l][None, :], g_post_mix=g_post_mix[l][None, :],
            w_in=w_in[l].astype(BF16), pool_w=pool_w[l].astype(BF16),
            pool_scale=pool_scale[l][None, :], q_norm=q_norm[l][None, :], k_norm=k_norm[l][None, :],
            fourier_w=fourier_w[l].astype(BF16), w_out=w_out[l].astype(BF16),
            g_pre_ffn=g_pre_ffn[l][None, :], g_post_ffn=g_post_ffn[l][None, :],
            w_up=w_up[l].astype(BF16), conv_w=conv_w[l], conv_b=conv_b[l][None, :],
            w_down=w_down[l].astype(BF16)))
    return _trunk(x_prompt, layers), _trunk(x_sample, layers)
```

```python
import functools
import math

import jax
import jax.numpy as jnp
import numpy as np
from jax import lax
from jax.experimental import pallas as pl
from jax.experimental.pallas import tpu as pltpu

F32 = jnp.float32
BF16 = jnp.bfloat16

NORM_EPS = 1e-6
GRID_W = 64
HEAD_DIM = 128
N_KV_HEADS = 2
GQA_GROUP = 4
POOL_WINDOWS = (2, 4, 8, 16)
POOL_HALO = 8
GROUP_CH = 128
N_GROUPS = 4
ROPE_THETA = 10000.0
CONV_HALO = 16
ATTN_CHUNKS_PER_TRIP = 4
FOURIER_S2_PER_STEP = 8
FFN_ROW_BLOCKS = 2
FFN_SUB_CHUNK = 256

V7X_VMEM_BYTES = 64 * 1024 * 1024
VMEM_LIMIT = V7X_VMEM_BYTES - 4 * 1024 * 1024


def _tiles(seq_len):
    return dict(
        tm_proj=min(512, seq_len),
        tm_mix=min(512, seq_len),
        tm_ffn=min(1024, seq_len),
        ff_chunk=512,
        tq=min(512, seq_len),
    )


def _params(*sem):
    return pltpu.CompilerParams(dimension_semantics=sem, vmem_limit_bytes=VMEM_LIMIT)


def _rms(x, g):
    return x * lax.rsqrt(jnp.mean(x * x, axis=-1, keepdims=True) + NORM_EPS) * g


def _in_proj_kernel(x_ref, g_ref, w_ref, cc_ref, ss_ref, qn_ref, kn_ref,
                    pool_ref, q_ref, k_ref, v_ref, four_ref, *, widths, scale):
    pool_w, q_w, kv_w, four_w = widths
    h = _rms(x_ref[...], g_ref[...]).astype(BF16)
    cc = cc_ref[...]
    ss = ss_ref[...]

    def seg(lo, width):
        return jnp.dot(h, w_ref[:, lo:lo + width], preferred_element_type=F32)

    def norm_rope(zh, gain, out_scale):
        y = zh * lax.rsqrt(jnp.mean(zh * zh, axis=-1, keepdims=True) + NORM_EPS) * gain
        y = y * cc + pltpu.roll(y, HEAD_DIM // 2, axis=1) * ss
        return y * out_scale

    pair = 2 * HEAD_DIM
    for p in range(q_w // pair):
        z = seg(pool_w + p * pair, pair)
        for s in range(2):
            zh = z[:, s * HEAD_DIM:(s + 1) * HEAD_DIM]
            q_ref[2 * p + s] = norm_rope(zh, qn_ref[...], scale).T.astype(BF16)
    z = seg(pool_w + q_w, kv_w)
    for s in range(kv_w // HEAD_DIM):
        zh = z[:, s * HEAD_DIM:(s + 1) * HEAD_DIM]
        k_ref[:, s * HEAD_DIM:(s + 1) * HEAD_DIM] = norm_rope(zh, kn_ref[...], 1.0).astype(BF16)
    z = seg(pool_w + q_w + kv_w, kv_w)
    for s in range(kv_w // HEAD_DIM):
        v_ref[s] = z[:, s * HEAD_DIM:(s + 1) * HEAD_DIM].T.astype(BF16)
    four_ref[...] = seg(pool_w + q_w + 2 * kv_w, four_w)
    pool_ref[...] = seg(0, pool_w)


def _in_proj(x, g, w, cc, ss, qn, kn, widths, tm):
    B, L, D = x.shape
    pool_w, q_w, kv_w, four_w = widths
    nt = L // tm
    row = lambda b, i: (b, i, 0)
    const2 = lambda b, i: (0, 0)
    n_q = q_w // HEAD_DIM
    v_rows = HEAD_DIM
    kern = functools.partial(_in_proj_kernel, widths=widths, scale=HEAD_DIM ** -0.5 * math.log2(math.e))
    return pl.pallas_call(
        kern,
        grid=(B, nt),
        in_specs=[
            pl.BlockSpec((None, tm, D), row),
            pl.BlockSpec((1, D), const2),
            pl.BlockSpec(w.shape, const2),
            pl.BlockSpec((tm, HEAD_DIM), lambda b, i: (i, 0)),
            pl.BlockSpec((tm, HEAD_DIM), lambda b, i: (i, 0)),
            pl.BlockSpec((1, HEAD_DIM), const2),
            pl.BlockSpec((1, HEAD_DIM), const2),
        ],
        out_specs=[
            pl.BlockSpec((None, tm, pool_w), row),
            pl.BlockSpec((None, n_q, HEAD_DIM, tm), lambda b, i: (b, 0, 0, i)),
            pl.BlockSpec((None, tm, kv_w), row),
            pl.BlockSpec((None, N_KV_HEADS, None, v_rows, tm), lambda b, i: (b, 0, i, 0, 0)),
            pl.BlockSpec((None, tm, four_w), row),
        ],
        out_shape=[
            jax.ShapeDtypeStruct((B, L, pool_w), F32),
            jax.ShapeDtypeStruct((B, n_q, HEAD_DIM, L), BF16),
            jax.ShapeDtypeStruct((B, L, kv_w), BF16),
            jax.ShapeDtypeStruct((B, N_KV_HEADS, nt, v_rows, tm), BF16),
            jax.ShapeDtypeStruct((B, L, four_w), F32),
        ],
        compiler_params=_params("parallel", "parallel"),
        name="in_proj",
    )(x, g, w, cc, ss, qn, kn)


def _attn_kernel(qt_ref, k_ref, vt_ref, o_ref, m_ref, l_ref, acc_ref, sa_ref, sb_ref, ma_ref, mb_ref,
                 *, tk, nk, unroll):
    assert nk == 1 or (unroll % 2 == 0 and nk % unroll == 0)
    m_ref[...] = jnp.full(m_ref.shape, -jnp.inf, F32)
    l_ref[...] = jnp.zeros(l_ref.shape, F32)
    acc_ref[...] = jnp.zeros(acc_ref.shape, F32)

    def scores(c, g, s_ref, cm_ref):
        kc = k_ref[pl.ds(pl.multiple_of(c * tk, tk), tk), :]
        st = jnp.dot(kc, qt_ref[g], preferred_element_type=F32)
        s_ref[g] = st
        cm_ref[g] = jnp.max(st, axis=0, keepdims=True)

    def accumulate(c, g, s_ref, cm_ref):
        m_old = m_ref[g]
        m_new = jnp.maximum(m_old, cm_ref[g])
        alpha = jnp.exp2(m_old - m_new)
        p = jnp.exp2(s_ref[g] - m_new)
        l_ref[g] = alpha * l_ref[g] + jnp.sum(p, axis=0, keepdims=True)
        acc_ref[g] = alpha * acc_ref[g] + jnp.dot(vt_ref[c], p.astype(BF16),
                                                  preferred_element_type=F32)
        m_ref[g] = m_new

    def step(c_scores, s_next, m_next, c_acc, s_cur, m_cur):
        for g in range(GQA_GROUP):
            if c_scores is not None:
                scores(c_scores, g, s_next, m_next)
            if c_acc is not None:
                accumulate(c_acc, g, s_cur, m_cur)

    bufs = ((sa_ref, ma_ref), (sb_ref, mb_ref))
    step(0, *bufs[0], None, None, None)
    if nk > 1:
        def group(i, carry):
            for u in range(unroll):
                c = unroll * i + u
                step(c + 1, *bufs[(u + 1) % 2], c, *bufs[u % 2])
            return carry

        lax.fori_loop(0, nk // unroll - 1, group, 0)
        for c in range(nk - unroll, nk - 1):
            step(c + 1, *bufs[(c + 1) % 2], c, *bufs[c % 2])
    step(None, None, None, nk - 1, *bufs[(nk - 1) % 2])
    for g in range(GQA_GROUP):
        o = acc_ref[g] / l_ref[g]
        o_ref[:, g * HEAD_DIM:(g + 1) * HEAD_DIM] = o.T.astype(o_ref.dtype)


def _attention(qt, k, vt, tq):
    B, n_q, _, L = qt.shape
    _, _, nk, v_rows, tk = vt.shape
    gw = GQA_GROUP * HEAD_DIM
    unroll = ATTN_CHUNKS_PER_TRIP if nk % ATTN_CHUNKS_PER_TRIP == 0 and nk >= 4 * ATTN_CHUNKS_PER_TRIP else 2
    kern = functools.partial(_attn_kernel, tk=tk, nk=nk, unroll=unroll)
    return pl.pallas_call(
        kern,
        grid=(B, N_KV_HEADS, L // tq),
        in_specs=[
            pl.BlockSpec((None, GQA_GROUP, HEAD_DIM, tq), lambda b, j, i: (b, j, 0, i)),
            pl.BlockSpec((None, L, HEAD_DIM), lambda b, j, i: (b, 0, j)),
            pl.BlockSpec((None, None, nk, v_rows, tk), lambda b, j, i: (b, j, 0, 0, 0)),
        ],
        out_specs=pl.BlockSpec((None, tq, gw), lambda b, j, i: (b, i, j)),
        out_shape=jax.ShapeDtypeStruct((B, L, n_q * HEAD_DIM), BF16),
        scratch_shapes=[
            pltpu.VMEM((GQA_GROUP, 1, tq), F32),
            pltpu.VMEM((GQA_GROUP, 1, tq), F32),
            pltpu.VMEM((GQA_GROUP, v_rows, tq), F32),
            pltpu.VMEM((GQA_GROUP, tk, tq), F32),
            pltpu.VMEM((GQA_GROUP, tk, tq), F32),
            pltpu.VMEM((GQA_GROUP, 1, tq), F32),
            pltpu.VMEM((GQA_GROUP, 1, tq), F32),
        ],
        compiler_params=_params("parallel", "parallel", "arbitrary"),
        name="attention",
    )(qt, k, vt)


def _dft_cos_sin(n):
    idx = np.arange(n)
    ang = 2.0 * np.pi * ((idx[:, None] * idx[None, :]) % n) / n
    return np.cos(ang), np.sin(ang)


def _fourier_a_kernel(x_ref, f1_ref, twc_ref, tws_ref, zr_ref, zi_ref, *, l2, k):
    xt = pltpu.einshape("mkw->kmw", x_ref[...])
    zr, zi = [], []
    for t in range(k):
        z = jnp.dot(f1_ref[...], xt[t].astype(BF16), preferred_element_type=F32)
        a = z[:l2]
        b = z[l2:]
        c = twc_ref[:, t:t + 1]
        s = tws_ref[:, t:t + 1]
        zr.append(a * c + b * s)
        zi.append(b * c - a * s)
    zr_ref[...] = pltpu.einshape("kmw->mkw", jnp.stack(zr))
    zi_ref[...] = pltpu.einshape("kmw->mkw", jnp.stack(zi))


def _fourier_b_kernel(zr_ref, zi_ref, f2_ref, fc_ref, fw_ref, o_ref, hr_ref, hi_ref, *, l1, n_s2, norm):
    for a in range(n_s2):
        rows = slice(a * l1, (a + 1) * l1)
        zz = jnp.concatenate([zr_ref[rows, :].astype(BF16), zi_ref[rows, :].astype(BF16)], axis=0)
        hh = jnp.dot(f2_ref[...], zz, preferred_element_type=F32)
        hr_ref[rows, :] = hh[:l1].astype(BF16)
        hi_ref[rows, :] = hh[l1:].astype(BF16)
    ys = []
    for g in range(N_GROUPS):
        sl = slice(g * GROUP_CH, (g + 1) * GROUP_CH)
        hg = jnp.concatenate([hr_ref[:, sl], hi_ref[:, sl]], axis=1)
        f = jnp.dot(hg, fc_ref[...], preferred_element_type=F32) * norm
        ys.append(jnp.dot(f.astype(BF16), fw_ref[g], preferred_element_type=F32))
    y = jnp.concatenate(ys, axis=1).reshape(n_s2, l1, o_ref.shape[-1])
    o_ref[...] = pltpu.einshape("alw->law", y)


def _fourier(u, fw, l1, l2):
    B, L, width = u.shape
    k = min(FOURIER_S2_PER_STEP, l1)
    n_s2 = min(FOURIER_S2_PER_STEP, l2)
    c2, s2 = _dft_cos_sin(l2)
    f1 = jnp.asarray(np.concatenate([c2, -s2], axis=0), BF16)
    c1, s1 = _dft_cos_sin(l1)
    f2 = jnp.asarray(np.block([[c1, s1], [-s1, c1]]), BF16)
    cc, sc = _dft_cos_sin(GROUP_CH)
    fc = jnp.asarray(np.concatenate([cc, sc], axis=0), BF16)
    ang = 2.0 * np.pi * ((np.arange(l2)[:, None] * np.arange(l1)[None, :]) % L) / L
    tw = lambda t: jnp.asarray(t.reshape(l2, l1 // k, k).transpose(1, 0, 2), F32)
    twc, tws = tw(np.cos(ang)), tw(np.sin(ang))

    x = u.reshape(B, l2, l1, width)
    col = lambda b, j: (b, 0, j, 0)
    const2 = lambda b, j: (0, 0)
    zr, zi = pl.pallas_call(
        functools.partial(_fourier_a_kernel, l2=l2, k=k),
        grid=(B, l1 // k),
        in_specs=[
            pl.BlockSpec((None, l2, k, width), col),
            pl.BlockSpec(f1.shape, const2),
            pl.BlockSpec((None, l2, k), lambda b, j: (j, 0, 0)),
            pl.BlockSpec((None, l2, k), lambda b, j: (j, 0, 0)),
        ],
        out_specs=[pl.BlockSpec((None, l2, k, width), col)] * 2,
        out_shape=[jax.ShapeDtypeStruct((B, l2, l1, width), F32)] * 2,
        compiler_params=_params("parallel", "parallel"),
        name="fourier_a",
    )(x, f1, twc, tws)

    zr = zr.reshape(B, L, width)
    zi = zi.reshape(B, L, width)
    rows = lambda b, j: (b, j, 0)
    out = pl.pallas_call(
        functools.partial(_fourier_b_kernel, l1=l1, n_s2=n_s2, norm=1.0 / math.sqrt(L * GROUP_CH)),
        grid=(B, l2 // n_s2),
        in_specs=[
            pl.BlockSpec((None, n_s2 * l1, width), rows),
            pl.BlockSpec((None, n_s2 * l1, width), rows),
            pl.BlockSpec(f2.shape, const2),
            pl.BlockSpec(fc.shape, const2),
            pl.BlockSpec(fw.shape, lambda b, j: (0, 0, 0)),
        ],
        out_specs=pl.BlockSpec((None, l1, n_s2, width), lambda b, j: (b, 0, j, 0)),
        out_shape=jax.ShapeDtypeStruct((B, l1, l2, width), F32),
        scratch_shapes=[pltpu.VMEM((n_s2 * l1, width), BF16)] * 2,
        compiler_params=_params("parallel", "parallel"),
        name="fourier_b",
    )(zr, zi, f2, fc, fw)
    return out.reshape(B, L, width)


def _mix_out_kernel(up_ref, uc_ref, un_ref, attn_ref, four_ref, x_ref, wo_ref, pw_ref, ps_ref,
                    g_ref, o_ref, ext_ref, v_ref, m_ref, pooled_ref, *, tm, seq_len, widths):
    pool_w, attn_w, four_w = widths
    i = pl.program_id(1)
    last = pl.num_programs(1) - 1
    ext_ref[0:POOL_HALO, :] = jnp.where(i > 0, up_ref[...], 0.0)
    ext_ref[POOL_HALO:POOL_HALO + tm, :] = uc_ref[...]
    ext_ref[POOL_HALO + tm:, :] = jnp.where(i < last, un_ref[...], 0.0)

    halves = [slice(r * (tm // 2), (r + 1) * (tm // 2)) for r in range(2)]
    t = i * tm + lax.broadcasted_iota(jnp.int32, (tm, 1), 0)
    for g in range(len(POOL_WINDOWS)):
        sl = slice(g * GROUP_CH, (g + 1) * GROUP_CH)
        ug = ext_ref[:, sl]
        hi = ug.astype(BF16)
        lo = (ug - hi.astype(F32)).astype(BF16)
        pw2 = jnp.concatenate([pw_ref[g], pw_ref[g]], axis=0)
        v_ref[:, sl] = jnp.dot(jnp.concatenate([hi, lo], axis=1), pw2, preferred_element_type=F32)

    for rows in halves:
        m_ref[rows, :] = (
            jnp.dot(attn_ref[rows, :], wo_ref[pool_w:pool_w + attn_w, :], preferred_element_type=F32)
            + jnp.dot(four_ref[rows, :].astype(BF16), wo_ref[pool_w + attn_w:, :],
                      preferred_element_type=F32))

    for g, w in enumerate(POOL_WINDOWS):
        sl = slice(g * GROUP_CH, (g + 1) * GROUP_CH)
        lo = jnp.maximum(t - w // 2, 0)
        hi = jnp.minimum(t + (w - 1 - w // 2), seq_len - 1)
        cnt = (hi - lo + 1).astype(F32)
        win = v_ref[POOL_HALO - w // 2:POOL_HALO - w // 2 + tm, sl]
        for d in range(1 - w // 2, w - w // 2):
            win = win + v_ref[POOL_HALO + d:POOL_HALO + d + tm, sl]
        yg = (win / cnt - v_ref[POOL_HALO:POOL_HALO + tm, sl]) * ps_ref[:, sl]
        pooled_ref[:, sl] = yg.astype(BF16)
    for rows in halves:
        m = m_ref[rows, :] + jnp.dot(pooled_ref[rows, :], wo_ref[0:pool_w, :],
                                     preferred_element_type=F32)
        o_ref[rows, :] = x_ref[rows, :] + _rms(m, g_ref[...])


def _mix_out(x, u_pool, attn, four, wo, pw, ps, g, tm):
    B, L, D = x.shape
    widths = (u_pool.shape[-1], attn.shape[-1], four.shape[-1])
    hb = tm // POOL_HALO
    n_hb = L // POOL_HALO
    row = lambda b, i: (b, i, 0)
    const2 = lambda b, i: (0, 0)
    kern = functools.partial(_mix_out_kernel, tm=tm, seq_len=L, widths=widths)
    return pl.pallas_call(
        kern,
        grid=(B, L // tm),
        in_specs=[
            pl.BlockSpec((None, POOL_HALO, widths[0]), lambda b, i: (b, jnp.maximum(i * hb - 1, 0), 0)),
            pl.BlockSpec((None, tm, widths[0]), row),
            pl.BlockSpec((None, POOL_HALO, widths[0]),
                         lambda b, i: (b, jnp.minimum((i + 1) * hb, n_hb - 1), 0)),
            pl.BlockSpec((None, tm, widths[1]), row),
            pl.BlockSpec((None, tm, widths[2]), row),
            pl.BlockSpec((None, tm, D), row),
            pl.BlockSpec(wo.shape, const2),
            pl.BlockSpec(pw.shape, lambda b, i: (0, 0, 0)),
            pl.BlockSpec((1, widths[0]), const2),
            pl.BlockSpec((1, D), const2),
        ],
        out_specs=pl.BlockSpec((None, tm, D), row),
        out_shape=jax.ShapeDtypeStruct((B, L, D), F32),
        scratch_shapes=[
            pltpu.VMEM((tm + 2 * POOL_HALO, widths[0]), F32),
            pltpu.VMEM((tm + 2 * POOL_HALO, widths[0]), F32),
            pltpu.VMEM((tm, D), F32),
            pltpu.VMEM((tm, widths[0]), BF16),
        ],
        compiler_params=_params("parallel", "parallel"),
        name="mix_out",
    )(u_pool, u_pool, u_pool, attn, four, x, wo, pw, ps, g)


def _gelu_tanh(x):
    return 0.5 * x * (1.0 + jnp.tanh(math.sqrt(2.0 / math.pi) * (x + 0.044715 * (x * x * x))))


def _ffn_kernel(xp_ref, x_ref, xn_ref, gpre_ref, wg_ref, wv_ref, cwg_ref, cwv_ref, cbg_ref, cbv_ref,
                wd_ref, gpost_ref, o_ref, h_ref, ug_ref, uv_ref, *, tm, sub):
    i = pl.program_id(1)
    j = pl.program_id(2)
    rows = tm + 2 * CONV_HALO

    @pl.when(j == 0)
    def _():
        g = gpre_ref[...]
        hp = jnp.where(i > 0, _rms(xp_ref[...], g), 0.0)
        hn = jnp.where(i < pl.num_programs(1) - 1, _rms(xn_ref[...], g), 0.0)
        h_ref[0:CONV_HALO, :] = hp.astype(BF16)
        h_ref[CONV_HALO:CONV_HALO + tm, :] = _rms(x_ref[...], g).astype(BF16)
        h_ref[CONV_HALO + tm:, :] = hn.astype(BF16)
        o_ref[...] = jnp.zeros(o_ref.shape, F32)

    def conv(u_ref, cw_ref, cb_ref, r0, nr):
        prev = u_ref[CONV_HALO - 1 + r0:CONV_HALO - 1 + r0 + nr, :]
        cur = u_ref[CONV_HALO + r0:CONV_HALO + r0 + nr, :]
        nxt = u_ref[CONV_HALO + 1 + r0:CONV_HALO + 1 + r0 + nr, :]
        return prev * cw_ref[0:1, :] + cur * cw_ref[1:2, :] + nxt * cw_ref[2:3, :] + cb_ref[...]

    h = h_ref[...]
    n_sub = wg_ref.shape[1] // sub
    nr = tm // FFN_ROW_BLOCKS
    for s in range(n_sub):
        sl = slice(s * sub, (s + 1) * sub)
        ug_ref[s] = jnp.dot(h, wg_ref[:, sl], preferred_element_type=F32)
        uv_ref[s] = jnp.dot(h, wv_ref[:, sl], preferred_element_type=F32)
    for s in range(n_sub):
        sl = slice(s * sub, (s + 1) * sub)
        for r0 in range(0, tm, nr):
            rows = slice(r0, r0 + nr)
            gate = conv(ug_ref.at[s], cwg_ref.at[:, sl], cbg_ref.at[:, sl], r0, nr)
            val = conv(uv_ref.at[s], cwv_ref.at[:, sl], cbv_ref.at[:, sl], r0, nr)
            act = (_gelu_tanh(gate) * val).astype(BF16)
            o_ref[rows, :] += jnp.dot(act, wd_ref[sl, :], preferred_element_type=F32)

    @pl.when(j == pl.num_programs(2) - 1)
    def _():
        o_ref[...] = x_ref[...] + _rms(o_ref[...], gpost_ref[...])


def _ffn(x, gpre, w_up, conv_w, conv_b, w_down, gpost, tm, chunk):
    B, L, D = x.shape
    d_ff = w_down.shape[0]
    nc = d_ff // chunk
    hb = tm // CONV_HALO
    n_hb = L // CONV_HALO
    const2 = lambda b, i, j: (0, 0)
    gate_col = lambda b, i, j: (0, j)
    val_col = lambda b, i, j: (0, nc + j)
    sub = min(FFN_SUB_CHUNK, chunk)
    kern = functools.partial(_ffn_kernel, tm=tm, sub=sub)
    return pl.pallas_call(
        kern,
        grid=(B, L // tm, nc),
        in_specs=[
            pl.BlockSpec((None, CONV_HALO, D), lambda b, i, j: (b, jnp.maximum(i * hb - 1, 0), 0)),
            pl.BlockSpec((None, tm, D), lambda b, i, j: (b, i, 0), pipeline_mode=pl.Buffered(1)),
            pl.BlockSpec((None, CONV_HALO, D),
                         lambda b, i, j: (b, jnp.minimum((i + 1) * hb, n_hb - 1), 0)),
            pl.BlockSpec((1, D), const2),
            pl.BlockSpec((D, chunk), gate_col),
            pl.BlockSpec((D, chunk), val_col),
            pl.BlockSpec((3, chunk), gate_col),
            pl.BlockSpec((3, chunk), val_col),
            pl.BlockSpec((1, chunk), gate_col),
            pl.BlockSpec((1, chunk), val_col),
            pl.BlockSpec((chunk, D), lambda b, i, j: (j, 0)),
            pl.BlockSpec((1, D), const2),
        ],
        out_specs=pl.BlockSpec((None, tm, D), lambda b, i, j: (b, i, 0)),
        out_shape=jax.ShapeDtypeStruct((B, L, D), F32),
        scratch_shapes=[
            pltpu.VMEM((tm + 2 * CONV_HALO, D), BF16),
            pltpu.VMEM((chunk // sub, tm + 2 * CONV_HALO, sub), F32),
            pltpu.VMEM((chunk // sub, tm + 2 * CONV_HALO, sub), F32),
        ],
        compiler_params=_params("parallel", "parallel", "arbitrary"),
        name="ffn",
    )(x, x, x, gpre, w_up, w_up, conv_w, conv_w, conv_b, conv_b, w_down, gpost)


def _rope_tables(seq_len):
    quarter = HEAD_DIM // 4
    t = jnp.arange(seq_len, dtype=jnp.int32)
    row = (t // GRID_W).astype(F32)
    col = (t % GRID_W).astype(F32)
    inv_freq = 1.0 / (ROPE_THETA ** (jnp.arange(quarter, dtype=F32) / quarter))
    ang = jnp.concatenate([row[:, None] * inv_freq[None, :], col[:, None] * inv_freq[None, :]], axis=-1)
    cos, sin = jnp.cos(ang), jnp.sin(ang)
    return jnp.concatenate([cos, cos], axis=-1), jnp.concatenate([-sin, sin], axis=-1)


def _fourier_split(seq_len):
    l2 = 1 << (int(math.log2(seq_len)) // 2)
    return seq_len // l2, l2


def _trunk(x, layers):
    B, L, D = x.shape
    ts = _tiles(L)
    cc, ss = _rope_tables(L)
    l1, l2 = _fourier_split(L)
    for p in layers:
        u_pool, qt, k, vt, u_four = _in_proj(x, p["g_pre_mix"], p["w_in"], cc, ss, p["q_norm"],
                                             p["k_norm"], p["widths"], ts["tm_proj"])
        attn = _attention(qt, k, vt, ts["tq"])
        four = _fourier(u_four, p["fourier_w"], l1, l2)
        x = _mix_out(x, u_pool, attn, four, p["w_out"], p["pool_w"], p["pool_scale"],
                     p["g_post_mix"], ts["tm_mix"])
        x = _ffn(x, p["g_pre_ffn"], p["w_up"], p["conv_w"], p["conv_b"], p["w_down"],
                 p["g_post_ffn"], ts["tm_ffn"], ts["ff_chunk"])
    return x


def kernel(x_prompt, x_sample, g_pre_mix, g_post_mix, w_in, pool_w, pool_scale, q_norm, k_norm,
           fourier_w, w_out, g_pre_ffn, g_post_ffn, w_up, conv_w, conv_b, w_down):
    depth = w_in.shape[0]
    pool_width = pool_scale.shape[-1]
    four_width = fourier_w.shape[1] * fourier_w.shape[2]
    kv_width = N_KV_HEADS * HEAD_DIM
    q_width = w_in.shape[-1] - pool_width - four_width - 2 * kv_width
    layers = []
    for l in range(depth):
        layers.append(dict(
            widths=(pool_width, q_width, kv_width, four_width),
            g_pre_mix=g_pre_mix[l][None, :], g_post_mix=g_post_mix[l][None, :],
            w_in=w_in[l].astype(BF16), pool_w=pool_w[l].astype(BF16),
            pool_scale=pool_scale[l][None, :], q_norm=q_norm[l][None, :], k_norm=k_norm[l][None, :],
            fourier_w=fourier_w[l].astype(BF16), w_out=w_out[l].astype(BF16),
            g_pre_ffn=g_pre_ffn[l][None, :], g_post_ffn=g_post_ffn[l][None, :],
            w_up=w_up[l].astype(BF16), conv_w=conv_w[l], conv_b=conv_b[l][None, :],
            w_down=w_down[l].astype(BF16)))
    return _trunk(x_prompt, layers), _trunk(x_sample, layers)
```

```python
import functools
import math

import jax
import jax.numpy as jnp
import numpy as np
from jax import lax
from jax.experimental import pallas as pl
from jax.experimental.pallas import tpu as pltpu

F32 = jnp.float32
BF16 = jnp.bfloat16

NORM_EPS = 1e-6
GRID_W = 64
HEAD_DIM = 128
N_KV_HEADS = 2
GQA_GROUP = 4
POOL_WINDOWS = (2, 4, 8, 16)
POOL_HALO = 8
GROUP_CH = 128
N_GROUPS = 4
ROPE_THETA = 10000.0
CONV_HALO = 16
ATTN_CHUNKS_PER_TRIP = 4
FOURIER_S2_PER_STEP = 8
FFN_ROW_BLOCKS = 2
FFN_SUB_CHUNK = 256

V7X_VMEM_BYTES = 64 * 1024 * 1024
VMEM_LIMIT = V7X_VMEM_BYTES - 4 * 1024 * 1024


def _tiles(seq_len):
    return dict(
        tm_proj=min(512, seq_len),
        tm_mix=min(512, seq_len),
        tm_ffn=min(1024, seq_len),
        ff_chunk=512,
        tq=min(512, seq_len),
    )


def _params(*sem):
    return pltpu.CompilerParams(dimension_semantics=sem, vmem_limit_bytes=VMEM_LIMIT)


def _rms(x, g):
    return x * lax.rsqrt(jnp.mean(x * x, axis=-1, keepdims=True) + NORM_EPS) * g


def _in_proj_kernel(x_ref, g_ref, w_ref, cc_ref, ss_ref, qn_ref, kn_ref,
                    pool_ref, q_ref, k_ref, v_ref, four_ref, *, widths, scale):
    pool_w, q_w, kv_w, four_w = widths
    h = _rms(x_ref[...], g_ref[...]).astype(BF16)
    cc = cc_ref[...]
    ss = ss_ref[...]

    def seg(lo, width):
        return jnp.dot(h, w_ref[:, lo:lo + width], preferred_element_type=F32)

    def norm_rope(zh, gain, out_scale):
        y = zh * lax.rsqrt(jnp.mean(zh * zh, axis=-1, keepdims=True) + NORM_EPS) * gain
        y = y * cc + pltpu.roll(y, HEAD_DIM // 2, axis=1) * ss
        return y * out_scale

    pair = 2 * HEAD_DIM
    for p in range(q_w // pair):
        z = seg(pool_w + p * pair, pair)
        for s in range(2):
            zh = z[:, s * HEAD_DIM:(s + 1) * HEAD_DIM]
            q_ref[2 * p + s] = norm_rope(zh, qn_ref[...], scale).T.astype(BF16)
    z = seg(pool_w + q_w, kv_w)
    for s in range(kv_w // HEAD_DIM):
        zh = z[:, s * HEAD_DIM:(s + 1) * HEAD_DIM]
        k_ref[:, s * HEAD_DIM:(s + 1) * HEAD_DIM] = norm_rope(zh, kn_ref[...], 1.0).astype(BF16)
    z = seg(pool_w + q_w + kv_w, kv_w)
    for s in range(kv_w // HEAD_DIM):
        v_ref[s] = z[:, s * HEAD_DIM:(s + 1) * HEAD_DIM].T.astype(BF16)
    four_ref[...] = seg(pool_w + q_w + 2 * kv_w, four_w)
    pool_ref[...] = seg(0, pool_w)


def _in_proj(x, g, w, cc, ss, qn, kn, widths, tm):
    B, L, D = x.shape
    pool_w, q_w, kv_w, four_w = widths
    nt = L // tm
    row = lambda b, i: (b, i, 0)
    const2 = lambda b, i: (0, 0)
    n_q = q_w // HEAD_DIM
    v_rows = HEAD_DIM
    kern = functools.partial(_in_proj_kernel, widths=widths, scale=HEAD_DIM ** -0.5 * math.log2(math.e))
    return pl.pallas_call(
        kern,
        grid=(B, nt),
        in_specs=[
            pl.BlockSpec((None, tm, D), row),
            pl.BlockSpec((1, D), const2),
            pl.BlockSpec(w.shape, const2),
            pl.BlockSpec((tm, HEAD_DIM), lambda b, i: (i, 0)),
            pl.BlockSpec((tm, HEAD_DIM), lambda b, i: (i, 0)),
            pl.BlockSpec((1, HEAD_DIM), const2),
            pl.BlockSpec((1, HEAD_DIM), const2),
        ],
        out_specs=[
            pl.BlockSpec((None, tm, pool_w), row),
            pl.BlockSpec((None, n_q, HEAD_DIM, tm), lambda b, i: (b, 0, 0, i)),
            pl.BlockSpec((None, tm, kv_w), row),
            pl.BlockSpec((None, N_KV_HEADS, None, v_rows, tm), lambda b, i: (b, 0, i, 0, 0)),
            pl.BlockSpec((None, tm, four_w), row),
        ],
        out_shape=[
            jax.ShapeDtypeStruct((B, L, pool_w), F32),
            jax.ShapeDtypeStruct((B, n_q, HEAD_DIM, L), BF16),
            jax.ShapeDtypeStruct((B, L, kv_w), BF16),
            jax.ShapeDtypeStruct((B, N_KV_HEADS, nt, v_rows, tm), BF16),
            jax.ShapeDtypeStruct((B, L, four_w), F32),
        ],
        compiler_params=_params("parallel", "parallel"),
        name="in_proj",
    )(x, g, w, cc, ss, qn, kn)


def _attn_kernel(qt_ref, k_ref, vt_ref, o_ref, m_ref, l_ref, acc_ref, sa_ref, sb_ref, ma_ref, mb_ref,
                 *, tk, nk, unroll):
    assert nk == 1 or (unroll % 2 == 0 and nk % unroll == 0)
    m_ref[...] = jnp.full(m_ref.shape, -jnp.inf, F32)
    l_ref[...] = jnp.zeros(l_ref.shape, F32)
    acc_ref[...] = jnp.zeros(acc_ref.shape, F32)

    def scores(c, g, s_ref, cm_ref):
        kc = k_ref[pl.ds(pl.multiple_of(c * tk, tk), tk), :]
        st = jnp.dot(kc, qt_ref[g], preferred_element_type=F32)
        s_ref[g] = st
        cm_ref[g] = jnp.max(st, axis=0, keepdims=True)

    def accumulate(c, g, s_ref, cm_ref):
        m_old = m_ref[g]
        m_new = jnp.maximum(m_old, cm_ref[g])
        alpha = jnp.exp2(m_old - m_new)
        p = jnp.exp2(s_ref[g] - m_new)
        l_ref[g] = alpha * l_ref[g] + jnp.sum(p, axis=0, keepdims=True)
        acc_ref[g] = alpha * acc_ref[g] + jnp.dot(vt_ref[c], p.astype(BF16),
                                                  preferred_element_type=F32)
        m_ref[g] = m_new

    def step(c_scores, s_next, m_next, c_acc, s_cur, m_cur):
        for g in range(GQA_GROUP):
            if c_scores is not None:
                scores(c_scores, g, s_next, m_next)
            if c_acc is not None:
                accumulate(c_acc, g, s_cur, m_cur)

    bufs = ((sa_ref, ma_ref), (sb_ref, mb_ref))
    step(0, *bufs[0], None, None, None)
    if nk > 1:
        def group(i, carry):
            for u in range(unroll):
                c = unroll * i + u
                step(c + 1, *bufs[(u + 1) % 2], c, *bufs[u % 2])
            return carry

        lax.fori_loop(0, nk // unroll - 1, group, 0)
        for c in range(nk - unroll, nk - 1):
            step(c + 1, *bufs[(c + 1) % 2], c, *bufs[c % 2])
    step(None, None, None, nk - 1, *bufs[(nk - 1) % 2])
    for g in range(GQA_GROUP):
        o = acc_ref[g] / l_ref[g]
        o_ref[:, g * HEAD_DIM:(g + 1) * HEAD_DIM] = o.T.astype(o_ref.dtype)


def _attention(qt, k, vt, tq):
    B, n_q, _, L = qt.shape
    _, _, nk, v_rows, tk = vt.shape
    gw = GQA_GROUP * HEAD_DIM
    unroll = ATTN_CHUNKS_PER_TRIP if nk % ATTN_CHUNKS_PER_TRIP == 0 and nk >= 4 * ATTN_CHUNKS_PER_TRIP else 2
    kern = functools.partial(_attn_kernel, tk=tk, nk=nk, unroll=unroll)
    return pl.pallas_call(
        kern,
        grid=(B, N_KV_HEADS, L // tq),
        in_specs=[
            pl.BlockSpec((None, GQA_GROUP, HEAD_DIM, tq), lambda b, j, i: (b, j, 0, i)),
            pl.BlockSpec((None, L, HEAD_DIM), lambda b, j, i: (b, 0, j)),
            pl.BlockSpec((None, None, nk, v_rows, tk), lambda b, j, i: (b, j, 0, 0, 0)),
        ],
        out_specs=pl.BlockSpec((None, tq, gw), lambda b, j, i: (b, i, j)),
        out_shape=jax.ShapeDtypeStruct((B, L, n_q * HEAD_DIM), BF16),
        scratch_shapes=[
            pltpu.VMEM((GQA_GROUP, 1, tq), F32),
            pltpu.VMEM((GQA_GROUP, 1, tq), F32),
            pltpu.VMEM((GQA_GROUP, v_rows, tq), F32),
            pltpu.VMEM((GQA_GROUP, tk, tq), F32),
            pltpu.VMEM((GQA_GROUP, tk, tq), F32),
            pltpu.VMEM((GQA_GROUP, 1, tq), F32),
            pltpu.VMEM((GQA_GROUP, 1, tq), F32),
        ],
        compiler_params=_params("parallel", "parallel", "arbitrary"),
        name="attention",
    )(qt, k, vt)


def _dft_cos_sin(n):
    idx = np.arange(n)
    ang = 2.0 * np.pi * ((idx[:, None] * idx[None, :]) % n) / n
    return np.cos(ang), np.sin(ang)


def _fourier_a_kernel(x_ref, f1_ref, twc_ref, tws_ref, zr_ref, zi_ref, *, l2, k):
    xt = pltpu.einshape("mkw->kmw", x_ref[...])
    zr, zi = [], []
    for t in range(k):
        z = jnp.dot(f1_ref[...], xt[t].astype(BF16), preferred_element_type=F32)
        a = z[:l2]
        b = z[l2:]
        c = twc_ref[:, t:t + 1]
        s = tws_ref[:, t:t + 1]
        zr.append(a * c + b * s)
        zi.append(b * c - a * s)
    zr_ref[...] = pltpu.einshape("kmw->mkw", jnp.stack(zr))
    zi_ref[...] = pltpu.einshape("kmw->mkw", jnp.stack(zi))


def _fourier_b_kernel(zr_ref, zi_ref, f2_ref, fc_ref, fw_ref, o_ref, hr_ref, hi_ref, *, l1, n_s2, norm):
    for a in range(n_s2):
        rows = slice(a * l1, (a + 1) * l1)
        zz = jnp.concatenate([zr_ref[rows, :].astype(BF16), zi_ref[rows, :].astype(BF16)], axis=0)
        hh = jnp.dot(f2_ref[...], zz, preferred_element_type=F32)
        hr_ref[rows, :] = hh[:l1].astype(BF16)
        hi_ref[rows, :] = hh[l1:].astype(BF16)
    ys = []
    for g in range(N_GROUPS):
        sl = slice(g * GROUP_CH, (g + 1) * GROUP_CH)
        hg = jnp.concatenate([hr_ref[:, sl], hi_ref[:, sl]], axis=1)
        f = jnp.dot(hg, fc_ref[...], preferred_element_type=F32) * norm
        ys.append(jnp.dot(f.astype(BF16), fw_ref[g], preferred_element_type=F32))
    y = jnp.concatenate(ys, axis=1).reshape(n_s2, l1, o_ref.shape[-1])
    o_ref[...] = pltpu.einshape("alw->law", y)


def _fourier(u, fw, l1, l2):
    B, L, width = u.shape
    k = min(FOURIER_S2_PER_STEP, l1)
    n_s2 = min(FOURIER_S2_PER_STEP, l2)
    c2, s2 = _dft_cos_sin(l2)
    f1 = jnp.asarray(np.concatenate([c2, -s2], axis=0), BF16)
    c1, s1 = _dft_cos_sin(l1)
    f2 = jnp.asarray(np.block([[c1, s1], [-s1, c1]]), BF16)
    cc, sc = _dft_cos_sin(GROUP_CH)
    fc = jnp.asarray(np.concatenate([cc, sc], axis=0), BF16)
    ang = 2.0 * np.pi * ((np.arange(l2)[:, None] * np.arange(l1)[None, :]) % L) / L
    tw = lambda t: jnp.asarray(t.reshape(l2, l1 // k, k).transpose(1, 0, 2), F32)
    twc, tws = tw(np.cos(ang)), tw(np.sin(ang))

    x = u.reshape(B, l2, l1, width)
    col = lambda b, j: (b, 0, j, 0)
    const2 = lambda b, j: (0, 0)
    zr, zi = pl.pallas_call(
        functools.partial(_fourier_a_kernel, l2=l2, k=k),
        grid=(B, l1 // k),
        in_specs=[
            pl.BlockSpec((None, l2, k, width), col),
            pl.BlockSpec(f1.shape, const2),
            pl.BlockSpec((None, l2, k), lambda b, j: (j, 0, 0)),
            pl.BlockSpec((None, l2, k), lambda b, j: (j, 0, 0)),
        ],
        out_specs=[pl.BlockSpec((None, l2, k, width), col)] * 2,
        out_shape=[jax.ShapeDtypeStruct((B, l2, l1, width), F32)] * 2,
        compiler_params=_params("parallel", "parallel"),
        name="fourier_a",
    )(x, f1, twc, tws)

    zr = zr.reshape(B, L, width)
    zi = zi.reshape(B, L, width)
    rows = lambda b, j: (b, j, 0)
    out = pl.pallas_call(
        functools.partial(_fourier_b_kernel, l1=l1, n_s2=n_s2, norm=1.0 / math.sqrt(L * GROUP_CH)),
        grid=(B, l2 // n_s2),
        in_specs=[
            pl.BlockSpec((None, n_s2 * l1, width), rows),
            pl.BlockSpec((None, n_s2 * l1, width), rows),
            pl.BlockSpec(f2.shape, const2),
            pl.BlockSpec(fc.shape, const2),
            pl.BlockSpec(fw.shape, lambda b, j: (0, 0, 0)),
        ],
        out_specs=pl.BlockSpec((None, l1, n_s2, width), lambda b, j: (b, 0, j, 0)),
        out_shape=jax.ShapeDtypeStruct((B, l1, l2, width), F32),
        scratch_shapes=[pltpu.VMEM((n_s2 * l1, width), BF16)] * 2,
        compiler_params=_params("parallel", "parallel"),
        name="fourier_b",
    )(zr, zi, f2, fc, fw)
    return out.reshape(B, L, width)


def _mix_out_kernel(up_ref, uc_ref, un_ref, attn_ref, four_ref, x_ref, wo_ref, pw_ref, ps_ref,
                    g_ref, o_ref, ext_ref, v_ref, m_ref, pooled_ref, *, tm, seq_len, widths):
    pool_w, attn_w, four_w = widths
    i = pl.program_id(1)
    last = pl.num_programs(1) - 1
    ext_ref[0:POOL_HALO, :] = jnp.where(i > 0, up_ref[...], 0.0)
    ext_ref[POOL_HALO:POOL_HALO + tm, :] = uc_ref[...]
    ext_ref[POOL_HALO + tm:, :] = jnp.where(i < last, un_ref[...], 0.0)

    halves = [slice(r * (tm // 2), (r + 1) * (tm // 2)) for r in range(2)]
    t = i * tm + lax.broadcasted_iota(jnp.int32, (tm, 1), 0)
    for g in range(len(POOL_WINDOWS)):
        sl = slice(g * GROUP_CH, (g + 1) * GROUP_CH)
        ug = ext_ref[:, sl]
        hi = ug.astype(BF16)
        lo = (ug - hi.astype(F32)).astype(BF16)
        pw2 = jnp.concatenate([pw_ref[g], pw_ref[g]], axis=0)
        v_ref[:, sl] = jnp.dot(jnp.concatenate([hi, lo], axis=1), pw2, preferred_element_type=F32)

    for rows in halves:
        m_ref[rows, :] = (
            jnp.dot(attn_ref[rows, :], wo_ref[pool_w:pool_w + attn_w, :], preferred_element_type=F32)
            + jnp.dot(four_ref[rows, :].astype(BF16), wo_ref[pool_w + attn_w:, :],
                      preferred_element_type=F32))

    for g, w in enumerate(POOL_WINDOWS):
        sl = slice(g * GROUP_CH, (g + 1) * GROUP_CH)
        lo = jnp.maximum(t - w // 2, 0)
        hi = jnp.minimum(t + (w - 1 - w // 2), seq_len - 1)
        cnt = (hi - lo + 1).astype(F32)
        win = v_ref[POOL_HALO - w // 2:POOL_HALO - w // 2 + tm, sl]
        for d in range(1 - w // 2, w - w // 2):
            win = win + v_ref[POOL_HALO + d:POOL_HALO + d + tm, sl]
        yg = (win / cnt - v_ref[POOL_HALO:POOL_HALO + tm, sl]) * ps_ref[:, sl]
        pooled_ref[:, sl] = yg.astype(BF16)
    for rows in halves:
        m = m_ref[rows, :] + jnp.dot(pooled_ref[rows, :], wo_ref[0:pool_w, :],
                                     preferred_element_type=F32)
        o_ref[rows, :] = x_ref[rows, :] + _rms(m, g_ref[...])


def _mix_out(x, u_pool, attn, four, wo, pw, ps, g, tm):
    B, L, D = x.shape
    widths = (u_pool.shape[-1], attn.shape[-1], four.shape[-1])
    hb = tm // POOL_HALO
    n_hb = L // POOL_HALO
    row = lambda b, i: (b, i, 0)
    const2 = lambda b, i: (0, 0)
    kern = functools.partial(_mix_out_kernel, tm=tm, seq_len=L, widths=widths)
    return pl.pallas_call(
        kern,
        grid=(B, L // tm),
        in_specs=[
            pl.BlockSpec((None, POOL_HALO, widths[0]), lambda b, i: (b, jnp.maximum(i * hb - 1, 0), 0)),
            pl.BlockSpec((None, tm, widths[0]), row),
            pl.BlockSpec((None, POOL_HALO, widths[0]),
                         lambda b, i: (b, jnp.minimum((i + 1) * hb, n_hb - 1), 0)),
            pl.BlockSpec((None, tm, widths[1]), row),
            pl.BlockSpec((None, tm, widths[2]), row),
            pl.BlockSpec((None, tm, D), row),
            pl.BlockSpec(wo.shape, const2),
            pl.BlockSpec(pw.shape, lambda b, i: (0, 0, 0)),
            pl.BlockSpec((1, widths[0]), const2),
            pl.BlockSpec((1, D), const2),
        ],
        out_specs=pl.BlockSpec((None, tm, D), row),
        out_shape=jax.ShapeDtypeStruct((B, L, D), F32),
        scratch_shapes=[
            pltpu.VMEM((tm + 2 * POOL_HALO, widths[0]), F32),
            pltpu.VMEM((tm + 2 * POOL_HALO, widths[0]), F32),
            pltpu.VMEM((tm, D), F32),
            pltpu.VMEM((tm, widths[0]), BF16),
        ],
        compiler_params=_params("parallel", "parallel"),
        name="mix_out",
    )(u_pool, u_pool, u_pool, attn, four, x, wo, pw, ps, g)


def _gated_gelu(gate, half_val):
    c0 = math.sqrt(2.0 / math.pi)
    inner = gate * (gate * gate * (c0 * 0.044715) + c0)
    return gate * (1.0 + jnp.tanh(inner)) * half_val


def _ffn_kernel(xp_ref, x_ref, xn_ref, gpre_ref, wg_ref, wv_ref, cwg_ref, cwv_ref, cbg_ref, cbv_ref,
                wd_ref, gpost_ref, o_ref, h_ref, ug_ref, uv_ref, *, tm, sub):
    i = pl.program_id(1)
    j = pl.program_id(2)
    rows = tm + 2 * CONV_HALO

    @pl.when(j == 0)
    def _():
        g = gpre_ref[...]
        hp = jnp.where(i > 0, _rms(xp_ref[...], g), 0.0)
        hn = jnp.where(i < pl.num_programs(1) - 1, _rms(xn_ref[...], g), 0.0)
        h_ref[0:CONV_HALO, :] = hp.astype(BF16)
        h_ref[CONV_HALO:CONV_HALO + tm, :] = _rms(x_ref[...], g).astype(BF16)
        h_ref[CONV_HALO + tm:, :] = hn.astype(BF16)
        o_ref[...] = jnp.zeros(o_ref.shape, F32)

    def conv(u_ref, cw, cb, r0, nr):
        prev = u_ref[CONV_HALO - 1 + r0:CONV_HALO - 1 + r0 + nr, :]
        cur = u_ref[CONV_HALO + r0:CONV_HALO + r0 + nr, :]
        nxt = u_ref[CONV_HALO + 1 + r0:CONV_HALO + 1 + r0 + nr, :]
        return prev * cw[0:1, :] + cur * cw[1:2, :] + nxt * cw[2:3, :] + cb

    def step(finish):
        h = h_ref[...]
        n_sub = wg_ref.shape[1] // sub
        nr = tm // FFN_ROW_BLOCKS
        for s in range(n_sub):
            sl = slice(s * sub, (s + 1) * sub)
            ug_ref[s] = jnp.dot(h, wg_ref[:, sl], preferred_element_type=F32)
            uv_ref[s] = jnp.dot(h, wv_ref[:, sl], preferred_element_type=F32)
        for s in range(n_sub):
            sl = slice(s * sub, (s + 1) * sub)
            cwg, cbg = cwg_ref[:, sl], cbg_ref[:, sl]
            cwv, cbv = 0.5 * cwv_ref[:, sl], 0.5 * cbv_ref[:, sl]
            for r0 in range(0, tm, nr):
                rows = slice(r0, r0 + nr)
                gate = conv(ug_ref.at[s], cwg, cbg, r0, nr)
                half_val = conv(uv_ref.at[s], cwv, cbv, r0, nr)
                act = _gated_gelu(gate, half_val).astype(BF16)
                d = jnp.dot(act, wd_ref[sl, :], preferred_element_type=F32)
                if finish and s == n_sub - 1:
                    o_ref[rows, :] = x_ref[rows, :] + _rms(o_ref[rows, :] + d, gpost_ref[...])
                else:
                    o_ref[rows, :] += d

    last = pl.num_programs(2) - 1
    pl.when(j < last)(functools.partial(step, False))
    pl.when(j == last)(functools.partial(step, True))


def _ffn(x, gpre, w_up, conv_w, conv_b, w_down, gpost, tm, chunk):
    B, L, D = x.shape
    d_ff = w_down.shape[0]
    nc = d_ff // chunk
    hb = tm // CONV_HALO
    n_hb = L // CONV_HALO
    const2 = lambda b, i, j: (0, 0)
    gate_col = lambda b, i, j: (0, j)
    val_col = lambda b, i, j: (0, nc + j)
    sub = min(FFN_SUB_CHUNK, chunk)
    kern = functools.partial(_ffn_kernel, tm=tm, sub=sub)
    return pl.pallas_call(
        kern,
        grid=(B, L // tm, nc),
        in_specs=[
            pl.BlockSpec((None, CONV_HALO, D), lambda b, i, j: (b, jnp.maximum(i * hb - 1, 0), 0)),
            pl.BlockSpec((None, tm, D), lambda b, i, j: (b, i, 0), pipeline_mode=pl.Buffered(1)),
            pl.BlockSpec((None, CONV_HALO, D),
                         lambda b, i, j: (b, jnp.minimum((i + 1) * hb, n_hb - 1), 0)),
            pl.BlockSpec((1, D), const2),
            pl.BlockSpec((D, chunk), gate_col),
            pl.BlockSpec((D, chunk), val_col),
            pl.BlockSpec((3, chunk), gate_col),
            pl.BlockSpec((3, chunk), val_col),
            pl.BlockSpec((1, chunk), gate_col),
            pl.BlockSpec((1, chunk), val_col),
            pl.BlockSpec((chunk, D), lambda b, i, j: (j, 0)),
            pl.BlockSpec((1, D), const2),
        ],
        out_specs=pl.BlockSpec((None, tm, D), lambda b, i, j: (b, i, 0)),
        out_shape=jax.ShapeDtypeStruct((B, L, D), F32),
        scratch_shapes=[
            pltpu.VMEM((tm + 2 * CONV_HALO, D), BF16),
            pltpu.VMEM((chunk // sub, tm + 2 * CONV_HALO, sub), F32),
            pltpu.VMEM((chunk // sub, tm + 2 * CONV_HALO, sub), F32),
        ],
        compiler_params=_params("parallel", "parallel", "arbitrary"),
        name="ffn",
    )(x, x, x, gpre, w_up, w_up, conv_w, conv_w, conv_b, conv_b, w_down, gpost)


def _rope_tables(seq_len):
    quarter = HEAD_DIM // 4
    t = jnp.arange(seq_len, dtype=jnp.int32)
    row = (t // GRID_W).astype(F32)
    col = (t % GRID_W).astype(F32)
    inv_freq = 1.0 / (ROPE_THETA ** (jnp.arange(quarter, dtype=F32) / quarter))
    ang = jnp.concatenate([row[:, None] * inv_freq[None, :], col[:, None] * inv_freq[None, :]], axis=-1)
    cos, sin = jnp.cos(ang), jnp.sin(ang)
    return jnp.concatenate([cos, cos], axis=-1), jnp.concatenate([-sin, sin], axis=-1)


def _fourier_split(seq_len):
    l2 = 1 << (int(math.log2(seq_len)) // 2)
    return seq_len // l2, l2


def _trunk(x, layers):
    B, L, D = x.shape
    ts = _tiles(L)
    cc, ss = _rope_tables(L)
    l1, l2 = _fourier_split(L)
    for p in layers:
        u_pool, qt, k, vt, u_four = _in_proj(x, p["g_pre_mix"], p["w_in"], cc, ss, p["q_norm"],
                                             p["k_norm"], p["widths"], ts["tm_proj"])
        attn = _attention(qt, k, vt, ts["tq"])
        four = _fourier(u_four, p["fourier_w"], l1, l2)
        x = _mix_out(x, u_pool, attn, four, p["w_out"], p["pool_w"], p["pool_scale"],
                     p["g_post_mix"], ts["tm_mix"])
        x = _ffn(x, p["g_pre_ffn"], p["w_up"], p["conv_w"], p["conv_b"], p["w_down"],
                 p["g_post_ffn"], ts["tm_ffn"], ts["ff_chunk"])
    return x


def kernel(x_prompt, x_sample, g_pre_mix, g_post_mix, w_in, pool_w, pool_scale, q_norm, k_norm,
           fourier_w, w_out, g_pre_ffn, g_post_ffn, w_up, conv_w, conv_b, w_down):
    depth = w_in.shape[0]
    pool_width = pool_scale.shape[-1]
    four_width = fourier_w.shape[1] * fourier_w.shape[2]
    kv_width = N_KV_HEADS * HEAD_DIM
    q_width = w_in.shape[-1] - pool_width - four_width - 2 * kv_width
    layers = []
    for l in range(depth):
        layers.append(dict(
            widths=(pool_width, q_width, kv_width, four_width),
            g_pre_mix=g_pre_mix[l][None, :], g_post_mix=g_post_mix[l][None, :],
            w_in=w_in[l].astype(BF16), pool_w=pool_w[l].astype(BF16),
            pool_scale=pool_scale[l][None, :], q_norm=q_norm[l][None, :], k_norm=k_norm[l][None, :],
            fourier_w=fourier_w[l].astype(BF16), w_out=w_out[l].astype(BF16),
            g_pre_ffn=g_pre_ffn[l][None, :], g_post_ffn=g_post_ffn[l][None, :],
            w_up=w_up[l].astype(BF16), conv_w=conv_w[l], conv_b=conv_b[l][None, :],
            w_down=w_down[l].astype(BF16)))
    return _trunk(x_prompt, layers), _trunk(x_sample, layers)
```

```python
import functools
import math

import jax
import jax.numpy as jnp
import numpy as np
from jax import lax
from jax.experimental import pallas as pl
from jax.experimental.pallas import tpu as pltpu

F32 = jnp.float32
BF16 = jnp.bfloat16

NORM_EPS = 1e-6
GRID_W = 64
HEAD_DIM = 128
N_KV_HEADS = 2
GQA_GROUP = 4
POOL_WINDOWS = (2, 4, 8, 16)
POOL_HALO = 8
GROUP_CH = 128
N_GROUPS = 4
ROPE_THETA = 10000.0
CONV_HALO = 16
ATTN_CHUNKS_PER_TRIP = 4
FOURIER_S2_PER_STEP = 8
FFN_ROW_BLOCKS = 2
FFN_SUB_CHUNK = 256

V7X_VMEM_BYTES = 64 * 1024 * 1024
VMEM_LIMIT = V7X_VMEM_BYTES - 4 * 1024 * 1024


def _tiles(seq_len):
    return dict(
        tm_proj=min(512, seq_len),
        tm_mix=min(512, seq_len),
        tm_ffn=min(1024, seq_len),
        ff_chunk=512,
        tq=min(512, seq_len),
    )


def _params(*sem):
    return pltpu.CompilerParams(dimension_semantics=sem, vmem_limit_bytes=VMEM_LIMIT)


def _rms(x, g):
    return x * lax.rsqrt(jnp.mean(x * x, axis=-1, keepdims=True) + NORM_EPS) * g


def _in_proj_kernel(x_ref, g_ref, w_ref, cc_ref, ss_ref, qn_ref, kn_ref,
                    pool_ref, q_ref, k_ref, v_ref, four_ref, *, widths, scale):
    pool_w, q_w, kv_w, four_w = widths
    h = _rms(x_ref[...], g_ref[...]).astype(BF16)
    cc = cc_ref[...]
    ss = ss_ref[...]

    def seg(lo, width):
        return jnp.dot(h, w_ref[:, lo:lo + width], preferred_element_type=F32)

    def norm_rope(zh, gain, out_scale):
        y = zh * lax.rsqrt(jnp.mean(zh * zh, axis=-1, keepdims=True) + NORM_EPS) * gain
        y = y * cc + pltpu.roll(y, HEAD_DIM // 2, axis=1) * ss
        return y * out_scale

    pair = 2 * HEAD_DIM
    for p in range(q_w // pair):
        z = seg(pool_w + p * pair, pair)
        for s in range(2):
            zh = z[:, s * HEAD_DIM:(s + 1) * HEAD_DIM]
            q_ref[2 * p + s] = norm_rope(zh, qn_ref[...], scale).T.astype(BF16)
    z = seg(pool_w + q_w, kv_w)
    for s in range(kv_w // HEAD_DIM):
        zh = z[:, s * HEAD_DIM:(s + 1) * HEAD_DIM]
        k_ref[:, s * HEAD_DIM:(s + 1) * HEAD_DIM] = norm_rope(zh, kn_ref[...], 1.0).astype(BF16)
    z = seg(pool_w + q_w + kv_w, kv_w)
    for s in range(kv_w // HEAD_DIM):
        v_ref[s] = z[:, s * HEAD_DIM:(s + 1) * HEAD_DIM].T.astype(BF16)
    four_ref[...] = seg(pool_w + q_w + 2 * kv_w, four_w)
    pool_ref[...] = seg(0, pool_w)


def _in_proj(x, g, w, cc, ss, qn, kn, widths, tm):
    B, L, D = x.shape
    pool_w, q_w, kv_w, four_w = widths
    assert L % tm == 0 and w.shape == (D, sum(widths) + kv_w)
    nt = L // tm
    row = lambda b, i: (b, i, 0)
    const2 = lambda b, i: (0, 0)
    n_q = q_w // HEAD_DIM
    v_rows = HEAD_DIM
    kern = functools.partial(_in_proj_kernel, widths=widths, scale=HEAD_DIM ** -0.5 * math.log2(math.e))
    return pl.pallas_call(
        kern,
        grid=(B, nt),
        in_specs=[
            pl.BlockSpec((None, tm, D), row),
            pl.BlockSpec((1, D), const2),
            pl.BlockSpec(w.shape, const2),
            pl.BlockSpec((tm, HEAD_DIM), lambda b, i: (i, 0)),
            pl.BlockSpec((tm, HEAD_DIM), lambda b, i: (i, 0)),
            pl.BlockSpec((1, HEAD_DIM), const2),
            pl.BlockSpec((1, HEAD_DIM), const2),
        ],
        out_specs=[
            pl.BlockSpec((None, tm, pool_w), row),
            pl.BlockSpec((None, n_q, HEAD_DIM, tm), lambda b, i: (b, 0, 0, i)),
            pl.BlockSpec((None, tm, kv_w), row),
            pl.BlockSpec((None, N_KV_HEADS, None, v_rows, tm), lambda b, i: (b, 0, i, 0, 0)),
            pl.BlockSpec((None, tm, four_w), row),
        ],
        out_shape=[
            jax.ShapeDtypeStruct((B, L, pool_w), F32),
            jax.ShapeDtypeStruct((B, n_q, HEAD_DIM, L), BF16),
            jax.ShapeDtypeStruct((B, L, kv_w), BF16),
            jax.ShapeDtypeStruct((B, N_KV_HEADS, nt, v_rows, tm), BF16),
            jax.ShapeDtypeStruct((B, L, four_w), F32),
        ],
        compiler_params=_params("parallel", "parallel"),
        name="in_proj",
    )(x, g, w, cc, ss, qn, kn)


def _attn_kernel(qt_ref, k_ref, vt_ref, o_ref, m_ref, l_ref, acc_ref, sa_ref, sb_ref, ma_ref, mb_ref,
                 *, tk, nk, unroll):
    assert nk == 1 or (unroll % 2 == 0 and nk % unroll == 0)
    m_ref[...] = jnp.full(m_ref.shape, -jnp.inf, F32)
    l_ref[...] = jnp.zeros(l_ref.shape, F32)
    acc_ref[...] = jnp.zeros(acc_ref.shape, F32)

    def scores(c, g, s_ref, cm_ref):
        kc = k_ref[pl.ds(pl.multiple_of(c * tk, tk), tk), :]
        st = jnp.dot(kc, qt_ref[g], preferred_element_type=F32)
        s_ref[g] = st
        cm_ref[g] = jnp.max(st, axis=0, keepdims=True)

    def accumulate(c, g, s_ref, cm_ref):
        m_old = m_ref[g]
        m_new = jnp.maximum(m_old, cm_ref[g])
        alpha = jnp.exp2(m_old - m_new)
        p = jnp.exp2(s_ref[g] - m_new)
        l_ref[g] = alpha * l_ref[g] + jnp.sum(p, axis=0, keepdims=True)
        acc_ref[g] = alpha * acc_ref[g] + jnp.dot(vt_ref[c], p.astype(BF16),
                                                  preferred_element_type=F32)
        m_ref[g] = m_new

    def step(c_scores, s_next, m_next, c_acc, s_cur, m_cur):
        for g in range(GQA_GROUP):
            if c_scores is not None:
                scores(c_scores, g, s_next, m_next)
            if c_acc is not None:
                accumulate(c_acc, g, s_cur, m_cur)

    bufs = ((sa_ref, ma_ref), (sb_ref, mb_ref))
    step(0, *bufs[0], None, None, None)
    if nk > 1:
        def group(i, carry):
            for u in range(unroll):
                c = unroll * i + u
                step(c + 1, *bufs[(u + 1) % 2], c, *bufs[u % 2])
            return carry

        lax.fori_loop(0, nk // unroll - 1, group, 0)
        for c in range(nk - unroll, nk - 1):
            step(c + 1, *bufs[(c + 1) % 2], c, *bufs[c % 2])
    step(None, None, None, nk - 1, *bufs[(nk - 1) % 2])
    for g in range(GQA_GROUP):
        o = acc_ref[g] / l_ref[g]
        o_ref[:, g * HEAD_DIM:(g + 1) * HEAD_DIM] = o.T.astype(o_ref.dtype)


def _attention(qt, k, vt, tq):
    B, n_q, _, L = qt.shape
    _, _, nk, v_rows, tk = vt.shape
    assert L % tq == 0 and nk * tk == L and n_q == N_KV_HEADS * GQA_GROUP
    gw = GQA_GROUP * HEAD_DIM
    unroll = ATTN_CHUNKS_PER_TRIP if nk % ATTN_CHUNKS_PER_TRIP == 0 and nk >= 4 * ATTN_CHUNKS_PER_TRIP else 2
    kern = functools.partial(_attn_kernel, tk=tk, nk=nk, unroll=unroll)
    return pl.pallas_call(
        kern,
        grid=(B, N_KV_HEADS, L // tq),
        in_specs=[
            pl.BlockSpec((None, GQA_GROUP, HEAD_DIM, tq), lambda b, j, i: (b, j, 0, i)),
            pl.BlockSpec((None, L, HEAD_DIM), lambda b, j, i: (b, 0, j)),
            pl.BlockSpec((None, None, nk, v_rows, tk), lambda b, j, i: (b, j, 0, 0, 0)),
        ],
        out_specs=pl.BlockSpec((None, tq, gw), lambda b, j, i: (b, i, j)),
        out_shape=jax.ShapeDtypeStruct((B, L, n_q * HEAD_DIM), BF16),
        scratch_shapes=[
            pltpu.VMEM((GQA_GROUP, 1, tq), F32),
            pltpu.VMEM((GQA_GROUP, 1, tq), F32),
            pltpu.VMEM((GQA_GROUP, v_rows, tq), F32),
            pltpu.VMEM((GQA_GROUP, tk, tq), F32),
            pltpu.VMEM((GQA_GROUP, tk, tq), F32),
            pltpu.VMEM((GQA_GROUP, 1, tq), F32),
            pltpu.VMEM((GQA_GROUP, 1, tq), F32),
        ],
        compiler_params=_params("parallel", "parallel", "arbitrary"),
        name="attention",
    )(qt, k, vt)


def _dft_cos_sin(n):
    idx = np.arange(n)
    ang = 2.0 * np.pi * ((idx[:, None] * idx[None, :]) % n) / n
    return np.cos(ang), np.sin(ang)


def _fourier_a_kernel(x_ref, f1_ref, twc_ref, tws_ref, zr_ref, zi_ref, *, l2, k):
    xt = pltpu.einshape("mkw->kmw", x_ref[...])
    zr, zi = [], []
    for t in range(k):
        z = jnp.dot(f1_ref[...], xt[t].astype(BF16), preferred_element_type=F32)
        a = z[:l2]
        b = z[l2:]
        c = twc_ref[:, t:t + 1]
        s = tws_ref[:, t:t + 1]
        zr.append(a * c + b * s)
        zi.append(b * c - a * s)
    zr_ref[...] = pltpu.einshape("kmw->mkw", jnp.stack(zr))
    zi_ref[...] = pltpu.einshape("kmw->mkw", jnp.stack(zi))


def _fourier_b_kernel(zr_ref, zi_ref, f2_ref, fc_ref, fw_ref, o_ref, hr_ref, hi_ref, *, l1, n_s2, norm):
    for a in range(n_s2):
        rows = slice(a * l1, (a + 1) * l1)
        zz = jnp.concatenate([zr_ref[rows, :].astype(BF16), zi_ref[rows, :].astype(BF16)], axis=0)
        hh = jnp.dot(f2_ref[...], zz, preferred_element_type=F32)
        hr_ref[rows, :] = hh[:l1].astype(BF16)
        hi_ref[rows, :] = hh[l1:].astype(BF16)
    ys = []
    for g in range(N_GROUPS):
        sl = slice(g * GROUP_CH, (g + 1) * GROUP_CH)
        hg = jnp.concatenate([hr_ref[:, sl], hi_ref[:, sl]], axis=1)
        f = jnp.dot(hg, fc_ref[...], preferred_element_type=F32) * norm
        ys.append(jnp.dot(f.astype(BF16), fw_ref[g], preferred_element_type=F32))
    y = jnp.concatenate(ys, axis=1).reshape(n_s2, l1, o_ref.shape[-1])
    o_ref[...] = pltpu.einshape("alw->law", y)


def _fourier(u, fw, l1, l2):
    B, L, width = u.shape
    k = min(FOURIER_S2_PER_STEP, l1)
    n_s2 = min(FOURIER_S2_PER_STEP, l2)
    assert l1 * l2 == L and l1 % k == 0 and l2 % n_s2 == 0 and width == N_GROUPS * GROUP_CH
    c2, s2 = _dft_cos_sin(l2)
    f1 = jnp.asarray(np.concatenate([c2, -s2], axis=0), BF16)
    c1, s1 = _dft_cos_sin(l1)
    f2 = jnp.asarray(np.block([[c1, s1], [-s1, c1]]), BF16)
    cc, sc = _dft_cos_sin(GROUP_CH)
    fc = jnp.asarray(np.concatenate([cc, sc], axis=0), BF16)
    ang = 2.0 * np.pi * ((np.arange(l2)[:, None] * np.arange(l1)[None, :]) % L) / L
    tw = lambda t: jnp.asarray(t.reshape(l2, l1 // k, k).transpose(1, 0, 2), F32)
    twc, tws = tw(np.cos(ang)), tw(np.sin(ang))

    x = u.reshape(B, l2, l1, width)
    col = lambda b, j: (b, 0, j, 0)
    const2 = lambda b, j: (0, 0)
    zr, zi = pl.pallas_call(
        functools.partial(_fourier_a_kernel, l2=l2, k=k),
        grid=(B, l1 // k),
        in_specs=[
            pl.BlockSpec((None, l2, k, width), col),
            pl.BlockSpec(f1.shape, const2),
            pl.BlockSpec((None, l2, k), lambda b, j: (j, 0, 0)),
            pl.BlockSpec((None, l2, k), lambda b, j: (j, 0, 0)),
        ],
        out_specs=[pl.BlockSpec((None, l2, k, width), col)] * 2,
        out_shape=[jax.ShapeDtypeStruct((B, l2, l1, width), F32)] * 2,
        compiler_params=_params("parallel", "parallel"),
        name="fourier_a",
    )(x, f1, twc, tws)

    zr = zr.reshape(B, L, width)
    zi = zi.reshape(B, L, width)
    rows = lambda b, j: (b, j, 0)
    out = pl.pallas_call(
        functools.partial(_fourier_b_kernel, l1=l1, n_s2=n_s2, norm=1.0 / math.sqrt(L * GROUP_CH)),
        grid=(B, l2 // n_s2),
        in_specs=[
            pl.BlockSpec((None, n_s2 * l1, width), rows),
            pl.BlockSpec((None, n_s2 * l1, width), rows),
            pl.BlockSpec(f2.shape, const2),
            pl.BlockSpec(fc.shape, const2),
            pl.BlockSpec(fw.shape, lambda b, j: (0, 0, 0)),
        ],
        out_specs=pl.BlockSpec((None, l1, n_s2, width), lambda b, j: (b, 0, j, 0)),
        out_shape=jax.ShapeDtypeStruct((B, l1, l2, width), F32),
        scratch_shapes=[pltpu.VMEM((n_s2 * l1, width), BF16)] * 2,
        compiler_params=_params("parallel", "parallel"),
        name="fourier_b",
    )(zr, zi, f2, fc, fw)
    return out.reshape(B, L, width)


def _mix_out_kernel(up_ref, uc_ref, un_ref, attn_ref, four_ref, x_ref, wo_ref, pw_ref, ps_ref,
                    g_ref, o_ref, ext_ref, v_ref, m_ref, pooled_ref, *, tm, seq_len, widths):
    pool_w, attn_w, four_w = widths
    i = pl.program_id(1)
    last = pl.num_programs(1) - 1
    ext_ref[0:POOL_HALO, :] = jnp.where(i > 0, up_ref[...], 0.0)
    ext_ref[POOL_HALO:POOL_HALO + tm, :] = uc_ref[...]
    ext_ref[POOL_HALO + tm:, :] = jnp.where(i < last, un_ref[...], 0.0)

    halves = [slice(r * (tm // 2), (r + 1) * (tm // 2)) for r in range(2)]
    t = i * tm + lax.broadcasted_iota(jnp.int32, (tm, 1), 0)
    for g in range(len(POOL_WINDOWS)):
        sl = slice(g * GROUP_CH, (g + 1) * GROUP_CH)
        ug = ext_ref[:, sl]
        hi = ug.astype(BF16)
        lo = (ug - hi.astype(F32)).astype(BF16)
        pw2 = jnp.concatenate([pw_ref[g], pw_ref[g]], axis=0)
        v_ref[:, sl] = jnp.dot(jnp.concatenate([hi, lo], axis=1), pw2, preferred_element_type=F32)

    for rows in halves:
        m_ref[rows, :] = (
            jnp.dot(attn_ref[rows, :], wo_ref[pool_w:pool_w + attn_w, :], preferred_element_type=F32)
            + jnp.dot(four_ref[rows, :].astype(BF16), wo_ref[pool_w + attn_w:, :],
                      preferred_element_type=F32))

    for g, w in enumerate(POOL_WINDOWS):
        sl = slice(g * GROUP_CH, (g + 1) * GROUP_CH)
        lo = jnp.maximum(t - w // 2, 0)
        hi = jnp.minimum(t + (w - 1 - w // 2), seq_len - 1)
        cnt = (hi - lo + 1).astype(F32)
        win = v_ref[POOL_HALO - w // 2:POOL_HALO - w // 2 + tm, sl]
        for d in range(1 - w // 2, w - w // 2):
            win = win + v_ref[POOL_HALO + d:POOL_HALO + d + tm, sl]
        yg = (win / cnt - v_ref[POOL_HALO:POOL_HALO + tm, sl]) * ps_ref[:, sl]
        pooled_ref[:, sl] = yg.astype(BF16)
    for rows in halves:
        m = m_ref[rows, :] + jnp.dot(pooled_ref[rows, :], wo_ref[0:pool_w, :],
                                     preferred_element_type=F32)
        o_ref[rows, :] = x_ref[rows, :] + _rms(m, g_ref[...])


def _mix_out(x, u_pool, attn, four, wo, pw, ps, g, tm):
    B, L, D = x.shape
    widths = (u_pool.shape[-1], attn.shape[-1], four.shape[-1])
    assert L % tm == 0 and tm % (2 * POOL_HALO) == 0 and wo.shape == (sum(widths), D)
    assert widths[0] == len(POOL_WINDOWS) * GROUP_CH
    hb = tm // POOL_HALO
    n_hb = L // POOL_HALO
    row = lambda b, i: (b, i, 0)
    const2 = lambda b, i: (0, 0)
    kern = functools.partial(_mix_out_kernel, tm=tm, seq_len=L, widths=widths)
    return pl.pallas_call(
        kern,
        grid=(B, L // tm),
        in_specs=[
            pl.BlockSpec((None, POOL_HALO, widths[0]), lambda b, i: (b, jnp.maximum(i * hb - 1, 0), 0)),
            pl.BlockSpec((None, tm, widths[0]), row),
            pl.BlockSpec((None, POOL_HALO, widths[0]),
                         lambda b, i: (b, jnp.minimum((i + 1) * hb, n_hb - 1), 0)),
            pl.BlockSpec((None, tm, widths[1]), row),
            pl.BlockSpec((None, tm, widths[2]), row),
            pl.BlockSpec((None, tm, D), row),
            pl.BlockSpec(wo.shape, const2),
            pl.BlockSpec(pw.shape, lambda b, i: (0, 0, 0)),
            pl.BlockSpec((1, widths[0]), const2),
            pl.BlockSpec((1, D), const2),
        ],
        out_specs=pl.BlockSpec((None, tm, D), row),
        out_shape=jax.ShapeDtypeStruct((B, L, D), F32),
        scratch_shapes=[
            pltpu.VMEM((tm + 2 * POOL_HALO, widths[0]), F32),
            pltpu.VMEM((tm + 2 * POOL_HALO, widths[0]), F32),
            pltpu.VMEM((tm, D), F32),
            pltpu.VMEM((tm, widths[0]), BF16),
        ],
        compiler_params=_params("parallel", "parallel"),
        name="mix_out",
    )(u_pool, u_pool, u_pool, attn, four, x, wo, pw, ps, g)


def _gated_gelu(gate, half_val):
    c0 = math.sqrt(2.0 / math.pi)
    inner = gate * (gate * gate * (c0 * 0.044715) + c0)
    return gate * (1.0 + jnp.tanh(inner)) * half_val


def _ffn_kernel(xp_ref, x_ref, xn_ref, gpre_ref, wg_ref, wv_ref, cwg_ref, cwv_ref, cbg_ref, cbv_ref,
                wd_ref, gpost_ref, o_ref, h_ref, ug_ref, uv_ref, *, tm, sub):
    i = pl.program_id(1)
    j = pl.program_id(2)
    rows = tm + 2 * CONV_HALO

    @pl.when(j == 0)
    def _():
        g = gpre_ref[...]
        hp = jnp.where(i > 0, _rms(xp_ref[...], g), 0.0)
        hn = jnp.where(i < pl.num_programs(1) - 1, _rms(xn_ref[...], g), 0.0)
        h_ref[0:CONV_HALO, :] = hp.astype(BF16)
        h_ref[CONV_HALO:CONV_HALO + tm, :] = _rms(x_ref[...], g).astype(BF16)
        h_ref[CONV_HALO + tm:, :] = hn.astype(BF16)
        o_ref[...] = jnp.zeros(o_ref.shape, F32)

    def conv(u_ref, cw, cb, r0, nr):
        prev = u_ref[CONV_HALO - 1 + r0:CONV_HALO - 1 + r0 + nr, :]
        cur = u_ref[CONV_HALO + r0:CONV_HALO + r0 + nr, :]
        nxt = u_ref[CONV_HALO + 1 + r0:CONV_HALO + 1 + r0 + nr, :]
        return prev * cw[0:1, :] + cur * cw[1:2, :] + nxt * cw[2:3, :] + cb

    def step(finish):
        h = h_ref[...]
        n_sub = wg_ref.shape[1] // sub
        nr = tm // FFN_ROW_BLOCKS
        for s in range(n_sub):
            sl = slice(s * sub, (s + 1) * sub)
            ug_ref[s] = jnp.dot(h, wg_ref[:, sl], preferred_element_type=F32)
            uv_ref[s] = jnp.dot(h, wv_ref[:, sl], preferred_element_type=F32)
        for s in range(n_sub):
            sl = slice(s * sub, (s + 1) * sub)
            cwg, cbg = cwg_ref[:, sl], cbg_ref[:, sl]
            cwv, cbv = 0.5 * cwv_ref[:, sl], 0.5 * cbv_ref[:, sl]
            for r0 in range(0, tm, nr):
                rows = slice(r0, r0 + nr)
                gate = conv(ug_ref.at[s], cwg, cbg, r0, nr)
                half_val = conv(uv_ref.at[s], cwv, cbv, r0, nr)
                act = _gated_gelu(gate, half_val).astype(BF16)
                d = jnp.dot(act, wd_ref[sl, :], preferred_element_type=F32)
                if finish and s == n_sub - 1:
                    o_ref[rows, :] = x_ref[rows, :] + _rms(o_ref[rows, :] + d, gpost_ref[...])
                else:
                    o_ref[rows, :] += d

    last = pl.num_programs(2) - 1
    pl.when(j < last)(functools.partial(step, False))
    pl.when(j == last)(functools.partial(step, True))


def _ffn(x, gpre, w_up, conv_w, conv_b, w_down, gpost, layer, tm, chunk):
    B, L, D = x.shape
    d_ff = w_down.shape[1]
    assert L % tm == 0 and tm % (FFN_ROW_BLOCKS * CONV_HALO) == 0
    assert d_ff % chunk == 0 and chunk % FFN_SUB_CHUNK == 0 and w_up.shape[1:] == (D, 2 * d_ff)
    nc = d_ff // chunk
    hb = tm // CONV_HALO
    n_hb = L // CONV_HALO
    const2 = lambda b, i, j: (0, 0)
    gate_col = lambda b, i, j: (0, j)
    val_col = lambda b, i, j: (0, nc + j)
    sub = min(FFN_SUB_CHUNK, chunk)
    kern = functools.partial(_ffn_kernel, tm=tm, sub=sub)
    return pl.pallas_call(
        kern,
        grid=(B, L // tm, nc),
        in_specs=[
            pl.BlockSpec((None, CONV_HALO, D), lambda b, i, j: (b, jnp.maximum(i * hb - 1, 0), 0)),
            pl.BlockSpec((None, tm, D), lambda b, i, j: (b, i, 0), pipeline_mode=pl.Buffered(1)),
            pl.BlockSpec((None, CONV_HALO, D),
                         lambda b, i, j: (b, jnp.minimum((i + 1) * hb, n_hb - 1), 0)),
            pl.BlockSpec((1, D), const2),
            pl.BlockSpec((None, D, chunk), lambda b, i, j: (layer, 0, j)),
            pl.BlockSpec((None, D, chunk), lambda b, i, j: (layer, 0, nc + j)),
            pl.BlockSpec((3, chunk), gate_col),
            pl.BlockSpec((3, chunk), val_col),
            pl.BlockSpec((1, chunk), gate_col),
            pl.BlockSpec((1, chunk), val_col),
            pl.BlockSpec((None, chunk, D), lambda b, i, j: (layer, j, 0)),
            pl.BlockSpec((1, D), const2),
        ],
        out_specs=pl.BlockSpec((None, tm, D), lambda b, i, j: (b, i, 0)),
        out_shape=jax.ShapeDtypeStruct((B, L, D), F32),
        scratch_shapes=[
            pltpu.VMEM((tm + 2 * CONV_HALO, D), BF16),
            pltpu.VMEM((chunk // sub, tm + 2 * CONV_HALO, sub), F32),
            pltpu.VMEM((chunk // sub, tm + 2 * CONV_HALO, sub), F32),
        ],
        compiler_params=_params("parallel", "parallel", "arbitrary"),
        name="ffn",
    )(x, x, x, gpre, w_up, w_up, conv_w, conv_w, conv_b, conv_b, w_down, gpost)


def _rope_tables(seq_len):
    quarter = HEAD_DIM // 4
    t = jnp.arange(seq_len, dtype=jnp.int32)
    row = (t // GRID_W).astype(F32)
    col = (t % GRID_W).astype(F32)
    inv_freq = 1.0 / (ROPE_THETA ** (jnp.arange(quarter, dtype=F32) / quarter))
    ang = jnp.concatenate([row[:, None] * inv_freq[None, :], col[:, None] * inv_freq[None, :]], axis=-1)
    cos, sin = jnp.cos(ang), jnp.sin(ang)
    return jnp.concatenate([cos, cos], axis=-1), jnp.concatenate([-sin, sin], axis=-1)


def _fourier_split(seq_len):
    l2 = 1 << (int(math.log2(seq_len)) // 2)
    return seq_len // l2, l2


def _trunk(x, layers):
    B, L, D = x.shape
    ts = _tiles(L)
    cc, ss = _rope_tables(L)
    l1, l2 = _fourier_split(L)
    for p in layers:
        u_pool, qt, k, vt, u_four = _in_proj(x, p["g_pre_mix"], p["w_in"], cc, ss, p["q_norm"],
                                             p["k_norm"], p["widths"], ts["tm_proj"])
        attn = _attention(qt, k, vt, ts["tq"])
        four = _fourier(u_four, p["fourier_w"], l1, l2)
        x = _mix_out(x, u_pool, attn, four, p["w_out"], p["pool_w"], p["pool_scale"],
                     p["g_post_mix"], ts["tm_mix"])
        x = _ffn(x, p["g_pre_ffn"], p["w_up"], p["conv_w"], p["conv_b"], p["w_down"],
                 p["g_post_ffn"], p["layer"], ts["tm_ffn"], ts["ff_chunk"])
    return x


def kernel(x_prompt, x_sample, g_pre_mix, g_post_mix, w_in, pool_w, pool_scale, q_norm, k_norm,
           fourier_w, w_out, g_pre_ffn, g_post_ffn, w_up, conv_w, conv_b, w_down):
    depth = w_in.shape[0]
    pool_width = pool_scale.shape[-1]
    four_width = fourier_w.shape[1] * fourier_w.shape[2]
    kv_width = N_KV_HEADS * HEAD_DIM
    q_width = w_in.shape[-1] - pool_width - four_width - 2 * kv_width
    w_up_b, w_down_b = w_up.astype(BF16), w_down.astype(BF16)
    layers = []
    for l in range(depth):
        layers.append(dict(
            layer=l, widths=(pool_width, q_width, kv_width, four_width),
            g_pre_mix=g_pre_mix[l][None, :], g_post_mix=g_post_mix[l][None, :],
            w_in=w_in[l].astype(BF16), pool_w=pool_w[l].astype(BF16),
            pool_scale=pool_scale[l][None, :], q_norm=q_norm[l][None, :], k_norm=k_norm[l][None, :],
            fourier_w=fourier_w[l].astype(BF16), w_out=w_out[l].astype(BF16),
            g_pre_ffn=g_pre_ffn[l][None, :], g_post_ffn=g_post_ffn[l][None, :],
            w_up=w_up_b, conv_w=conv_w[l], conv_b=conv_b[l][None, :], w_down=w_down_b))
    return _trunk(x_prompt, layers), _trunk(x_sample, layers)
```

```python
import functools
import math

import jax
import jax.numpy as jnp
import numpy as np
from jax import lax
from jax.experimental import pallas as pl
from jax.experimental.pallas import tpu as pltpu

F32 = jnp.float32
BF16 = jnp.bfloat16

NORM_EPS = 1e-6
GRID_W = 64
HEAD_DIM = 128
N_KV_HEADS = 2
GQA_GROUP = 4
POOL_WINDOWS = (2, 4, 8, 16)
POOL_HALO = 8
GROUP_CH = 128
N_GROUPS = 4
ROPE_THETA = 10000.0
CONV_HALO = 16
ATTN_CHUNKS_PER_TRIP = 4
FOURIER_S2_PER_STEP = 8
FFN_ROW_BLOCKS = 2
FFN_SUB_CHUNK = 256

V7X_VMEM_BYTES = 64 * 1024 * 1024
VMEM_LIMIT = V7X_VMEM_BYTES - 4 * 1024 * 1024


def _tiles(seq_len):
    return dict(
        tm_proj=min(512, seq_len),
        tm_mix=min(512, seq_len),
        tm_ffn=min(1024, seq_len),
        ff_chunk=512,
        tq=min(512, seq_len),
    )


def _params(*sem):
    return pltpu.CompilerParams(dimension_semantics=sem, vmem_limit_bytes=VMEM_LIMIT)


def _rms(x, g):
    return x * lax.rsqrt(jnp.mean(x * x, axis=-1, keepdims=True) + NORM_EPS) * g


def _in_proj_kernel(x_ref, g_ref, w_ref, cc_ref, ss_ref, qn_ref, kn_ref,
                    pool_ref, q_ref, k_ref, v_ref, four_ref, *, widths, scale):
    pool_w, q_w, kv_w, four_w = widths
    h = _rms(x_ref[...], g_ref[...]).astype(BF16)
    cc = cc_ref[...]
    ss = ss_ref[...]

    def seg(lo, width):
        return jnp.dot(h, w_ref[:, lo:lo + width], preferred_element_type=F32)

    def norm_rope(zh, gain, out_scale):
        y = zh * lax.rsqrt(jnp.mean(zh * zh, axis=-1, keepdims=True) + NORM_EPS) * gain
        y = y * cc + pltpu.roll(y, HEAD_DIM // 2, axis=1) * ss
        return y * out_scale

    pair = 2 * HEAD_DIM
    for p in range(q_w // pair):
        z = seg(pool_w + p * pair, pair)
        for s in range(2):
            zh = z[:, s * HEAD_DIM:(s + 1) * HEAD_DIM]
            q_ref[2 * p + s] = norm_rope(zh, qn_ref[...], scale).T.astype(BF16)
    z = seg(pool_w + q_w, kv_w)
    for s in range(kv_w // HEAD_DIM):
        zh = z[:, s * HEAD_DIM:(s + 1) * HEAD_DIM]
        k_ref[:, s * HEAD_DIM:(s + 1) * HEAD_DIM] = norm_rope(zh, kn_ref[...], 1.0).astype(BF16)
    z = seg(pool_w + q_w + kv_w, kv_w)
    for s in range(kv_w // HEAD_DIM):
        v_ref[s] = z[:, s * HEAD_DIM:(s + 1) * HEAD_DIM].T.astype(BF16)
    four_ref[...] = seg(pool_w + q_w + 2 * kv_w, four_w)
    pool_ref[...] = seg(0, pool_w)


def _in_proj(x, g, w, cc, ss, qn, kn, widths, tm):
    B, L, D = x.shape
    pool_w, q_w, kv_w, four_w = widths
    assert L % tm == 0 and w.shape == (D, sum(widths) + kv_w)
    nt = L // tm
    row = lambda b, i: (b, i, 0)
    const2 = lambda b, i: (0, 0)
    n_q = q_w // HEAD_DIM
    v_rows = HEAD_DIM
    kern = functools.partial(_in_proj_kernel, widths=widths, scale=HEAD_DIM ** -0.5 * math.log2(math.e))
    return pl.pallas_call(
        kern,
        grid=(B, nt),
        in_specs=[
            pl.BlockSpec((None, tm, D), row),
            pl.BlockSpec((1, D), const2),
            pl.BlockSpec(w.shape, const2),
            pl.BlockSpec((tm, HEAD_DIM), lambda b, i: (i, 0)),
            pl.BlockSpec((tm, HEAD_DIM), lambda b, i: (i, 0)),
            pl.BlockSpec((1, HEAD_DIM), const2),
            pl.BlockSpec((1, HEAD_DIM), const2),
        ],
        out_specs=[
            pl.BlockSpec((None, tm, pool_w), row),
            pl.BlockSpec((None, n_q, HEAD_DIM, tm), lambda b, i: (b, 0, 0, i)),
            pl.BlockSpec((None, tm, kv_w), row),
            pl.BlockSpec((None, N_KV_HEADS, None, v_rows, tm), lambda b, i: (b, 0, i, 0, 0)),
            pl.BlockSpec((None, tm, four_w), row),
        ],
        out_shape=[
            jax.ShapeDtypeStruct((B, L, pool_w), F32),
            jax.ShapeDtypeStruct((B, n_q, HEAD_DIM, L), BF16),
            jax.ShapeDtypeStruct((B, L, kv_w), BF16),
            jax.ShapeDtypeStruct((B, N_KV_HEADS, nt, v_rows, tm), BF16),
            jax.ShapeDtypeStruct((B, L, four_w), F32),
        ],
        compiler_params=_params("parallel", "parallel"),
        name="in_proj",
    )(x, g, w, cc, ss, qn, kn)


def _attn_kernel(qt_ref, k_ref, vt_ref, o_ref, m_ref, l_ref, acc_ref, sa_ref, sb_ref, ma_ref, mb_ref,
                 *, tk, nk, unroll):
    assert nk == 1 or (unroll % 2 == 0 and nk % unroll == 0)
    m_ref[...] = jnp.full(m_ref.shape, -jnp.inf, F32)
    l_ref[...] = jnp.zeros(l_ref.shape, F32)
    acc_ref[...] = jnp.zeros(acc_ref.shape, F32)

    def scores(c, g, s_ref, cm_ref):
        kc = k_ref[pl.ds(pl.multiple_of(c * tk, tk), tk), :]
        st = jnp.dot(kc, qt_ref[g], preferred_element_type=F32)
        s_ref[g] = st
        cm_ref[g] = jnp.max(st, axis=0, keepdims=True)

    def accumulate(c, g, s_ref, cm_ref):
        m_old = m_ref[g]
        m_new = jnp.maximum(m_old, cm_ref[g])
        alpha = jnp.exp2(m_old - m_new)
        p = jnp.exp2(s_ref[g] - m_new)
        l_ref[g] = alpha * l_ref[g] + jnp.sum(p, axis=0, keepdims=True)
        acc_ref[g] = alpha * acc_ref[g] + jnp.dot(vt_ref[c], p.astype(BF16),
                                                  preferred_element_type=F32)
        m_ref[g] = m_new

    def step(c_scores, s_next, m_next, c_acc, s_cur, m_cur):
        for g in range(GQA_GROUP):
            if c_scores is not None:
                scores(c_scores, g, s_next, m_next)
            if c_acc is not None:
                accumulate(c_acc, g, s_cur, m_cur)

    bufs = ((sa_ref, ma_ref), (sb_ref, mb_ref))
    step(0, *bufs[0], None, None, None)
    if nk > 1:
        def group(i, carry):
            for u in range(unroll):
                c = unroll * i + u
                step(c + 1, *bufs[(u + 1) % 2], c, *bufs[u % 2])
            return carry

        lax.fori_loop(0, nk // unroll - 1, group, 0)
        for c in range(nk - unroll, nk - 1):
            step(c + 1, *bufs[(c + 1) % 2], c, *bufs[c % 2])
    step(None, None, None, nk - 1, *bufs[(nk - 1) % 2])
    for g in range(GQA_GROUP):
        o = acc_ref[g] / l_ref[g]
        o_ref[:, g * HEAD_DIM:(g + 1) * HEAD_DIM] = o.T.astype(o_ref.dtype)


def _attention(qt, k, vt, tq):
    B, n_q, _, L = qt.shape
    _, _, nk, v_rows, tk = vt.shape
    assert L % tq == 0 and nk * tk == L and n_q == N_KV_HEADS * GQA_GROUP
    gw = GQA_GROUP * HEAD_DIM
    unroll = ATTN_CHUNKS_PER_TRIP if nk % ATTN_CHUNKS_PER_TRIP == 0 and nk >= 4 * ATTN_CHUNKS_PER_TRIP else 2
    kern = functools.partial(_attn_kernel, tk=tk, nk=nk, unroll=unroll)
    return pl.pallas_call(
        kern,
        grid=(B, N_KV_HEADS, L // tq),
        in_specs=[
            pl.BlockSpec((None, GQA_GROUP, HEAD_DIM, tq), lambda b, j, i: (b, j, 0, i)),
            pl.BlockSpec((None, L, HEAD_DIM), lambda b, j, i: (b, 0, j)),
            pl.BlockSpec((None, None, nk, v_rows, tk), lambda b, j, i: (b, j, 0, 0, 0)),
        ],
        out_specs=pl.BlockSpec((None, tq, gw), lambda b, j, i: (b, i, j)),
        out_shape=jax.ShapeDtypeStruct((B, L, n_q * HEAD_DIM), BF16),
        scratch_shapes=[
            pltpu.VMEM((GQA_GROUP, 1, tq), F32),
            pltpu.VMEM((GQA_GROUP, 1, tq), F32),
            pltpu.VMEM((GQA_GROUP, v_rows, tq), F32),
            pltpu.VMEM((GQA_GROUP, tk, tq), F32),
            pltpu.VMEM((GQA_GROUP, tk, tq), F32),
            pltpu.VMEM((GQA_GROUP, 1, tq), F32),
            pltpu.VMEM((GQA_GROUP, 1, tq), F32),
        ],
        compiler_params=_params("parallel", "parallel", "arbitrary"),
        name="attention",
    )(qt, k, vt)


def _dft_cos_sin(n):
    idx = np.arange(n)
    ang = 2.0 * np.pi * ((idx[:, None] * idx[None, :]) % n) / n
    return np.cos(ang), np.sin(ang)


def _fourier_a_kernel(x_ref, f1_ref, twc_ref, tws_ref, zr_ref, zi_ref, *, l2, k):
    xt = pltpu.einshape("mkw->kmw", x_ref[...])
    zr, zi = [], []
    for t in range(k):
        z = jnp.dot(f1_ref[...], xt[t].astype(BF16), preferred_element_type=F32)
        a = z[:l2]
        b = z[l2:]
        c = twc_ref[:, t:t + 1]
        s = tws_ref[:, t:t + 1]
        zr.append(a * c + b * s)
        zi.append(b * c - a * s)
    zr_ref[...] = pltpu.einshape("kmw->mkw", jnp.stack(zr))
    zi_ref[...] = pltpu.einshape("kmw->mkw", jnp.stack(zi))


def _fourier_b_kernel(zr_ref, zi_ref, f2_ref, fc_ref, fw_ref, o_ref, hr_ref, hi_ref, *, l1, n_s2, norm):
    for a in range(n_s2):
        rows = slice(a * l1, (a + 1) * l1)
        zz = jnp.concatenate([zr_ref[rows, :].astype(BF16), zi_ref[rows, :].astype(BF16)], axis=0)
        hh = jnp.dot(f2_ref[...], zz, preferred_element_type=F32)
        hr_ref[rows, :] = hh[:l1].astype(BF16)
        hi_ref[rows, :] = hh[l1:].astype(BF16)
    ys = []
    for g in range(N_GROUPS):
        sl = slice(g * GROUP_CH, (g + 1) * GROUP_CH)
        hg = jnp.concatenate([hr_ref[:, sl], hi_ref[:, sl]], axis=1)
        f = jnp.dot(hg, fc_ref[...], preferred_element_type=F32) * norm
        ys.append(jnp.dot(f.astype(BF16), fw_ref[g], preferred_element_type=F32))
    y = jnp.concatenate(ys, axis=1).reshape(n_s2, l1, o_ref.shape[-1])
    o_ref[...] = pltpu.einshape("alw->law", y)


def _fourier(u, fw, l1, l2):
    B, L, width = u.shape
    k = min(FOURIER_S2_PER_STEP, l1)
    n_s2 = min(FOURIER_S2_PER_STEP, l2)
    assert l1 * l2 == L and l1 % k == 0 and l2 % n_s2 == 0 and width == N_GROUPS * GROUP_CH
    c2, s2 = _dft_cos_sin(l2)
    f1 = jnp.asarray(np.concatenate([c2, -s2], axis=0), BF16)
    c1, s1 = _dft_cos_sin(l1)
    f2 = jnp.asarray(np.block([[c1, s1], [-s1, c1]]), BF16)
    cc, sc = _dft_cos_sin(GROUP_CH)
    fc = jnp.asarray(np.concatenate([cc, sc], axis=0), BF16)
    ang = 2.0 * np.pi * ((np.arange(l2)[:, None] * np.arange(l1)[None, :]) % L) / L
    tw = lambda t: jnp.asarray(t.reshape(l2, l1 // k, k).transpose(1, 0, 2), F32)
    twc, tws = tw(np.cos(ang)), tw(np.sin(ang))

    x = u.reshape(B, l2, l1, width)
    col = lambda b, j: (b, 0, j, 0)
    const2 = lambda b, j: (0, 0)
    zr, zi = pl.pallas_call(
        functools.partial(_fourier_a_kernel, l2=l2, k=k),
        grid=(B, l1 // k),
        in_specs=[
            pl.BlockSpec((None, l2, k, width), col),
            pl.BlockSpec(f1.shape, const2),
            pl.BlockSpec((None, l2, k), lambda b, j: (j, 0, 0)),
            pl.BlockSpec((None, l2, k), lambda b, j: (j, 0, 0)),
        ],
        out_specs=[pl.BlockSpec((None, l2, k, width), col)] * 2,
        out_shape=[jax.ShapeDtypeStruct((B, l2, l1, width), F32)] * 2,
        compiler_params=_params("parallel", "parallel"),
        name="fourier_a",
    )(x, f1, twc, tws)

    zr = zr.reshape(B, L, width)
    zi = zi.reshape(B, L, width)
    rows = lambda b, j: (b, j, 0)
    out = pl.pallas_call(
        functools.partial(_fourier_b_kernel, l1=l1, n_s2=n_s2, norm=1.0 / math.sqrt(L * GROUP_CH)),
        grid=(B, l2 // n_s2),
        in_specs=[
            pl.BlockSpec((None, n_s2 * l1, width), rows),
            pl.BlockSpec((None, n_s2 * l1, width), rows),
            pl.BlockSpec(f2.shape, const2),
            pl.BlockSpec(fc.shape, const2),
            pl.BlockSpec(fw.shape, lambda b, j: (0, 0, 0)),
        ],
        out_specs=pl.BlockSpec((None, l1, n_s2, width), lambda b, j: (b, 0, j, 0)),
        out_shape=jax.ShapeDtypeStruct((B, l1, l2, width), F32),
        scratch_shapes=[pltpu.VMEM((n_s2 * l1, width), BF16)] * 2,
        compiler_params=_params("parallel", "parallel"),
        name="fourier_b",
    )(zr, zi, f2, fc, fw)
    return out.reshape(B, L, width)


def _mix_out_kernel(up_ref, uc_ref, un_ref, attn_ref, four_ref, x_ref, wo_ref, pw_ref, ps_ref,
                    g_ref, o_ref, ext_ref, v_ref, m_ref, pooled_ref, *, tm, seq_len, widths):
    pool_w, attn_w, four_w = widths
    i = pl.program_id(1)
    last = pl.num_programs(1) - 1
    ext_ref[0:POOL_HALO, :] = jnp.where(i > 0, up_ref[...], 0.0)
    ext_ref[POOL_HALO:POOL_HALO + tm, :] = uc_ref[...]
    ext_ref[POOL_HALO + tm:, :] = jnp.where(i < last, un_ref[...], 0.0)

    halves = [slice(r * (tm // 2), (r + 1) * (tm // 2)) for r in range(2)]
    t = i * tm + lax.broadcasted_iota(jnp.int32, (tm, 1), 0)
    for g in range(len(POOL_WINDOWS)):
        sl = slice(g * GROUP_CH, (g + 1) * GROUP_CH)
        ug = ext_ref[:, sl]
        hi = ug.astype(BF16)
        lo = (ug - hi.astype(F32)).astype(BF16)
        pw2 = jnp.concatenate([pw_ref[g], pw_ref[g]], axis=0)
        v_ref[:, sl] = jnp.dot(jnp.concatenate([hi, lo], axis=1), pw2, preferred_element_type=F32)

    for rows in halves:
        m_ref[rows, :] = (
            jnp.dot(attn_ref[rows, :], wo_ref[pool_w:pool_w + attn_w, :], preferred_element_type=F32)
            + jnp.dot(four_ref[rows, :].astype(BF16), wo_ref[pool_w + attn_w:, :],
                      preferred_element_type=F32))

    for g, w in enumerate(POOL_WINDOWS):
        sl = slice(g * GROUP_CH, (g + 1) * GROUP_CH)
        lo = jnp.maximum(t - w // 2, 0)
        hi = jnp.minimum(t + (w - 1 - w // 2), seq_len - 1)
        cnt = (hi - lo + 1).astype(F32)
        win = v_ref[POOL_HALO - w // 2:POOL_HALO - w // 2 + tm, sl]
        for d in range(1 - w // 2, w - w // 2):
            win = win + v_ref[POOL_HALO + d:POOL_HALO + d + tm, sl]
        yg = (win / cnt - v_ref[POOL_HALO:POOL_HALO + tm, sl]) * ps_ref[:, sl]
        pooled_ref[:, sl] = yg.astype(BF16)
    for rows in halves:
        m = m_ref[rows, :] + jnp.dot(pooled_ref[rows, :], wo_ref[0:pool_w, :],
                                     preferred_element_type=F32)
        o_ref[rows, :] = x_ref[rows, :] + _rms(m, g_ref[...])


def _mix_out(x, u_pool, attn, four, wo, pw, ps, g, tm):
    B, L, D = x.shape
    widths = (u_pool.shape[-1], attn.shape[-1], four.shape[-1])
    assert L % tm == 0 and tm % (2 * POOL_HALO) == 0 and wo.shape == (sum(widths), D)
    assert widths[0] == len(POOL_WINDOWS) * GROUP_CH
    hb = tm // POOL_HALO
    n_hb = L // POOL_HALO
    row = lambda b, i: (b, i, 0)
    const2 = lambda b, i: (0, 0)
    kern = functools.partial(_mix_out_kernel, tm=tm, seq_len=L, widths=widths)
    return pl.pallas_call(
        kern,
        grid=(B, L // tm),
        in_specs=[
            pl.BlockSpec((None, POOL_HALO, widths[0]), lambda b, i: (b, jnp.maximum(i * hb - 1, 0), 0)),
            pl.BlockSpec((None, tm, widths[0]), row),
            pl.BlockSpec((None, POOL_HALO, widths[0]),
                         lambda b, i: (b, jnp.minimum((i + 1) * hb, n_hb - 1), 0)),
            pl.BlockSpec((None, tm, widths[1]), row),
            pl.BlockSpec((None, tm, widths[2]), row),
            pl.BlockSpec((None, tm, D), row),
            pl.BlockSpec(wo.shape, const2),
            pl.BlockSpec(pw.shape, lambda b, i: (0, 0, 0)),
            pl.BlockSpec((1, widths[0]), const2),
            pl.BlockSpec((1, D), const2),
        ],
        out_specs=pl.BlockSpec((None, tm, D), row),
        out_shape=jax.ShapeDtypeStruct((B, L, D), F32),
        scratch_shapes=[
            pltpu.VMEM((tm + 2 * POOL_HALO, widths[0]), F32),
            pltpu.VMEM((tm + 2 * POOL_HALO, widths[0]), F32),
            pltpu.VMEM((tm, D), F32),
            pltpu.VMEM((tm, widths[0]), BF16),
        ],
        compiler_params=_params("parallel", "parallel"),
        name="mix_out",
    )(u_pool, u_pool, u_pool, attn, four, x, wo, pw, ps, g)


def _gated_gelu(gate, half_val):
    c0 = math.sqrt(2.0 / math.pi)
    inner = gate * (gate * gate * (c0 * 0.044715) + c0)
    return gate * (1.0 + jnp.tanh(inner)) * half_val


def _ffn_kernel(xp_ref, x_ref, xn_ref, gpre_ref, wg_ref, wv_ref, cwg_ref, cwv_ref, cbg_ref, cbv_ref,
                wd_ref, gpost_ref, o_ref, h_ref, ug_ref, uv_ref, *, tm, sub):
    i = pl.program_id(1)
    j = pl.program_id(2)
    rows = tm + 2 * CONV_HALO

    @pl.when(j == 0)
    def _():
        g = gpre_ref[...]
        hp = jnp.where(i > 0, _rms(xp_ref[...], g), 0.0)
        hn = jnp.where(i < pl.num_programs(1) - 1, _rms(xn_ref[...], g), 0.0)
        h_ref[0:CONV_HALO, :] = hp.astype(BF16)
        h_ref[CONV_HALO:CONV_HALO + tm, :] = _rms(x_ref[...], g).astype(BF16)
        h_ref[CONV_HALO + tm:, :] = hn.astype(BF16)
        o_ref[...] = jnp.zeros(o_ref.shape, F32)

    def conv(u_ref, cw, cb, r0, nr):
        prev = u_ref[CONV_HALO - 1 + r0:CONV_HALO - 1 + r0 + nr, :]
        cur = u_ref[CONV_HALO + r0:CONV_HALO + r0 + nr, :]
        nxt = u_ref[CONV_HALO + 1 + r0:CONV_HALO + 1 + r0 + nr, :]
        return prev * cw[0:1, :] + cur * cw[1:2, :] + nxt * cw[2:3, :] + cb

    def step(finish):
        h = h_ref[...]
        n_sub = wg_ref.shape[1] // sub
        nr = tm // FFN_ROW_BLOCKS
        for s in range(n_sub):
            sl = slice(s * sub, (s + 1) * sub)
            ug_ref[s] = jnp.dot(h, wg_ref[:, sl], preferred_element_type=F32)
            uv_ref[s] = jnp.dot(h, wv_ref[:, sl], preferred_element_type=F32)
        for s in range(n_sub):
            sl = slice(s * sub, (s + 1) * sub)
            cwg, cbg = cwg_ref[:, sl], cbg_ref[:, sl]
            cwv, cbv = 0.5 * cwv_ref[:, sl], 0.5 * cbv_ref[:, sl]
            for r0 in range(0, tm, nr):
                rows = slice(r0, r0 + nr)
                gate = conv(ug_ref.at[s], cwg, cbg, r0, nr)
                half_val = conv(uv_ref.at[s], cwv, cbv, r0, nr)
                act = _gated_gelu(gate, half_val).astype(BF16)
                d = jnp.dot(act, wd_ref[sl, :], preferred_element_type=F32)
                if finish and s == n_sub - 1:
                    o_ref[rows, :] = x_ref[rows, :] + _rms(o_ref[rows, :] + d, gpost_ref[...])
                else:
                    o_ref[rows, :] += d

    last = pl.num_programs(2) - 1
    pl.when(j < last)(functools.partial(step, False))
    pl.when(j == last)(functools.partial(step, True))


def _ffn(x, gpre, w_up, conv_w, conv_b, w_down, gpost, layer, tm, chunk):
    B, L, D = x.shape
    d_ff = w_down.shape[1]
    assert L % tm == 0 and tm % (FFN_ROW_BLOCKS * CONV_HALO) == 0
    assert d_ff % chunk == 0 and chunk % FFN_SUB_CHUNK == 0 and w_up.shape[1:] == (D, 2 * d_ff)
    nc = d_ff // chunk
    hb = tm // CONV_HALO
    n_hb = L // CONV_HALO
    const2 = lambda b, i, j: (0, 0)
    gate_col = lambda b, i, j: (0, j)
    val_col = lambda b, i, j: (0, nc + j)
    sub = min(FFN_SUB_CHUNK, chunk)
    kern = functools.partial(_ffn_kernel, tm=tm, sub=sub)
    return pl.pallas_call(
        kern,
        grid=(B, L // tm, nc),
        in_specs=[
            pl.BlockSpec((None, CONV_HALO, D), lambda b, i, j: (b, jnp.maximum(i * hb - 1, 0), 0)),
            pl.BlockSpec((None, tm, D), lambda b, i, j: (b, i, 0)),
            pl.BlockSpec((None, CONV_HALO, D),
                         lambda b, i, j: (b, jnp.minimum((i + 1) * hb, n_hb - 1), 0)),
            pl.BlockSpec((1, D), const2),
            pl.BlockSpec((None, D, chunk), lambda b, i, j: (layer, 0, j)),
            pl.BlockSpec((None, D, chunk), lambda b, i, j: (layer, 0, nc + j)),
            pl.BlockSpec((3, chunk), gate_col),
            pl.BlockSpec((3, chunk), val_col),
            pl.BlockSpec((1, chunk), gate_col),
            pl.BlockSpec((1, chunk), val_col),
            pl.BlockSpec((None, chunk, D), lambda b, i, j: (layer, j, 0)),
            pl.BlockSpec((1, D), const2),
        ],
        out_specs=pl.BlockSpec((None, tm, D), lambda b, i, j: (b, i, 0)),
        out_shape=jax.ShapeDtypeStruct((B, L, D), F32),
        scratch_shapes=[
            pltpu.VMEM((tm + 2 * CONV_HALO, D), BF16),
            pltpu.VMEM((chunk // sub, tm + 2 * CONV_HALO, sub), F32),
            pltpu.VMEM((chunk // sub, tm + 2 * CONV_HALO, sub), F32),
        ],
        compiler_params=_params("parallel", "parallel", "arbitrary"),
        name="ffn",
    )(x, x, x, gpre, w_up, w_up, conv_w, conv_w, conv_b, conv_b, w_down, gpost)


def _rope_tables(seq_len):
    quarter = HEAD_DIM // 4
    t = jnp.arange(seq_len, dtype=jnp.int32)
    row = (t // GRID_W).astype(F32)
    col = (t % GRID_W).astype(F32)
    inv_freq = 1.0 / (ROPE_THETA ** (jnp.arange(quarter, dtype=F32) / quarter))
    ang = jnp.concatenate([row[:, None] * inv_freq[None, :], col[:, None] * inv_freq[None, :]], axis=-1)
    cos, sin = jnp.cos(ang), jnp.sin(ang)
    return jnp.concatenate([cos, cos], axis=-1), jnp.concatenate([-sin, sin], axis=-1)


def _fourier_split(seq_len):
    l2 = 1 << (int(math.log2(seq_len)) // 2)
    return seq_len // l2, l2


def _trunk(x, layers):
    B, L, D = x.shape
    ts = _tiles(L)
    cc, ss = _rope_tables(L)
    l1, l2 = _fourier_split(L)
    for p in layers:
        u_pool, qt, k, vt, u_four = _in_proj(x, p["g_pre_mix"], p["w_in"], cc, ss, p["q_norm"],
                                             p["k_norm"], p["widths"], ts["tm_proj"])
        attn = _attention(qt, k, vt, ts["tq"])
        four = _fourier(u_four, p["fourier_w"], l1, l2)
        x = _mix_out(x, u_pool, attn, four, p["w_out"], p["pool_w"], p["pool_scale"],
                     p["g_post_mix"], ts["tm_mix"])
        x = _ffn(x, p["g_pre_ffn"], p["w_up"], p["conv_w"], p["conv_b"], p["w_down"],
                 p["g_post_ffn"], p["layer"], ts["tm_ffn"], ts["ff_chunk"])
    return x


def kernel(x_prompt, x_sample, g_pre_mix, g_post_mix, w_in, pool_w, pool_scale, q_norm, k_norm,
           fourier_w, w_out, g_pre_ffn, g_post_ffn, w_up, conv_w, conv_b, w_down):
    depth = w_in.shape[0]
    pool_width = pool_scale.shape[-1]
    four_width = fourier_w.shape[1] * fourier_w.shape[2]
    kv_width = N_KV_HEADS * HEAD_DIM
    q_width = w_in.shape[-1] - pool_width - four_width - 2 * kv_width
    w_up_b, w_down_b = w_up.astype(BF16), w_down.astype(BF16)
    layers = []
    for l in range(depth):
        layers.append(dict(
            layer=l, widths=(pool_width, q_width, kv_width, four_width),
            g_pre_mix=g_pre_mix[l][None, :], g_post_mix=g_post_mix[l][None, :],
            w_in=w_in[l].astype(BF16), pool_w=pool_w[l].astype(BF16),
            pool_scale=pool_scale[l][None, :], q_norm=q_norm[l][None, :], k_norm=k_norm[l][None, :],
            fourier_w=fourier_w[l].astype(BF16), w_out=w_out[l].astype(BF16),
            g_pre_ffn=g_pre_ffn[l][None, :], g_post_ffn=g_post_ffn[l][None, :],
            w_up=w_up_b, conv_w=conv_w[l], conv_b=conv_b[l][None, :], w_down=w_down_b))
    return _trunk(x_prompt, layers), _trunk(x_sample, layers)
```

```python
import functools
import math

import jax
import jax.numpy as jnp
import numpy as np
from jax import lax
from jax.experimental import pallas as pl
from jax.experimental.pallas import tpu as pltpu

F32 = jnp.float32
BF16 = jnp.bfloat16

NORM_EPS = 1e-6
GRID_W = 64
HEAD_DIM = 128
N_KV_HEADS = 2
GQA_GROUP = 4
POOL_WINDOWS = (2, 4, 8, 16)
POOL_HALO = 8
GROUP_CH = 128
N_GROUPS = 4
ROPE_THETA = 10000.0
CONV_HALO = 16
ATTN_CHUNKS_PER_TRIP = 4
FOURIER_S2_PER_STEP = 16
FFN_ROW_BLOCKS = 2
FFN_SUB_CHUNK = 256

V7X_VMEM_BYTES = 64 * 1024 * 1024
VMEM_LIMIT = V7X_VMEM_BYTES - 4 * 1024 * 1024


def _tiles(seq_len):
    return dict(
        tm_proj=min(512, seq_len),
        tm_mix=min(512, seq_len),
        tm_ffn=min(1024, seq_len),
        ff_chunk=512,
        tq=min(512, seq_len),
    )


def _params(*sem):
    return pltpu.CompilerParams(dimension_semantics=sem, vmem_limit_bytes=VMEM_LIMIT)


def _rms(x, g):
    return x * lax.rsqrt(jnp.mean(x * x, axis=-1, keepdims=True) + NORM_EPS) * g


def _in_proj_kernel(x_ref, g_ref, w_ref, cc_ref, ss_ref, qn_ref, kn_ref,
                    pool_ref, q_ref, k_ref, v_ref, four_ref, *, widths, scale):
    pool_w, q_w, kv_w, four_w = widths
    h = _rms(x_ref[...], g_ref[...]).astype(BF16)
    cc = cc_ref[...]
    ss = ss_ref[...]

    def seg(lo, width):
        return jnp.dot(h, w_ref[:, lo:lo + width], preferred_element_type=F32)

    def norm_rope(zh, gain, out_scale):
        y = zh * lax.rsqrt(jnp.mean(zh * zh, axis=-1, keepdims=True) + NORM_EPS) * gain
        y = y * cc + pltpu.roll(y, HEAD_DIM // 2, axis=1) * ss
        return y * out_scale

    pair = 2 * HEAD_DIM
    for p in range(q_w // pair):
        z = seg(pool_w + p * pair, pair)
        for s in range(2):
            zh = z[:, s * HEAD_DIM:(s + 1) * HEAD_DIM]
            q_ref[2 * p + s] = norm_rope(zh, qn_ref[...], scale).T.astype(BF16)
    z = seg(pool_w + q_w, kv_w)
    for s in range(kv_w // HEAD_DIM):
        zh = z[:, s * HEAD_DIM:(s + 1) * HEAD_DIM]
        k_ref[:, s * HEAD_DIM:(s + 1) * HEAD_DIM] = norm_rope(zh, kn_ref[...], 1.0).astype(BF16)
    z = seg(pool_w + q_w + kv_w, kv_w)
    for s in range(kv_w // HEAD_DIM):
        v_ref[s] = z[:, s * HEAD_DIM:(s + 1) * HEAD_DIM].T.astype(BF16)
    four_ref[...] = seg(pool_w + q_w + 2 * kv_w, four_w)
    pool_ref[...] = seg(0, pool_w)


def _in_proj(x, g, w, cc, ss, qn, kn, widths, tm):
    B, L, D = x.shape
    pool_w, q_w, kv_w, four_w = widths
    assert L % tm == 0 and w.shape == (D, sum(widths) + kv_w)
    nt = L // tm
    row = lambda b, i: (b, i, 0)
    const2 = lambda b, i: (0, 0)
    n_q = q_w // HEAD_DIM
    v_rows = HEAD_DIM
    kern = functools.partial(_in_proj_kernel, widths=widths, scale=HEAD_DIM ** -0.5 * math.log2(math.e))
    return pl.pallas_call(
        kern,
        grid=(B, nt),
        in_specs=[
            pl.BlockSpec((None, tm, D), row),
            pl.BlockSpec((1, D), const2),
            pl.BlockSpec(w.shape, const2),
            pl.BlockSpec((tm, HEAD_DIM), lambda b, i: (i, 0)),
            pl.BlockSpec((tm, HEAD_DIM), lambda b, i: (i, 0)),
            pl.BlockSpec((1, HEAD_DIM), const2),
            pl.BlockSpec((1, HEAD_DIM), const2),
        ],
        out_specs=[
            pl.BlockSpec((None, tm, pool_w), row),
            pl.BlockSpec((None, n_q, HEAD_DIM, tm), lambda b, i: (b, 0, 0, i)),
            pl.BlockSpec((None, tm, kv_w), row),
            pl.BlockSpec((None, N_KV_HEADS, None, v_rows, tm), lambda b, i: (b, 0, i, 0, 0)),
            pl.BlockSpec((None, tm, four_w), row),
        ],
        out_shape=[
            jax.ShapeDtypeStruct((B, L, pool_w), F32),
            jax.ShapeDtypeStruct((B, n_q, HEAD_DIM, L), BF16),
            jax.ShapeDtypeStruct((B, L, kv_w), BF16),
            jax.ShapeDtypeStruct((B, N_KV_HEADS, nt, v_rows, tm), BF16),
            jax.ShapeDtypeStruct((B, L, four_w), F32),
        ],
        compiler_params=_params("parallel", "parallel"),
        name="in_proj",
    )(x, g, w, cc, ss, qn, kn)


def _attn_kernel(qt_ref, k_ref, vt_ref, o_ref, m_ref, l_ref, acc_ref, sa_ref, sb_ref, ma_ref, mb_ref,
                 *, tk, nk, unroll):
    assert nk == 1 or (unroll % 2 == 0 and nk % unroll == 0)
    m_ref[...] = jnp.full(m_ref.shape, -jnp.inf, F32)
    l_ref[...] = jnp.zeros(l_ref.shape, F32)
    acc_ref[...] = jnp.zeros(acc_ref.shape, F32)

    def scores(c, g, s_ref, cm_ref):
        kc = k_ref[pl.ds(pl.multiple_of(c * tk, tk), tk), :]
        st = jnp.dot(kc, qt_ref[g], preferred_element_type=F32)
        s_ref[g] = st
        cm_ref[g] = jnp.max(st, axis=0, keepdims=True)

    def accumulate(c, g, s_ref, cm_ref):
        m_old = m_ref[g]
        m_new = jnp.maximum(m_old, cm_ref[g])
        alpha = jnp.exp2(m_old - m_new)
        p = jnp.exp2(s_ref[g] - m_new)
        l_ref[g] = alpha * l_ref[g] + jnp.sum(p, axis=0, keepdims=True)
        acc_ref[g] = alpha * acc_ref[g] + jnp.dot(vt_ref[c], p.astype(BF16),
                                                  preferred_element_type=F32)
        m_ref[g] = m_new

    def step(c_scores, s_next, m_next, c_acc, s_cur, m_cur):
        for g in range(GQA_GROUP):
            if c_scores is not None:
                scores(c_scores, g, s_next, m_next)
            if c_acc is not None:
                accumulate(c_acc, g, s_cur, m_cur)

    bufs = ((sa_ref, ma_ref), (sb_ref, mb_ref))
    step(0, *bufs[0], None, None, None)
    if nk > 1:
        def group(i, carry):
            for u in range(unroll):
                c = unroll * i + u
                step(c + 1, *bufs[(u + 1) % 2], c, *bufs[u % 2])
            return carry

        lax.fori_loop(0, nk // unroll - 1, group, 0)
        for c in range(nk - unroll, nk - 1):
            step(c + 1, *bufs[(c + 1) % 2], c, *bufs[c % 2])
    step(None, None, None, nk - 1, *bufs[(nk - 1) % 2])
    for g in range(GQA_GROUP):
        o = acc_ref[g] / l_ref[g]
        o_ref[:, g * HEAD_DIM:(g + 1) * HEAD_DIM] = o.T.astype(o_ref.dtype)


def _attention(qt, k, vt, tq):
    B, n_q, _, L = qt.shape
    _, _, nk, v_rows, tk = vt.shape
    assert L % tq == 0 and nk * tk == L and n_q == N_KV_HEADS * GQA_GROUP
    gw = GQA_GROUP * HEAD_DIM
    unroll = ATTN_CHUNKS_PER_TRIP if nk % ATTN_CHUNKS_PER_TRIP == 0 and nk >= 4 * ATTN_CHUNKS_PER_TRIP else 2
    kern = functools.partial(_attn_kernel, tk=tk, nk=nk, unroll=unroll)
    return pl.pallas_call(
        kern,
        grid=(B, N_KV_HEADS, L // tq),
        in_specs=[
            pl.BlockSpec((None, GQA_GROUP, HEAD_DIM, tq), lambda b, j, i: (b, j, 0, i)),
            pl.BlockSpec((None, L, HEAD_DIM), lambda b, j, i: (b, 0, j)),
            pl.BlockSpec((None, None, nk, v_rows, tk), lambda b, j, i: (b, j, 0, 0, 0)),
        ],
        out_specs=pl.BlockSpec((None, tq, gw), lambda b, j, i: (b, i, j)),
        out_shape=jax.ShapeDtypeStruct((B, L, n_q * HEAD_DIM), BF16),
        scratch_shapes=[
            pltpu.VMEM((GQA_GROUP, 1, tq), F32),
            pltpu.VMEM((GQA_GROUP, 1, tq), F32),
            pltpu.VMEM((GQA_GROUP, v_rows, tq), F32),
            pltpu.VMEM((GQA_GROUP, tk, tq), F32),
            pltpu.VMEM((GQA_GROUP, tk, tq), F32),
            pltpu.VMEM((GQA_GROUP, 1, tq), F32),
            pltpu.VMEM((GQA_GROUP, 1, tq), F32),
        ],
        compiler_params=_params("parallel", "parallel", "arbitrary"),
        name="attention",
    )(qt, k, vt)


def _dft_cos_sin(n):
    idx = np.arange(n)
    ang = 2.0 * np.pi * ((idx[:, None] * idx[None, :]) % n) / n
    return np.cos(ang), np.sin(ang)


def _fourier_a_kernel(x_ref, f1_ref, twc_ref, tws_ref, zr_ref, zi_ref, *, l2, k):
    xt = pltpu.einshape("mkw->kmw", x_ref[...])
    zr, zi = [], []
    for t in range(k):
        z = jnp.dot(f1_ref[...], xt[t].astype(BF16), preferred_element_type=F32)
        a = z[:l2]
        b = z[l2:]
        c = twc_ref[:, t:t + 1]
        s = tws_ref[:, t:t + 1]
        zr.append(a * c + b * s)
        zi.append(b * c - a * s)
    zr_ref[...] = pltpu.einshape("kmw->mkw", jnp.stack(zr)).astype(zr_ref.dtype)
    zi_ref[...] = pltpu.einshape("kmw->mkw", jnp.stack(zi)).astype(zi_ref.dtype)


def _fourier_b_kernel(zr_ref, zi_ref, f2_ref, fc_ref, fw_ref, o_ref, hr_ref, hi_ref, *, l1, n_s2, norm):
    for a in range(n_s2):
        rows = slice(a * l1, (a + 1) * l1)
        zz = jnp.concatenate([zr_ref[rows, :], zi_ref[rows, :]], axis=0)
        hh = jnp.dot(f2_ref[...], zz, preferred_element_type=F32)
        hr_ref[rows, :] = hh[:l1].astype(BF16)
        hi_ref[rows, :] = hh[l1:].astype(BF16)
    ys = []
    for g in range(N_GROUPS):
        sl = slice(g * GROUP_CH, (g + 1) * GROUP_CH)
        hg = jnp.concatenate([hr_ref[:, sl], hi_ref[:, sl]], axis=1)
        f = jnp.dot(hg, fc_ref[...], preferred_element_type=F32) * norm
        ys.append(jnp.dot(f.astype(BF16), fw_ref[g], preferred_element_type=F32))
    y = jnp.concatenate(ys, axis=1).reshape(n_s2, l1, o_ref.shape[-1])
    o_ref[...] = pltpu.einshape("alw->law", y).astype(o_ref.dtype)


def _fourier(u, fw, l1, l2):
    B, L, width = u.shape
    k = min(FOURIER_S2_PER_STEP, l1)
    n_s2 = min(FOURIER_S2_PER_STEP, l2)
    assert l1 * l2 == L and l1 % k == 0 and l2 % n_s2 == 0 and width == N_GROUPS * GROUP_CH
    c2, s2 = _dft_cos_sin(l2)
    f1 = jnp.asarray(np.concatenate([c2, -s2], axis=0), BF16)
    c1, s1 = _dft_cos_sin(l1)
    f2 = jnp.asarray(np.block([[c1, s1], [-s1, c1]]), BF16)
    cc, sc = _dft_cos_sin(GROUP_CH)
    fc = jnp.asarray(np.concatenate([cc, sc], axis=0), BF16)
    ang = 2.0 * np.pi * ((np.arange(l2)[:, None] * np.arange(l1)[None, :]) % L) / L
    tw = lambda t: jnp.asarray(t.reshape(l2, l1 // k, k).transpose(1, 0, 2), F32)
    twc, tws = tw(np.cos(ang)), tw(np.sin(ang))

    x = u.reshape(B, l2, l1, width)
    col = lambda b, j: (b, 0, j, 0)
    const2 = lambda b, j: (0, 0)
    zr, zi = pl.pallas_call(
        functools.partial(_fourier_a_kernel, l2=l2, k=k),
        grid=(B, l1 // k),
        in_specs=[
            pl.BlockSpec((None, l2, k, width), col),
            pl.BlockSpec(f1.shape, const2),
            pl.BlockSpec((None, l2, k), lambda b, j: (j, 0, 0)),
            pl.BlockSpec((None, l2, k), lambda b, j: (j, 0, 0)),
        ],
        out_specs=[pl.BlockSpec((None, l2, k, width), col)] * 2,
        out_shape=[jax.ShapeDtypeStruct((B, l2, l1, width), BF16)] * 2,
        compiler_params=_params("parallel", "parallel"),
        name="fourier_a",
    )(x, f1, twc, tws)

    zr = zr.reshape(B, L, width)
    zi = zi.reshape(B, L, width)
    rows = lambda b, j: (b, j, 0)
    out = pl.pallas_call(
        functools.partial(_fourier_b_kernel, l1=l1, n_s2=n_s2, norm=1.0 / math.sqrt(L * GROUP_CH)),
        grid=(B, l2 // n_s2),
        in_specs=[
            pl.BlockSpec((None, n_s2 * l1, width), rows),
            pl.BlockSpec((None, n_s2 * l1, width), rows),
            pl.BlockSpec(f2.shape, const2),
            pl.BlockSpec(fc.shape, const2),
            pl.BlockSpec(fw.shape, lambda b, j: (0, 0, 0)),
        ],
        out_specs=pl.BlockSpec((None, l1, n_s2, width), lambda b, j: (b, 0, j, 0)),
        out_shape=jax.ShapeDtypeStruct((B, l1, l2, width), BF16),
        scratch_shapes=[pltpu.VMEM((n_s2 * l1, width), BF16)] * 2,
        compiler_params=_params("parallel", "parallel"),
        name="fourier_b",
    )(zr, zi, f2, fc, fw)
    return out.reshape(B, L, width)


def _mix_out_kernel(up_ref, uc_ref, un_ref, attn_ref, four_ref, x_ref, wo_ref, pw_ref, ps_ref,
                    g_ref, o_ref, ext_ref, v_ref, m_ref, pooled_ref, *, tm, seq_len, widths):
    pool_w, attn_w, four_w = widths
    i = pl.program_id(1)
    last = pl.num_programs(1) - 1
    ext_ref[0:POOL_HALO, :] = jnp.where(i > 0, up_ref[...], 0.0)
    ext_ref[POOL_HALO:POOL_HALO + tm, :] = uc_ref[...]
    ext_ref[POOL_HALO + tm:, :] = jnp.where(i < last, un_ref[...], 0.0)

    halves = [slice(r * (tm // 2), (r + 1) * (tm // 2)) for r in range(2)]
    t = i * tm + lax.broadcasted_iota(jnp.int32, (tm, 1), 0)
    for g in range(len(POOL_WINDOWS)):
        sl = slice(g * GROUP_CH, (g + 1) * GROUP_CH)
        ug = ext_ref[:, sl]
        hi = ug.astype(BF16)
        lo = (ug - hi.astype(F32)).astype(BF16)
        pw2 = jnp.concatenate([pw_ref[g], pw_ref[g]], axis=0)
        v_ref[:, sl] = jnp.dot(jnp.concatenate([hi, lo], axis=1), pw2, preferred_element_type=F32)

    for rows in halves:
        m_ref[rows, :] = (
            jnp.dot(attn_ref[rows, :], wo_ref[pool_w:pool_w + attn_w, :], preferred_element_type=F32)
            + jnp.dot(four_ref[rows, :], wo_ref[pool_w + attn_w:, :],
                      preferred_element_type=F32))

    for g, w in enumerate(POOL_WINDOWS):
        sl = slice(g * GROUP_CH, (g + 1) * GROUP_CH)
        lo = jnp.maximum(t - w // 2, 0)
        hi = jnp.minimum(t + (w - 1 - w // 2), seq_len - 1)
        cnt = (hi - lo + 1).astype(F32)
        win = v_ref[POOL_HALO - w // 2:POOL_HALO - w // 2 + tm, sl]
        for d in range(1 - w // 2, w - w // 2):
            win = win + v_ref[POOL_HALO + d:POOL_HALO + d + tm, sl]
        yg = (win / cnt - v_ref[POOL_HALO:POOL_HALO + tm, sl]) * ps_ref[:, sl]
        pooled_ref[:, sl] = yg.astype(BF16)
    for rows in halves:
        m = m_ref[rows, :] + jnp.dot(pooled_ref[rows, :], wo_ref[0:pool_w, :],
                                     preferred_element_type=F32)
        o_ref[rows, :] = x_ref[rows, :] + _rms(m, g_ref[...])


def _mix_out(x, u_pool, attn, four, wo, pw, ps, g, tm):
    B, L, D = x.shape
    widths = (u_pool.shape[-1], attn.shape[-1], four.shape[-1])
    assert L % tm == 0 and tm % (2 * POOL_HALO) == 0 and wo.shape == (sum(widths), D)
    assert widths[0] == len(POOL_WINDOWS) * GROUP_CH
    hb = tm // POOL_HALO
    n_hb = L // POOL_HALO
    row = lambda b, i: (b, i, 0)
    const2 = lambda b, i: (0, 0)
    kern = functools.partial(_mix_out_kernel, tm=tm, seq_len=L, widths=widths)
    return pl.pallas_call(
        kern,
        grid=(B, L // tm),
        in_specs=[
            pl.BlockSpec((None, POOL_HALO, widths[0]), lambda b, i: (b, jnp.maximum(i * hb - 1, 0), 0)),
            pl.BlockSpec((None, tm, widths[0]), row),
            pl.BlockSpec((None, POOL_HALO, widths[0]),
                         lambda b, i: (b, jnp.minimum((i + 1) * hb, n_hb - 1), 0)),
            pl.BlockSpec((None, tm, widths[1]), row),
            pl.BlockSpec((None, tm, widths[2]), row),
            pl.BlockSpec((None, tm, D), row),
            pl.BlockSpec(wo.shape, const2),
            pl.BlockSpec(pw.shape, lambda b, i: (0, 0, 0)),
            pl.BlockSpec((1, widths[0]), const2),
            pl.BlockSpec((1, D), const2),
        ],
        out_specs=pl.BlockSpec((None, tm, D), row),
        out_shape=jax.ShapeDtypeStruct((B, L, D), F32),
        scratch_shapes=[
            pltpu.VMEM((tm + 2 * POOL_HALO, widths[0]), F32),
            pltpu.VMEM((tm + 2 * POOL_HALO, widths[0]), F32),
            pltpu.VMEM((tm, D), F32),
            pltpu.VMEM((tm, widths[0]), BF16),
        ],
        compiler_params=_params("parallel", "parallel"),
        name="mix_out",
    )(u_pool, u_pool, u_pool, attn, four, x, wo, pw, ps, g)


def _gated_gelu(gate, half_val):
    c0 = math.sqrt(2.0 / math.pi)
    inner = gate * (gate * gate * (c0 * 0.044715) + c0)
    return gate * (1.0 + jnp.tanh(inner)) * half_val


def _ffn_kernel(xp_ref, x_ref, xn_ref, gpre_ref, wg_ref, wv_ref, cwg_ref, cwv_ref, cbg_ref, cbv_ref,
                wd_ref, gpost_ref, o_ref, h_ref, ug_ref, uv_ref, *, tm, sub):
    i = pl.program_id(1)
    j = pl.program_id(2)
    rows = tm + 2 * CONV_HALO

    @pl.when(j == 0)
    def _():
        g = gpre_ref[...]
        hp = jnp.where(i > 0, _rms(xp_ref[...], g), 0.0)
        hn = jnp.where(i < pl.num_programs(1) - 1, _rms(xn_ref[...], g), 0.0)
        h_ref[0:CONV_HALO, :] = hp.astype(BF16)
        h_ref[CONV_HALO:CONV_HALO + tm, :] = _rms(x_ref[...], g).astype(BF16)
        h_ref[CONV_HALO + tm:, :] = hn.astype(BF16)
        o_ref[...] = jnp.zeros(o_ref.shape, F32)

    def conv(u_ref, cw, cb, r0, nr):
        prev = u_ref[CONV_HALO - 1 + r0:CONV_HALO - 1 + r0 + nr, :]
        cur = u_ref[CONV_HALO + r0:CONV_HALO + r0 + nr, :]
        nxt = u_ref[CONV_HALO + 1 + r0:CONV_HALO + 1 + r0 + nr, :]
        return prev * cw[0:1, :] + cur * cw[1:2, :] + nxt * cw[2:3, :] + cb

    def step(finish):
        h = h_ref[...]
        n_sub = wg_ref.shape[1] // sub
        nr = tm // FFN_ROW_BLOCKS
        for s in range(n_sub):
            sl = slice(s * sub, (s + 1) * sub)
            ug_ref[s] = jnp.dot(h, wg_ref[:, sl], preferred_element_type=F32)
            uv_ref[s] = jnp.dot(h, wv_ref[:, sl], preferred_element_type=F32)
        for s in range(n_sub):
            sl = slice(s * sub, (s + 1) * sub)
            cwg, cbg = cwg_ref[:, sl], cbg_ref[:, sl]
            cwv, cbv = 0.5 * cwv_ref[:, sl], 0.5 * cbv_ref[:, sl]
            for r0 in range(0, tm, nr):
                rows = slice(r0, r0 + nr)
                gate = conv(ug_ref.at[s], cwg, cbg, r0, nr)
                half_val = conv(uv_ref.at[s], cwv, cbv, r0, nr)
                act = _gated_gelu(gate, half_val).astype(BF16)
                d = jnp.dot(act, wd_ref[sl, :], preferred_element_type=F32)
                if finish and s == n_sub - 1:
                    o_ref[rows, :] = x_ref[rows, :] + _rms(o_ref[rows, :] + d, gpost_ref[...])
                else:
                    o_ref[rows, :] += d

    last = pl.num_programs(2) - 1
    pl.when(j < last)(functools.partial(step, False))
    pl.when(j == last)(functools.partial(step, True))


def _ffn(x, gpre, w_up, conv_w, conv_b, w_down, gpost, layer, tm, chunk):
    B, L, D = x.shape
    d_ff = w_down.shape[1]
    assert L % tm == 0 and tm % (FFN_ROW_BLOCKS * CONV_HALO) == 0
    assert d_ff % chunk == 0 and chunk % FFN_SUB_CHUNK == 0 and w_up.shape[1:] == (D, 2 * d_ff)
    nc = d_ff // chunk
    hb = tm // CONV_HALO
    n_hb = L // CONV_HALO
    const2 = lambda b, i, j: (0, 0)
    gate_col = lambda b, i, j: (0, j)
    val_col = lambda b, i, j: (0, nc + j)
    sub = min(FFN_SUB_CHUNK, chunk)
    kern = functools.partial(_ffn_kernel, tm=tm, sub=sub)
    return pl.pallas_call(
        kern,
        grid=(B, L // tm, nc),
        in_specs=[
            pl.BlockSpec((None, CONV_HALO, D), lambda b, i, j: (b, jnp.maximum(i * hb - 1, 0), 0)),
            pl.BlockSpec((None, tm, D), lambda b, i, j: (b, i, 0)),
            pl.BlockSpec((None, CONV_HALO, D),
                         lambda b, i, j: (b, jnp.minimum((i + 1) * hb, n_hb - 1), 0)),
            pl.BlockSpec((1, D), const2),
            pl.BlockSpec((None, D, chunk), lambda b, i, j: (layer, 0, j)),
            pl.BlockSpec((None, D, chunk), lambda b, i, j: (layer, 0, nc + j)),
            pl.BlockSpec((3, chunk), gate_col),
            pl.BlockSpec((3, chunk), val_col),
            pl.BlockSpec((1, chunk), gate_col),
            pl.BlockSpec((1, chunk), val_col),
            pl.BlockSpec((None, chunk, D), lambda b, i, j: (layer, j, 0)),
            pl.BlockSpec((1, D), const2),
        ],
        out_specs=pl.BlockSpec((None, tm, D), lambda b, i, j: (b, i, 0)),
        out_shape=jax.ShapeDtypeStruct((B, L, D), F32),
        scratch_shapes=[
            pltpu.VMEM((tm + 2 * CONV_HALO, D), BF16),
            pltpu.VMEM((chunk // sub, tm + 2 * CONV_HALO, sub), F32),
            pltpu.VMEM((chunk // sub, tm + 2 * CONV_HALO, sub), F32),
        ],
        compiler_params=_params("parallel", "parallel", "arbitrary"),
        name="ffn",
    )(x, x, x, gpre, w_up, w_up, conv_w, conv_w, conv_b, conv_b, w_down, gpost)


def _rope_tables(seq_len):
    quarter = HEAD_DIM // 4
    t = jnp.arange(seq_len, dtype=jnp.int32)
    row = (t // GRID_W).astype(F32)
    col = (t % GRID_W).astype(F32)
    inv_freq = 1.0 / (ROPE_THETA ** (jnp.arange(quarter, dtype=F32) / quarter))
    ang = jnp.concatenate([row[:, None] * inv_freq[None, :], col[:, None] * inv_freq[None, :]], axis=-1)
    cos, sin = jnp.cos(ang), jnp.sin(ang)
    return jnp.concatenate([cos, cos], axis=-1), jnp.concatenate([-sin, sin], axis=-1)


def _fourier_split(seq_len):
    l2 = 1 << (int(math.log2(seq_len)) // 2)
    return seq_len // l2, l2


def _trunk(x, layers):
    B, L, D = x.shape
    ts = _tiles(L)
    cc, ss = _rope_tables(L)
    l1, l2 = _fourier_split(L)
    for p in layers:
        u_pool, qt, k, vt, u_four = _in_proj(x, p["g_pre_mix"], p["w_in"], cc, ss, p["q_norm"],
                                             p["k_norm"], p["widths"], ts["tm_proj"])
        attn = _attention(qt, k, vt, ts["tq"])
        four = _fourier(u_four, p["fourier_w"], l1, l2)
        x = _mix_out(x, u_pool, attn, four, p["w_out"], p["pool_w"], p["pool_scale"],
                     p["g_post_mix"], ts["tm_mix"])
        x = _ffn(x, p["g_pre_ffn"], p["w_up"], p["conv_w"], p["conv_b"], p["w_down"],
                 p["g_post_ffn"], p["layer"], ts["tm_ffn"], ts["ff_chunk"])
    return x


def kernel(x_prompt, x_sample, g_pre_mix, g_post_mix, w_in, pool_w, pool_scale, q_norm, k_norm,
           fourier_w, w_out, g_pre_ffn, g_post_ffn, w_up, conv_w, conv_b, w_down):
    depth = w_in.shape[0]
    pool_width = pool_scale.shape[-1]
    four_width = fourier_w.shape[1] * fourier_w.shape[2]
    kv_width = N_KV_HEADS * HEAD_DIM
    q_width = w_in.shape[-1] - pool_width - four_width - 2 * kv_width
    w_up_b, w_down_b = w_up.astype(BF16), w_down.astype(BF16)
    layers = []
    for l in range(depth):
        layers.append(dict(
            layer=l, widths=(pool_width, q_width, kv_width, four_width),
            g_pre_mix=g_pre_mix[l][None, :], g_post_mix=g_post_mix[l][None, :],
            w_in=w_in[l].astype(BF16), pool_w=pool_w[l].astype(BF16),
            pool_scale=pool_scale[l][None, :], q_norm=q_norm[l][None, :], k_norm=k_norm[l][None, :],
            fourier_w=fourier_w[l].astype(BF16), w_out=w_out[l].astype(BF16),
            g_pre_ffn=g_pre_ffn[l][None, :], g_post_ffn=g_post_ffn[l][None, :],
            w_up=w_up_b, conv_w=conv_w[l], conv_b=conv_b[l][None, :], w_down=w_down_b))
    return _trunk(x_prompt, layers), _trunk(x_sample, layers)
```

```python
import functools
import math

import jax
import jax.numpy as jnp
import numpy as np
from jax import lax
from jax.experimental import pallas as pl
from jax.experimental.pallas import tpu as pltpu

F32 = jnp.float32
BF16 = jnp.bfloat16

NORM_EPS = 1e-6
GRID_W = 64
HEAD_DIM = 128
N_KV_HEADS = 2
GQA_GROUP = 4
POOL_WINDOWS = (2, 4, 8, 16)
POOL_HALO = 8
GROUP_CH = 128
N_GROUPS = 4
ROPE_THETA = 10000.0
CONV_HALO = 16
ATTN_CHUNKS_PER_TRIP = 4
FOURIER_S2_PER_STEP = 16
FFN_ROW_BLOCKS = 2
FFN_SUB_CHUNK = 256

V7X_VMEM_BYTES = 64 * 1024 * 1024
VMEM_LIMIT = V7X_VMEM_BYTES - 4 * 1024 * 1024


def _tiles(seq_len):
    return dict(
        tm_proj=min(512, seq_len),
        tm_mix=min(512, seq_len),
        tm_ffn=min(1024, seq_len),
        ff_chunk=512,
        tq=min(512, seq_len),
    )


def _params(*sem):
    return pltpu.CompilerParams(dimension_semantics=sem, vmem_limit_bytes=VMEM_LIMIT)


def _rms(x, g):
    return x * lax.rsqrt(jnp.mean(x * x, axis=-1, keepdims=True) + NORM_EPS) * g


def _in_proj_kernel(x_ref, g_ref, w_ref, cc_ref, ss_ref, qn_ref, kn_ref,
                    pool_ref, q_ref, k_ref, v_ref, four_ref, *, widths, scale):
    pool_w, q_w, kv_w, four_w = widths
    h = _rms(x_ref[...], g_ref[...]).astype(BF16)
    cc = cc_ref[...]
    ss = ss_ref[...]

    def seg(lo, width):
        return jnp.dot(h, w_ref[:, lo:lo + width], preferred_element_type=F32)

    def norm_rope(zh, gain, out_scale):
        y = zh * lax.rsqrt(jnp.mean(zh * zh, axis=-1, keepdims=True) + NORM_EPS) * gain
        y = y * cc + pltpu.roll(y, HEAD_DIM // 2, axis=1) * ss
        return y * out_scale

    pair = 2 * HEAD_DIM
    for p in range(q_w // pair):
        z = seg(pool_w + p * pair, pair)
        for s in range(2):
            zh = z[:, s * HEAD_DIM:(s + 1) * HEAD_DIM]
            q_ref[2 * p + s] = norm_rope(zh, qn_ref[...], scale).T.astype(BF16)
    z = seg(pool_w + q_w, kv_w)
    for s in range(kv_w // HEAD_DIM):
        zh = z[:, s * HEAD_DIM:(s + 1) * HEAD_DIM]
        k_ref[:, s * HEAD_DIM:(s + 1) * HEAD_DIM] = norm_rope(zh, kn_ref[...], 1.0).astype(BF16)
    z = seg(pool_w + q_w + kv_w, kv_w)
    for s in range(kv_w // HEAD_DIM):
        v_ref[s] = z[:, s * HEAD_DIM:(s + 1) * HEAD_DIM].T.astype(BF16)
    four_ref[...] = seg(pool_w + q_w + 2 * kv_w, four_w)
    pool_ref[...] = seg(0, pool_w)


def _in_proj(x, g, w, cc, ss, qn, kn, widths, tm):
    B, L, D = x.shape
    pool_w, q_w, kv_w, four_w = widths
    assert L % tm == 0 and w.shape == (D, sum(widths) + kv_w)
    nt = L // tm
    row = lambda b, i: (b, i, 0)
    const2 = lambda b, i: (0, 0)
    n_q = q_w // HEAD_DIM
    v_rows = HEAD_DIM
    kern = functools.partial(_in_proj_kernel, widths=widths, scale=HEAD_DIM ** -0.5 * math.log2(math.e))
    return pl.pallas_call(
        kern,
        grid=(B, nt),
        in_specs=[
            pl.BlockSpec((None, tm, D), row),
            pl.BlockSpec((1, D), const2),
            pl.BlockSpec(w.shape, const2),
            pl.BlockSpec((tm, HEAD_DIM), lambda b, i: (i, 0)),
            pl.BlockSpec((tm, HEAD_DIM), lambda b, i: (i, 0)),
            pl.BlockSpec((1, HEAD_DIM), const2),
            pl.BlockSpec((1, HEAD_DIM), const2),
        ],
        out_specs=[
            pl.BlockSpec((None, tm, pool_w), row),
            pl.BlockSpec((None, n_q, HEAD_DIM, tm), lambda b, i: (b, 0, 0, i)),
            pl.BlockSpec((None, tm, kv_w), row),
            pl.BlockSpec((None, N_KV_HEADS, None, v_rows, tm), lambda b, i: (b, 0, i, 0, 0)),
            pl.BlockSpec((None, tm, four_w), row),
        ],
        out_shape=[
            jax.ShapeDtypeStruct((B, L, pool_w), F32),
            jax.ShapeDtypeStruct((B, n_q, HEAD_DIM, L), BF16),
            jax.ShapeDtypeStruct((B, L, kv_w), BF16),
            jax.ShapeDtypeStruct((B, N_KV_HEADS, nt, v_rows, tm), BF16),
            jax.ShapeDtypeStruct((B, L, four_w), F32),
        ],
        compiler_params=_params("parallel", "parallel"),
        name="in_proj",
    )(x, g, w, cc, ss, qn, kn)


def _attn_kernel(qt_ref, k_ref, vt_ref, o_ref, m_ref, l_ref, acc_ref, sa_ref, sb_ref, ma_ref, mb_ref,
                 *, tk, nk, unroll):
    assert nk == 1 or (unroll % 2 == 0 and nk % unroll == 0)
    m_ref[...] = jnp.full(m_ref.shape, -jnp.inf, F32)
    l_ref[...] = jnp.zeros(l_ref.shape, F32)
    acc_ref[...] = jnp.zeros(acc_ref.shape, F32)

    def scores(c, g, s_ref, cm_ref):
        kc = k_ref[pl.ds(pl.multiple_of(c * tk, tk), tk), :]
        st = jnp.dot(kc, qt_ref[g], preferred_element_type=F32)
        s_ref[g] = st
        cm_ref[g] = jnp.max(st, axis=0, keepdims=True)

    def accumulate(c, g, s_ref, cm_ref):
        m_old = m_ref[g]
        m_new = jnp.maximum(m_old, cm_ref[g])
        alpha = jnp.exp2(m_old - m_new)
        p = jnp.exp2(s_ref[g] - m_new)
        l_ref[g] = alpha * l_ref[g] + jnp.sum(p, axis=0, keepdims=True)
        acc_ref[g] = alpha * acc_ref[g] + jnp.dot(vt_ref[c], p.astype(BF16),
                                                  preferred_element_type=F32)
        m_ref[g] = m_new

    def step(c_scores, s_next, m_next, c_acc, s_cur, m_cur):
        for g in range(GQA_GROUP):
            if c_scores is not None:
                scores(c_scores, g, s_next, m_next)
            if c_acc is not None:
                accumulate(c_acc, g, s_cur, m_cur)

    bufs = ((sa_ref, ma_ref), (sb_ref, mb_ref))
    step(0, *bufs[0], None, None, None)
    if nk > 1:
        def group(i, carry):
            for u in range(unroll):
                c = unroll * i + u
                step(c + 1, *bufs[(u + 1) % 2], c, *bufs[u % 2])
            return carry

        lax.fori_loop(0, nk // unroll - 1, group, 0)
        for c in range(nk - unroll, nk - 1):
            step(c + 1, *bufs[(c + 1) % 2], c, *bufs[c % 2])
    step(None, None, None, nk - 1, *bufs[(nk - 1) % 2])
    for g in range(GQA_GROUP):
        o = acc_ref[g] / l_ref[g]
        o_ref[:, g * HEAD_DIM:(g + 1) * HEAD_DIM] = o.T.astype(o_ref.dtype)


def _attention(qt, k, vt, tq):
    B, n_q, _, L = qt.shape
    _, _, nk, v_rows, tk = vt.shape
    assert L % tq == 0 and nk * tk == L and n_q == N_KV_HEADS * GQA_GROUP
    gw = GQA_GROUP * HEAD_DIM
    unroll = ATTN_CHUNKS_PER_TRIP if nk % ATTN_CHUNKS_PER_TRIP == 0 and nk >= 4 * ATTN_CHUNKS_PER_TRIP else 2
    kern = functools.partial(_attn_kernel, tk=tk, nk=nk, unroll=unroll)
    return pl.pallas_call(
        kern,
        grid=(B, N_KV_HEADS, L // tq),
        in_specs=[
            pl.BlockSpec((None, GQA_GROUP, HEAD_DIM, tq), lambda b, j, i: (b, j, 0, i)),
            pl.BlockSpec((None, L, HEAD_DIM), lambda b, j, i: (b, 0, j)),
            pl.BlockSpec((None, None, nk, v_rows, tk), lambda b, j, i: (b, j, 0, 0, 0)),
        ],
        out_specs=pl.BlockSpec((None, tq, gw), lambda b, j, i: (b, i, j)),
        out_shape=jax.ShapeDtypeStruct((B, L, n_q * HEAD_DIM), BF16),
        scratch_shapes=[
            pltpu.VMEM((GQA_GROUP, 1, tq), F32),
            pltpu.VMEM((GQA_GROUP, 1, tq), F32),
            pltpu.VMEM((GQA_GROUP, v_rows, tq), F32),
            pltpu.VMEM((GQA_GROUP, tk, tq), F32),
            pltpu.VMEM((GQA_GROUP, tk, tq), F32),
            pltpu.VMEM((GQA_GROUP, 1, tq), F32),
            pltpu.VMEM((GQA_GROUP, 1, tq), F32),
        ],
        compiler_params=_params("parallel", "parallel", "arbitrary"),
        name="attention",
    )(qt, k, vt)


def _dft_cos_sin(n):
    idx = np.arange(n)
    ang = 2.0 * np.pi * ((idx[:, None] * idx[None, :]) % n) / n
    return np.cos(ang), np.sin(ang)


def _fourier_a_kernel(x_ref, f1_ref, twc_ref, tws_ref, zr_ref, zi_ref, *, l2, k):
    xt = pltpu.einshape("mkw->kmw", x_ref[...])
    zr, zi = [], []
    for t in range(k):
        z = jnp.dot(f1_ref[...], xt[t].astype(BF16), preferred_element_type=F32)
        a = z[:l2]
        b = z[l2:]
        c = twc_ref[:, t:t + 1]
        s = tws_ref[:, t:t + 1]
        zr.append(a * c + b * s)
        zi.append(b * c - a * s)
    zr_ref[...] = pltpu.einshape("kmw->mkw", jnp.stack(zr)).astype(zr_ref.dtype)
    zi_ref[...] = pltpu.einshape("kmw->mkw", jnp.stack(zi)).astype(zi_ref.dtype)


def _fourier_b_kernel(zr_ref, zi_ref, f2_ref, fc_ref, fw_ref, o_ref, hr_ref, hi_ref, *, l1, n_s2, norm):
    for a in range(n_s2):
        rows = slice(a * l1, (a + 1) * l1)
        zz = jnp.concatenate([zr_ref[rows, :], zi_ref[rows, :]], axis=0)
        hh = jnp.dot(f2_ref[...], zz, preferred_element_type=F32)
        hr_ref[rows, :] = hh[:l1].astype(BF16)
        hi_ref[rows, :] = hh[l1:].astype(BF16)
    ys = []
    for g in range(N_GROUPS):
        sl = slice(g * GROUP_CH, (g + 1) * GROUP_CH)
        hg = jnp.concatenate([hr_ref[:, sl], hi_ref[:, sl]], axis=1)
        f = jnp.dot(hg, fc_ref[...], preferred_element_type=F32) * norm
        ys.append(jnp.dot(f.astype(BF16), fw_ref[g], preferred_element_type=F32))
    y = jnp.concatenate(ys, axis=1).reshape(n_s2, l1, o_ref.shape[-1])
    o_ref[...] = pltpu.einshape("alw->law", y).astype(o_ref.dtype)


def _fourier(u, fw, l1, l2):
    B, L, width = u.shape
    k = min(FOURIER_S2_PER_STEP, l1)
    n_s2 = min(FOURIER_S2_PER_STEP, l2)
    assert l1 * l2 == L and l1 % k == 0 and l2 % n_s2 == 0 and width == N_GROUPS * GROUP_CH
    c2, s2 = _dft_cos_sin(l2)
    f1 = jnp.asarray(np.concatenate([c2, -s2], axis=0), BF16)
    c1, s1 = _dft_cos_sin(l1)
    f2 = jnp.asarray(np.block([[c1, s1], [-s1, c1]]), BF16)
    cc, sc = _dft_cos_sin(GROUP_CH)
    fc = jnp.asarray(np.concatenate([cc, sc], axis=0), BF16)
    ang = 2.0 * np.pi * ((np.arange(l2)[:, None] * np.arange(l1)[None, :]) % L) / L
    tw = lambda t: jnp.asarray(t.reshape(l2, l1 // k, k).transpose(1, 0, 2), F32)
    twc, tws = tw(np.cos(ang)), tw(np.sin(ang))

    x = u.reshape(B, l2, l1, width)
    col = lambda b, j: (b, 0, j, 0)
    const2 = lambda b, j: (0, 0)
    zr, zi = pl.pallas_call(
        functools.partial(_fourier_a_kernel, l2=l2, k=k),
        grid=(B, l1 // k),
        in_specs=[
            pl.BlockSpec((None, l2, k, width), col),
            pl.BlockSpec(f1.shape, const2),
            pl.BlockSpec((None, l2, k), lambda b, j: (j, 0, 0)),
            pl.BlockSpec((None, l2, k), lambda b, j: (j, 0, 0)),
        ],
        out_specs=[pl.BlockSpec((None, l2, k, width), col)] * 2,
        out_shape=[jax.ShapeDtypeStruct((B, l2, l1, width), BF16)] * 2,
        compiler_params=_params("parallel", "parallel"),
        name="fourier_a",
    )(x, f1, twc, tws)

    zr = zr.reshape(B, L, width)
    zi = zi.reshape(B, L, width)
    rows = lambda b, j: (b, j, 0)
    out = pl.pallas_call(
        functools.partial(_fourier_b_kernel, l1=l1, n_s2=n_s2, norm=1.0 / math.sqrt(L * GROUP_CH)),
        grid=(B, l2 // n_s2),
        in_specs=[
            pl.BlockSpec((None, n_s2 * l1, width), rows),
            pl.BlockSpec((None, n_s2 * l1, width), rows),
            pl.BlockSpec(f2.shape, const2),
            pl.BlockSpec(fc.shape, const2),
            pl.BlockSpec(fw.shape, lambda b, j: (0, 0, 0)),
        ],
        out_specs=pl.BlockSpec((None, l1, n_s2, width), lambda b, j: (b, 0, j, 0)),
        out_shape=jax.ShapeDtypeStruct((B, l1, l2, width), BF16),
        scratch_shapes=[pltpu.VMEM((n_s2 * l1, width), BF16)] * 2,
        compiler_params=_params("parallel", "parallel"),
        name="fourier_b",
    )(zr, zi, f2, fc, fw)
    return out.reshape(B, L, width)


def _mix_out_kernel(up_ref, uc_ref, un_ref, attn_ref, four_ref, x_ref, wo_ref, pw_ref, ps_ref,
                    g_ref, o_ref, ext_ref, v_ref, m_ref, pooled_ref, *, tm, seq_len, widths):
    pool_w, attn_w, four_w = widths
    i = pl.program_id(1)
    last = pl.num_programs(1) - 1
    ext_ref[0:POOL_HALO, :] = jnp.where(i > 0, up_ref[...], 0.0)
    ext_ref[POOL_HALO:POOL_HALO + tm, :] = uc_ref[...]
    ext_ref[POOL_HALO + tm:, :] = jnp.where(i < last, un_ref[...], 0.0)

    halves = [slice(r * (tm // 2), (r + 1) * (tm // 2)) for r in range(2)]
    t = i * tm + lax.broadcasted_iota(jnp.int32, (tm, 1), 0)
    for g in range(len(POOL_WINDOWS)):
        sl = slice(g * GROUP_CH, (g + 1) * GROUP_CH)
        ug = ext_ref[:, sl]
        hi = ug.astype(BF16)
        lo = (ug - hi.astype(F32)).astype(BF16)
        pw2 = jnp.concatenate([pw_ref[g], pw_ref[g]], axis=0)
        v_ref[:, sl] = jnp.dot(jnp.concatenate([hi, lo], axis=1), pw2, preferred_element_type=F32)

    for rows in halves:
        m_ref[rows, :] = (
            jnp.dot(attn_ref[rows, :], wo_ref[pool_w:pool_w + attn_w, :], preferred_element_type=F32)
            + jnp.dot(four_ref[rows, :], wo_ref[pool_w + attn_w:, :],
                      preferred_element_type=F32))

    for g, w in enumerate(POOL_WINDOWS):
        sl = slice(g * GROUP_CH, (g + 1) * GROUP_CH)
        lo = jnp.maximum(t - w // 2, 0)
        hi = jnp.minimum(t + (w - 1 - w // 2), seq_len - 1)
        cnt = (hi - lo + 1).astype(F32)
        win = v_ref[POOL_HALO - w // 2:POOL_HALO - w // 2 + tm, sl]
        for d in range(1 - w // 2, w - w // 2):
            win = win + v_ref[POOL_HALO + d:POOL_HALO + d + tm, sl]
        yg = (win / cnt - v_ref[POOL_HALO:POOL_HALO + tm, sl]) * ps_ref[:, sl]
        pooled_ref[:, sl] = yg.astype(BF16)
    for rows in halves:
        m = m_ref[rows, :] + jnp.dot(pooled_ref[rows, :], wo_ref[0:pool_w, :],
                                     preferred_element_type=F32)
        o_ref[rows, :] = x_ref[rows, :] + _rms(m, g_ref[...])


def _mix_out(x, u_pool, attn, four, wo, pw, ps, g, tm):
    B, L, D = x.shape
    widths = (u_pool.shape[-1], attn.shape[-1], four.shape[-1])
    assert L % tm == 0 and tm % (2 * POOL_HALO) == 0 and wo.shape == (sum(widths), D)
    assert widths[0] == len(POOL_WINDOWS) * GROUP_CH
    hb = tm // POOL_HALO
    n_hb = L // POOL_HALO
    row = lambda b, i: (b, i, 0)
    const2 = lambda b, i: (0, 0)
    kern = functools.partial(_mix_out_kernel, tm=tm, seq_len=L, widths=widths)
    return pl.pallas_call(
        kern,
        grid=(B, L // tm),
        in_specs=[
            pl.BlockSpec((None, POOL_HALO, widths[0]), lambda b, i: (b, jnp.maximum(i * hb - 1, 0), 0)),
            pl.BlockSpec((None, tm, widths[0]), row),
            pl.BlockSpec((None, POOL_HALO, widths[0]),
                         lambda b, i: (b, jnp.minimum((i + 1) * hb, n_hb - 1), 0)),
            pl.BlockSpec((None, tm, widths[1]), row),
            pl.BlockSpec((None, tm, widths[2]), row),
            pl.BlockSpec((None, tm, D), row),
            pl.BlockSpec(wo.shape, const2),
            pl.BlockSpec(pw.shape, lambda b, i: (0, 0, 0)),
            pl.BlockSpec((1, widths[0]), const2),
            pl.BlockSpec((1, D), const2),
        ],
        out_specs=pl.BlockSpec((None, tm, D), row),
        out_shape=jax.ShapeDtypeStruct((B, L, D), F32),
        scratch_shapes=[
            pltpu.VMEM((tm + 2 * POOL_HALO, widths[0]), F32),
            pltpu.VMEM((tm + 2 * POOL_HALO, widths[0]), F32),
            pltpu.VMEM((tm, D), F32),
            pltpu.VMEM((tm, widths[0]), BF16),
        ],
        compiler_params=_params("parallel", "parallel"),
        name="mix_out",
    )(u_pool, u_pool, u_pool, attn, four, x, wo, pw, ps, g)


def _gated_gelu(gate, half_val):
    c0 = math.sqrt(2.0 / math.pi)
    inner = gate * (gate * gate * (c0 * 0.044715) + c0)
    return gate * (1.0 + jnp.tanh(inner)) * half_val


def _ffn_kernel(xp_ref, x_ref, xn_ref, gpre_ref, wg_ref, wv_ref, cwg_ref, cwv_ref, cbg_ref, cbv_ref,
                wd_ref, gpost_ref, o_ref, h_ref, ug_ref, uv_ref, *, tm, sub, nc):
    i = pl.program_id(1)
    j = pl.program_id(2)
    rows = tm + 2 * CONV_HALO

    @pl.when(j == 0)
    def _():
        g = gpre_ref[...]
        hp = jnp.where(i > 0, _rms(xp_ref[...], g), 0.0)
        hn = jnp.where(i < pl.num_programs(1) - 1, _rms(xn_ref[...], g), 0.0)
        h_ref[0:CONV_HALO, :] = hp.astype(BF16)
        h_ref[CONV_HALO:CONV_HALO + tm, :] = _rms(x_ref[...], g).astype(BF16)
        h_ref[CONV_HALO + tm:, :] = hn.astype(BF16)

    def conv(u_ref, cw, cb, r0, nr):
        prev = u_ref[CONV_HALO - 1 + r0:CONV_HALO - 1 + r0 + nr, :]
        cur = u_ref[CONV_HALO + r0:CONV_HALO + r0 + nr, :]
        nxt = u_ref[CONV_HALO + 1 + r0:CONV_HALO + 1 + r0 + nr, :]
        return prev * cw[0:1, :] + cur * cw[1:2, :] + nxt * cw[2:3, :] + cb

    def step(first, finish):
        h = h_ref[...]
        n_sub = wg_ref.shape[1] // sub
        nr = tm // FFN_ROW_BLOCKS
        for s in range(n_sub):
            sl = slice(s * sub, (s + 1) * sub)
            ug_ref[s] = jnp.dot(h, wg_ref[:, sl], preferred_element_type=F32)
            uv_ref[s] = jnp.dot(h, wv_ref[:, sl], preferred_element_type=F32)
        for s in range(n_sub):
            sl = slice(s * sub, (s + 1) * sub)
            cwg, cbg = cwg_ref[:, sl], cbg_ref[:, sl]
            cwv, cbv = 0.5 * cwv_ref[:, sl], 0.5 * cbv_ref[:, sl]
            for r0 in range(0, tm, nr):
                rows = slice(r0, r0 + nr)
                gate = conv(ug_ref.at[s], cwg, cbg, r0, nr)
                half_val = conv(uv_ref.at[s], cwv, cbv, r0, nr)
                act = _gated_gelu(gate, half_val).astype(BF16)
                d = jnp.dot(act, wd_ref[sl, :], preferred_element_type=F32)
                total = d if first and s == 0 else o_ref[rows, :] + d
                if finish and s == n_sub - 1:
                    o_ref[rows, :] = x_ref[rows, :] + _rms(total, gpost_ref[...])
                else:
                    o_ref[rows, :] = total

    last = nc - 1
    if nc == 1:
        step(True, True)
    else:
        pl.when(j == 0)(functools.partial(step, True, False))
        if nc > 2:
            pl.when(jnp.logical_and(j > 0, j < last))(functools.partial(step, False, False))
        pl.when(j == last)(functools.partial(step, False, True))


def _ffn(x, gpre, w_up, conv_w, conv_b, w_down, gpost, layer, tm, chunk):
    B, L, D = x.shape
    d_ff = w_down.shape[1]
    assert L % tm == 0 and tm % (FFN_ROW_BLOCKS * CONV_HALO) == 0
    assert d_ff % chunk == 0 and chunk % FFN_SUB_CHUNK == 0 and w_up.shape[1:] == (D, 2 * d_ff)
    nc = d_ff // chunk
    hb = tm // CONV_HALO
    n_hb = L // CONV_HALO
    const2 = lambda b, i, j: (0, 0)
    gate_col = lambda b, i, j: (0, j)
    val_col = lambda b, i, j: (0, nc + j)
    sub = min(FFN_SUB_CHUNK, chunk)
    kern = functools.partial(_ffn_kernel, tm=tm, sub=sub, nc=nc)
    return pl.pallas_call(
        kern,
        grid=(B, L // tm, nc),
        in_specs=[
            pl.BlockSpec((None, CONV_HALO, D), lambda b, i, j: (b, jnp.maximum(i * hb - 1, 0), 0)),
            pl.BlockSpec((None, tm, D), lambda b, i, j: (b, i, 0)),
            pl.BlockSpec((None, CONV_HALO, D),
                         lambda b, i, j: (b, jnp.minimum((i + 1) * hb, n_hb - 1), 0)),
            pl.BlockSpec((1, D), const2),
            pl.BlockSpec((None, D, chunk), lambda b, i, j: (layer, 0, j)),
            pl.BlockSpec((None, D, chunk), lambda b, i, j: (layer, 0, nc + j)),
            pl.BlockSpec((3, chunk), gate_col),
            pl.BlockSpec((3, chunk), val_col),
            pl.BlockSpec((1, chunk), gate_col),
            pl.BlockSpec((1, chunk), val_col),
            pl.BlockSpec((None, chunk, D), lambda b, i, j: (layer, j, 0)),
            pl.BlockSpec((1, D), const2),
        ],
        out_specs=pl.BlockSpec((None, tm, D), lambda b, i, j: (b, i, 0)),
        out_shape=jax.ShapeDtypeStruct((B, L, D), F32),
        scratch_shapes=[
            pltpu.VMEM((tm + 2 * CONV_HALO, D), BF16),
            pltpu.VMEM((chunk // sub, tm + 2 * CONV_HALO, sub), F32),
            pltpu.VMEM((chunk // sub, tm + 2 * CONV_HALO, sub), F32),
        ],
        compiler_params=_params("parallel", "parallel", "arbitrary"),
        name="ffn",
    )(x, x, x, gpre, w_up, w_up, conv_w, conv_w, conv_b, conv_b, w_down, gpost)


def _rope_tables(seq_len):
    quarter = HEAD_DIM // 4
    t = jnp.arange(seq_len, dtype=jnp.int32)
    row = (t // GRID_W).astype(F32)
    col = (t % GRID_W).astype(F32)
    inv_freq = 1.0 / (ROPE_THETA ** (jnp.arange(quarter, dtype=F32) / quarter))
    ang = jnp.concatenate([row[:, None] * inv_freq[None, :], col[:, None] * inv_freq[None, :]], axis=-1)
    cos, sin = jnp.cos(ang), jnp.sin(ang)
    return jnp.concatenate([cos, cos], axis=-1), jnp.concatenate([-sin, sin], axis=-1)


def _fourier_split(seq_len):
    l2 = 1 << (int(math.log2(seq_len)) // 2)
    return seq_len // l2, l2


def _trunk(x, layers):
    B, L, D = x.shape
    ts = _tiles(L)
    cc, ss = _rope_tables(L)
    l1, l2 = _fourier_split(L)
    for p in layers:
        u_pool, qt, k, vt, u_four = _in_proj(x, p["g_pre_mix"], p["w_in"], cc, ss, p["q_norm"],
                                             p["k_norm"], p["widths"], ts["tm_proj"])
        attn = _attention(qt, k, vt, ts["tq"])
        four = _fourier(u_four, p["fourier_w"], l1, l2)
        x = _mix_out(x, u_pool, attn, four, p["w_out"], p["pool_w"], p["pool_scale"],
                     p["g_post_mix"], ts["tm_mix"])
        x = _ffn(x, p["g_pre_ffn"], p["w_up"], p["conv_w"], p["conv_b"], p["w_down"],
                 p["g_post_ffn"], p["layer"], ts["tm_ffn"], ts["ff_chunk"])
    return x


def kernel(x_prompt, x_sample, g_pre_mix, g_post_mix, w_in, pool_w, pool_scale, q_norm, k_norm,
           fourier_w, w_out, g_pre_ffn, g_post_ffn, w_up, conv_w, conv_b, w_down):
    depth = w_in.shape[0]
    pool_width = pool_scale.shape[-1]
    four_width = fourier_w.shape[1] * fourier_w.shape[2]
    kv_width = N_KV_HEADS * HEAD_DIM
    q_width = w_in.shape[-1] - pool_width - four_width - 2 * kv_width
    w_up_b, w_down_b = w_up.astype(BF16), w_down.astype(BF16)
    layers = []
    for l in range(depth):
        layers.append(dict(
            layer=l, widths=(pool_width, q_width, kv_width, four_width),
            g_pre_mix=g_pre_mix[l][None, :], g_post_mix=g_post_mix[l][None, :],
            w_in=w_in[l].astype(BF16), pool_w=pool_w[l].astype(BF16),
            pool_scale=pool_scale[l][None, :], q_norm=q_norm[l][None, :], k_norm=k_norm[l][None, :],
            fourier_w=fourier_w[l].astype(BF16), w_out=w_out[l].astype(BF16),
            g_pre_ffn=g_pre_ffn[l][None, :], g_post_ffn=g_post_ffn[l][None, :],
            w_up=w_up_b, conv_w=conv_w[l], conv_b=conv_b[l][None, :], w_down=w_down_b))
    return _trunk(x_prompt, layers), _trunk(x_sample, layers)
```

```python
import functools
import math

import jax
import jax.numpy as jnp
import numpy as np
from jax import lax
from jax.experimental import pallas as pl
from jax.experimental.pallas import tpu as pltpu

F32 = jnp.float32
BF16 = jnp.bfloat16

NORM_EPS = 1e-6
GRID_W = 64
HEAD_DIM = 128
N_KV_HEADS = 2
GQA_GROUP = 4
POOL_WINDOWS = (2, 4, 8, 16)
POOL_HALO = 8
GROUP_CH = 128
N_GROUPS = 4
ROPE_THETA = 10000.0
CONV_HALO = 16
ATTN_CHUNKS_PER_TRIP = 4
FOURIER_S2_PER_STEP = 16
FFN_ROW_BLOCKS = 2
FFN_SUB_CHUNK = 256

V7X_VMEM_BYTES = 64 * 1024 * 1024
VMEM_LIMIT = V7X_VMEM_BYTES - 4 * 1024 * 1024


def _tiles(seq_len):
    return dict(
        tm_proj=min(512, seq_len),
        tm_mix=min(512, seq_len),
        tm_ffn=min(1024, seq_len),
        ff_chunk=512,
        tq=min(512, seq_len),
    )


def _params(*sem):
    return pltpu.CompilerParams(dimension_semantics=sem, vmem_limit_bytes=VMEM_LIMIT)


def _rms(x, g):
    return x * lax.rsqrt(jnp.mean(x * x, axis=-1, keepdims=True) + NORM_EPS) * g


def _in_proj_kernel(x_ref, g_ref, w_ref, cc_ref, ss_ref, qn_ref, kn_ref,
                    pool_ref, q_ref, k_ref, v_ref, four_ref, *, widths, scale):
    pool_w, q_w, kv_w, four_w = widths
    h = _rms(x_ref[...], g_ref[...]).astype(BF16)
    cc = cc_ref[...]
    ss = ss_ref[...]

    def seg(lo, width):
        return jnp.dot(h, w_ref[:, lo:lo + width], preferred_element_type=F32)

    def norm_rope(zh, gain, out_scale):
        y = zh * lax.rsqrt(jnp.mean(zh * zh, axis=-1, keepdims=True) + NORM_EPS) * gain
        y = y * cc + pltpu.roll(y, HEAD_DIM // 2, axis=1) * ss
        return y * out_scale

    pair = 2 * HEAD_DIM
    for p in range(q_w // pair):
        z = seg(pool_w + p * pair, pair)
        for s in range(2):
            zh = z[:, s * HEAD_DIM:(s + 1) * HEAD_DIM]
            q_ref[2 * p + s] = norm_rope(zh, qn_ref[...], scale).T.astype(BF16)
    z = seg(pool_w + q_w, kv_w)
    for s in range(kv_w // HEAD_DIM):
        zh = z[:, s * HEAD_DIM:(s + 1) * HEAD_DIM]
        k_ref[:, s * HEAD_DIM:(s + 1) * HEAD_DIM] = norm_rope(zh, kn_ref[...], 1.0).astype(BF16)
    z = seg(pool_w + q_w + kv_w, kv_w)
    for s in range(kv_w // HEAD_DIM):
        v_ref[s] = z[:, s * HEAD_DIM:(s + 1) * HEAD_DIM].T.astype(BF16)
    four_ref[...] = seg(pool_w + q_w + 2 * kv_w, four_w)
    pool_ref[...] = seg(0, pool_w)


def _in_proj(x, g, w, cc, ss, qn, kn, widths, tm):
    B, L, D = x.shape
    pool_w, q_w, kv_w, four_w = widths
    assert L % tm == 0 and w.shape == (D, sum(widths) + kv_w)
    nt = L // tm
    row = lambda b, i: (b, i, 0)
    const2 = lambda b, i: (0, 0)
    n_q = q_w // HEAD_DIM
    v_rows = HEAD_DIM
    kern = functools.partial(_in_proj_kernel, widths=widths, scale=HEAD_DIM ** -0.5 * math.log2(math.e))
    return pl.pallas_call(
        kern,
        grid=(B, nt),
        in_specs=[
            pl.BlockSpec((None, tm, D), row),
            pl.BlockSpec((1, D), const2),
            pl.BlockSpec(w.shape, const2),
            pl.BlockSpec((tm, HEAD_DIM), lambda b, i: (i, 0)),
            pl.BlockSpec((tm, HEAD_DIM), lambda b, i: (i, 0)),
            pl.BlockSpec((1, HEAD_DIM), const2),
            pl.BlockSpec((1, HEAD_DIM), const2),
        ],
        out_specs=[
            pl.BlockSpec((None, tm, pool_w), row),
            pl.BlockSpec((None, n_q, HEAD_DIM, tm), lambda b, i: (b, 0, 0, i)),
            pl.BlockSpec((None, tm, kv_w), row),
            pl.BlockSpec((None, N_KV_HEADS, None, v_rows, tm), lambda b, i: (b, 0, i, 0, 0)),
            pl.BlockSpec((None, tm, four_w), row),
        ],
        out_shape=[
            jax.ShapeDtypeStruct((B, L, pool_w), F32),
            jax.ShapeDtypeStruct((B, n_q, HEAD_DIM, L), BF16),
            jax.ShapeDtypeStruct((B, L, kv_w), BF16),
            jax.ShapeDtypeStruct((B, N_KV_HEADS, nt, v_rows, tm), BF16),
            jax.ShapeDtypeStruct((B, L, four_w), F32),
        ],
        compiler_params=_params("parallel", "parallel"),
        name="in_proj",
    )(x, g, w, cc, ss, qn, kn)


def _attn_kernel(qt_ref, k_ref, vt_ref, o_ref, m_ref, l_ref, acc_ref, sa_ref, sb_ref, ma_ref, mb_ref,
                 *, tk, nk, unroll):
    assert nk == 1 or (unroll % 2 == 0 and nk % unroll == 0)
    m_ref[...] = jnp.full(m_ref.shape, -jnp.inf, F32)
    l_ref[...] = jnp.zeros(l_ref.shape, F32)
    acc_ref[...] = jnp.zeros(acc_ref.shape, F32)

    def scores(c, g, s_ref, cm_ref):
        kc = k_ref[pl.ds(pl.multiple_of(c * tk, tk), tk), :]
        st = jnp.dot(kc, qt_ref[g], preferred_element_type=F32)
        s_ref[g] = st
        cm_ref[g] = jnp.max(st, axis=0, keepdims=True)

    def accumulate(c, g, s_ref, cm_ref):
        m_old = m_ref[g]
        m_new = jnp.maximum(m_old, cm_ref[g])
        alpha = jnp.exp2(m_old - m_new)
        p = jnp.exp2(s_ref[g] - m_new)
        l_ref[g] = alpha * l_ref[g] + jnp.sum(p, axis=0, keepdims=True)
        acc_ref[g] = alpha * acc_ref[g] + jnp.dot(vt_ref[c], p.astype(BF16),
                                                  preferred_element_type=F32)
        m_ref[g] = m_new

    def step(c_scores, s_next, m_next, c_acc, s_cur, m_cur):
        for g in range(GQA_GROUP):
            if c_scores is not None:
                scores(c_scores, g, s_next, m_next)
            if c_acc is not None:
                accumulate(c_acc, g, s_cur, m_cur)

    bufs = ((sa_ref, ma_ref), (sb_ref, mb_ref))
    step(0, *bufs[0], None, None, None)
    if nk > 1:
        def group(i, carry):
            for u in range(unroll):
                c = unroll * i + u
                step(c + 1, *bufs[(u + 1) % 2], c, *bufs[u % 2])
            return carry

        lax.fori_loop(0, nk // unroll - 1, group, 0)
        for c in range(nk - unroll, nk - 1):
            step(c + 1, *bufs[(c + 1) % 2], c, *bufs[c % 2])
    step(None, None, None, nk - 1, *bufs[(nk - 1) % 2])
    for g in range(GQA_GROUP):
        o = acc_ref[g] / l_ref[g]
        o_ref[:, g * HEAD_DIM:(g + 1) * HEAD_DIM] = o.T.astype(o_ref.dtype)


def _attention(qt, k, vt, tq):
    B, n_q, _, L = qt.shape
    _, _, nk, v_rows, tk = vt.shape
    assert L % tq == 0 and nk * tk == L and n_q == N_KV_HEADS * GQA_GROUP
    gw = GQA_GROUP * HEAD_DIM
    unroll = ATTN_CHUNKS_PER_TRIP if nk % ATTN_CHUNKS_PER_TRIP == 0 and nk >= 4 * ATTN_CHUNKS_PER_TRIP else 2
    kern = functools.partial(_attn_kernel, tk=tk, nk=nk, unroll=unroll)
    return pl.pallas_call(
        kern,
        grid=(B, N_KV_HEADS, L // tq),
        in_specs=[
            pl.BlockSpec((None, GQA_GROUP, HEAD_DIM, tq), lambda b, j, i: (b, j, 0, i)),
            pl.BlockSpec((None, L, HEAD_DIM), lambda b, j, i: (b, 0, j)),
            pl.BlockSpec((None, None, nk, v_rows, tk), lambda b, j, i: (b, j, 0, 0, 0)),
        ],
        out_specs=pl.BlockSpec((None, tq, gw), lambda b, j, i: (b, i, j)),
        out_shape=jax.ShapeDtypeStruct((B, L, n_q * HEAD_DIM), BF16),
        scratch_shapes=[
            pltpu.VMEM((GQA_GROUP, 1, tq), F32),
            pltpu.VMEM((GQA_GROUP, 1, tq), F32),
            pltpu.VMEM((GQA_GROUP, v_rows, tq), F32),
            pltpu.VMEM((GQA_GROUP, tk, tq), F32),
            pltpu.VMEM((GQA_GROUP, tk, tq), F32),
            pltpu.VMEM((GQA_GROUP, 1, tq), F32),
            pltpu.VMEM((GQA_GROUP, 1, tq), F32),
        ],
        compiler_params=_params("parallel", "parallel", "arbitrary"),
        name="attention",
    )(qt, k, vt)


def _dft_cos_sin(n):
    idx = np.arange(n)
    ang = 2.0 * np.pi * ((idx[:, None] * idx[None, :]) % n) / n
    return np.cos(ang), np.sin(ang)


def _fourier_a_kernel(x_ref, f1_ref, twc_ref, tws_ref, zr_ref, zi_ref, *, l2, k):
    xt = pltpu.einshape("mkw->kmw", x_ref[...])
    zr, zi = [], []
    for t in range(k):
        z = jnp.dot(f1_ref[...], xt[t].astype(BF16), preferred_element_type=F32)
        a = z[:l2]
        b = z[l2:]
        c = twc_ref[:, t:t + 1]
        s = tws_ref[:, t:t + 1]
        zr.append(a * c + b * s)
        zi.append(b * c - a * s)
    zr_ref[...] = pltpu.einshape("kmw->mkw", jnp.stack(zr)).astype(zr_ref.dtype)
    zi_ref[...] = pltpu.einshape("kmw->mkw", jnp.stack(zi)).astype(zi_ref.dtype)


def _fourier_b_kernel(zr_ref, zi_ref, f2_ref, fc_ref, fw_ref, o_ref, hr_ref, hi_ref, *, l1, n_s2, norm):
    for a in range(n_s2):
        rows = slice(a * l1, (a + 1) * l1)
        zz = jnp.concatenate([zr_ref[rows, :], zi_ref[rows, :]], axis=0)
        hh = jnp.dot(f2_ref[...], zz, preferred_element_type=F32)
        hr_ref[rows, :] = hh[:l1].astype(BF16)
        hi_ref[rows, :] = hh[l1:].astype(BF16)
    ys = []
    for g in range(N_GROUPS):
        sl = slice(g * GROUP_CH, (g + 1) * GROUP_CH)
        hg = jnp.concatenate([hr_ref[:, sl], hi_ref[:, sl]], axis=1)
        f = jnp.dot(hg, fc_ref[...], preferred_element_type=F32) * norm
        ys.append(jnp.dot(f.astype(BF16), fw_ref[g], preferred_element_type=F32))
    y = jnp.concatenate(ys, axis=1).reshape(n_s2, l1, o_ref.shape[-1])
    o_ref[...] = pltpu.einshape("alw->law", y).astype(o_ref.dtype)


def _fourier(u, fw, l1, l2):
    B, L, width = u.shape
    k = min(FOURIER_S2_PER_STEP, l1)
    n_s2 = min(FOURIER_S2_PER_STEP, l2)
    assert l1 * l2 == L and l1 % k == 0 and l2 % n_s2 == 0 and width == N_GROUPS * GROUP_CH
    c2, s2 = _dft_cos_sin(l2)
    f1 = jnp.asarray(np.concatenate([c2, -s2], axis=0), BF16)
    c1, s1 = _dft_cos_sin(l1)
    f2 = jnp.asarray(np.block([[c1, s1], [-s1, c1]]), BF16)
    cc, sc = _dft_cos_sin(GROUP_CH)
    fc = jnp.asarray(np.concatenate([cc, sc], axis=0), BF16)
    ang = 2.0 * np.pi * ((np.arange(l2)[:, None] * np.arange(l1)[None, :]) % L) / L
    tw = lambda t: jnp.asarray(t.reshape(l2, l1 // k, k).transpose(1, 0, 2), F32)
    twc, tws = tw(np.cos(ang)), tw(np.sin(ang))

    x = u.reshape(B, l2, l1, width)
    col = lambda b, j: (b, 0, j, 0)
    const2 = lambda b, j: (0, 0)
    zr, zi = pl.pallas_call(
        functools.partial(_fourier_a_kernel, l2=l2, k=k),
        grid=(B, l1 // k),
        in_specs=[
            pl.BlockSpec((None, l2, k, width), col),
            pl.BlockSpec(f1.shape, const2),
            pl.BlockSpec((None, l2, k), lambda b, j: (j, 0, 0)),
            pl.BlockSpec((None, l2, k), lambda b, j: (j, 0, 0)),
        ],
        out_specs=[pl.BlockSpec((None, l2, k, width), col)] * 2,
        out_shape=[jax.ShapeDtypeStruct((B, l2, l1, width), BF16)] * 2,
        compiler_params=_params("parallel", "parallel"),
        name="fourier_a",
    )(x, f1, twc, tws)

    zr = zr.reshape(B, L, width)
    zi = zi.reshape(B, L, width)
    rows = lambda b, j: (b, j, 0)
    out = pl.pallas_call(
        functools.partial(_fourier_b_kernel, l1=l1, n_s2=n_s2, norm=1.0 / math.sqrt(L * GROUP_CH)),
        grid=(B, l2 // n_s2),
        in_specs=[
            pl.BlockSpec((None, n_s2 * l1, width), rows),
            pl.BlockSpec((None, n_s2 * l1, width), rows),
            pl.BlockSpec(f2.shape, const2),
            pl.BlockSpec(fc.shape, const2),
            pl.BlockSpec(fw.shape, lambda b, j: (0, 0, 0)),
        ],
        out_specs=pl.BlockSpec((None, l1, n_s2, width), lambda b, j: (b, 0, j, 0)),
        out_shape=jax.ShapeDtypeStruct((B, l1, l2, width), BF16),
        scratch_shapes=[pltpu.VMEM((n_s2 * l1, width), BF16)] * 2,
        compiler_params=_params("parallel", "parallel"),
        name="fourier_b",
    )(zr, zi, f2, fc, fw)
    return out.reshape(B, L, width)


def _mix_out_kernel(up_ref, uc_ref, un_ref, attn_ref, four_ref, x_ref, wo_ref, pw_ref, ps_ref,
                    g_ref, o_ref, ext_ref, v_ref, m_ref, pooled_ref, *, tm, seq_len, widths):
    pool_w, attn_w, four_w = widths
    i = pl.program_id(1)
    last = pl.num_programs(1) - 1
    ext_ref[0:POOL_HALO, :] = jnp.where(i > 0, up_ref[...], 0.0)
    ext_ref[POOL_HALO:POOL_HALO + tm, :] = uc_ref[...]
    ext_ref[POOL_HALO + tm:, :] = jnp.where(i < last, un_ref[...], 0.0)

    halves = [slice(r * (tm // 2), (r + 1) * (tm // 2)) for r in range(2)]
    t = i * tm + lax.broadcasted_iota(jnp.int32, (tm, 1), 0)
    for g in range(len(POOL_WINDOWS)):
        sl = slice(g * GROUP_CH, (g + 1) * GROUP_CH)
        ug = ext_ref[:, sl]
        hi = ug.astype(BF16)
        lo = (ug - hi.astype(F32)).astype(BF16)
        pw2 = jnp.concatenate([pw_ref[g], pw_ref[g]], axis=0)
        v_ref[:, sl] = jnp.dot(jnp.concatenate([hi, lo], axis=1), pw2, preferred_element_type=F32)

    for rows in halves:
        m_ref[rows, :] = (
            jnp.dot(attn_ref[rows, :], wo_ref[pool_w:pool_w + attn_w, :], preferred_element_type=F32)
            + jnp.dot(four_ref[rows, :], wo_ref[pool_w + attn_w:, :],
                      preferred_element_type=F32))

    for g, w in enumerate(POOL_WINDOWS):
        sl = slice(g * GROUP_CH, (g + 1) * GROUP_CH)
        lo = jnp.maximum(t - w // 2, 0)
        hi = jnp.minimum(t + (w - 1 - w // 2), seq_len - 1)
        cnt = (hi - lo + 1).astype(F32)
        win = v_ref[POOL_HALO - w // 2:POOL_HALO - w // 2 + tm, sl]
        for d in range(1 - w // 2, w - w // 2):
            win = win + v_ref[POOL_HALO + d:POOL_HALO + d + tm, sl]
        yg = (win / cnt - v_ref[POOL_HALO:POOL_HALO + tm, sl]) * ps_ref[:, sl]
        pooled_ref[:, sl] = yg.astype(BF16)
    for rows in halves:
        m = m_ref[rows, :] + jnp.dot(pooled_ref[rows, :], wo_ref[0:pool_w, :],
                                     preferred_element_type=F32)
        o_ref[rows, :] = x_ref[rows, :] + _rms(m, g_ref[...])


def _mix_out(x, u_pool, attn, four, wo, pw, ps, g, tm):
    B, L, D = x.shape
    widths = (u_pool.shape[-1], attn.shape[-1], four.shape[-1])
    assert L % tm == 0 and tm % (2 * POOL_HALO) == 0 and wo.shape == (sum(widths), D)
    assert widths[0] == len(POOL_WINDOWS) * GROUP_CH
    hb = tm // POOL_HALO
    n_hb = L // POOL_HALO
    row = lambda b, i: (b, i, 0)
    const2 = lambda b, i: (0, 0)
    kern = functools.partial(_mix_out_kernel, tm=tm, seq_len=L, widths=widths)
    return pl.pallas_call(
        kern,
        grid=(B, L // tm),
        in_specs=[
            pl.BlockSpec((None, POOL_HALO, widths[0]), lambda b, i: (b, jnp.maximum(i * hb - 1, 0), 0)),
            pl.BlockSpec((None, tm, widths[0]), row),
            pl.BlockSpec((None, POOL_HALO, widths[0]),
                         lambda b, i: (b, jnp.minimum((i + 1) * hb, n_hb - 1), 0)),
            pl.BlockSpec((None, tm, widths[1]), row),
            pl.BlockSpec((None, tm, widths[2]), row),
            pl.BlockSpec((None, tm, D), row),
            pl.BlockSpec(wo.shape, const2),
            pl.BlockSpec(pw.shape, lambda b, i: (0, 0, 0)),
            pl.BlockSpec((1, widths[0]), const2),
            pl.BlockSpec((1, D), const2),
        ],
        out_specs=pl.BlockSpec((None, tm, D), row),
        out_shape=jax.ShapeDtypeStruct((B, L, D), F32),
        scratch_shapes=[
            pltpu.VMEM((tm + 2 * POOL_HALO, widths[0]), F32),
            pltpu.VMEM((tm + 2 * POOL_HALO, widths[0]), F32),
            pltpu.VMEM((tm, D), F32),
            pltpu.VMEM((tm, widths[0]), BF16),
        ],
        compiler_params=_params("parallel", "parallel"),
        name="mix_out",
    )(u_pool, u_pool, u_pool, attn, four, x, wo, pw, ps, g)


def _gated_gelu(gate, half_val):
    c0 = math.sqrt(2.0 / math.pi)
    inner = gate * (gate * gate * (c0 * 0.044715) + c0)
    return gate * (1.0 + jnp.tanh(inner)) * half_val


def _ffn_kernel(xp_ref, x_ref, xn_ref, gpre_ref, wg_ref, wv_ref, cp_ref,
                wd_ref, gpost_ref, o_ref, h_ref, ug_ref, uv_ref, *, tm, sub, nc):
    i = pl.program_id(1)
    j = pl.program_id(2)
    rows = tm + 2 * CONV_HALO

    @pl.when(j == 0)
    def _():
        g = gpre_ref[...]
        hp = jnp.where(i > 0, _rms(xp_ref[...], g), 0.0)
        hn = jnp.where(i < pl.num_programs(1) - 1, _rms(xn_ref[...], g), 0.0)
        h_ref[0:CONV_HALO, :] = hp.astype(BF16)
        h_ref[CONV_HALO:CONV_HALO + tm, :] = _rms(x_ref[...], g).astype(BF16)
        h_ref[CONV_HALO + tm:, :] = hn.astype(BF16)

    def conv(u_ref, cw, cb, r0, nr):
        prev = u_ref[CONV_HALO - 1 + r0:CONV_HALO - 1 + r0 + nr, :]
        cur = u_ref[CONV_HALO + r0:CONV_HALO + r0 + nr, :]
        nxt = u_ref[CONV_HALO + 1 + r0:CONV_HALO + 1 + r0 + nr, :]
        return prev * cw[0:1, :] + cur * cw[1:2, :] + nxt * cw[2:3, :] + cb

    def step(first, finish):
        h = h_ref[...]
        n_sub = wg_ref.shape[1] // sub
        nr = tm // FFN_ROW_BLOCKS
        for s in range(n_sub):
            sl = slice(s * sub, (s + 1) * sub)
            ug_ref[s] = jnp.dot(h, wg_ref[:, sl], preferred_element_type=F32)
            uv_ref[s] = jnp.dot(h, wv_ref[:, sl], preferred_element_type=F32)
        for s in range(n_sub):
            sl = slice(s * sub, (s + 1) * sub)
            cwg, cbg = cp_ref[0:3, sl], cp_ref[3:4, sl]
            cwv, cbv = 0.5 * cp_ref[4:7, sl], 0.5 * cp_ref[7:8, sl]
            for r0 in range(0, tm, nr):
                rows = slice(r0, r0 + nr)
                gate = conv(ug_ref.at[s], cwg, cbg, r0, nr)
                half_val = conv(uv_ref.at[s], cwv, cbv, r0, nr)
                act = _gated_gelu(gate, half_val).astype(BF16)
                d = jnp.dot(act, wd_ref[sl, :], preferred_element_type=F32)
                total = d if first and s == 0 else o_ref[rows, :] + d
                if finish and s == n_sub - 1:
                    o_ref[rows, :] = x_ref[rows, :] + _rms(total, gpost_ref[...])
                else:
                    o_ref[rows, :] = total

    last = nc - 1
    if nc == 1:
        step(True, True)
    else:
        pl.when(j == 0)(functools.partial(step, True, False))
        if nc > 2:
            pl.when(jnp.logical_and(j > 0, j < last))(functools.partial(step, False, False))
        pl.when(j == last)(functools.partial(step, False, True))


def _ffn(x, gpre, w_up, conv_w, conv_b, w_down, gpost, layer, tm, chunk):
    B, L, D = x.shape
    d_ff = w_down.shape[1]
    assert L % tm == 0 and tm % (FFN_ROW_BLOCKS * CONV_HALO) == 0
    assert d_ff % chunk == 0 and chunk % FFN_SUB_CHUNK == 0 and w_up.shape[1:] == (D, 2 * d_ff)
    nc = d_ff // chunk
    hb = tm // CONV_HALO
    n_hb = L // CONV_HALO
    const2 = lambda b, i, j: (0, 0)
    taps = conv_w.reshape(3, 2, nc, chunk)
    bias = conv_b.reshape(1, 2, nc, chunk)
    conv_p = jnp.concatenate([taps[:, 0], bias[:, 0], taps[:, 1], bias[:, 1]], axis=0).transpose(1, 0, 2)
    sub = min(FFN_SUB_CHUNK, chunk)
    kern = functools.partial(_ffn_kernel, tm=tm, sub=sub, nc=nc)
    return pl.pallas_call(
        kern,
        grid=(B, L // tm, nc),
        in_specs=[
            pl.BlockSpec((None, CONV_HALO, D), lambda b, i, j: (b, jnp.maximum(i * hb - 1, 0), 0)),
            pl.BlockSpec((None, tm, D), lambda b, i, j: (b, i, 0)),
            pl.BlockSpec((None, CONV_HALO, D),
                         lambda b, i, j: (b, jnp.minimum((i + 1) * hb, n_hb - 1), 0)),
            pl.BlockSpec((1, D), const2),
            pl.BlockSpec((None, D, chunk), lambda b, i, j: (layer, 0, j)),
            pl.BlockSpec((None, D, chunk), lambda b, i, j: (layer, 0, nc + j)),
            pl.BlockSpec((None, 8, chunk), lambda b, i, j: (j, 0, 0)),
            pl.BlockSpec((None, chunk, D), lambda b, i, j: (layer, j, 0)),
            pl.BlockSpec((1, D), const2),
        ],
        out_specs=pl.BlockSpec((None, tm, D), lambda b, i, j: (b, i, 0)),
        out_shape=jax.ShapeDtypeStruct((B, L, D), F32),
        scratch_shapes=[
            pltpu.VMEM((tm + 2 * CONV_HALO, D), BF16),
            pltpu.VMEM((chunk // sub, tm + 2 * CONV_HALO, sub), F32),
            pltpu.VMEM((chunk // sub, tm + 2 * CONV_HALO, sub), F32),
        ],
        compiler_params=_params("parallel", "parallel", "arbitrary"),
        name="ffn",
    )(x, x, x, gpre, w_up, w_up, conv_p, w_down, gpost)


def _rope_tables(seq_len):
    quarter = HEAD_DIM // 4
    t = jnp.arange(seq_len, dtype=jnp.int32)
    row = (t // GRID_W).astype(F32)
    col = (t % GRID_W).astype(F32)
    inv_freq = 1.0 / (ROPE_THETA ** (jnp.arange(quarter, dtype=F32) / quarter))
    ang = jnp.concatenate([row[:, None] * inv_freq[None, :], col[:, None] * inv_freq[None, :]], axis=-1)
    cos, sin = jnp.cos(ang), jnp.sin(ang)
    return jnp.concatenate([cos, cos], axis=-1), jnp.concatenate([-sin, sin], axis=-1)


def _fourier_split(seq_len):
    l2 = 1 << (int(math.log2(seq_len)) // 2)
    return seq_len // l2, l2


def _trunk(x, layers):
    B, L, D = x.shape
    ts = _tiles(L)
    cc, ss = _rope_tables(L)
    l1, l2 = _fourier_split(L)
    for p in layers:
        u_pool, qt, k, vt, u_four = _in_proj(x, p["g_pre_mix"], p["w_in"], cc, ss, p["q_norm"],
                                             p["k_norm"], p["widths"], ts["tm_proj"])
        attn = _attention(qt, k, vt, ts["tq"])
        four = _fourier(u_four, p["fourier_w"], l1, l2)
        x = _mix_out(x, u_pool, attn, four, p["w_out"], p["pool_w"], p["pool_scale"],
                     p["g_post_mix"], ts["tm_mix"])
        x = _ffn(x, p["g_pre_ffn"], p["w_up"], p["conv_w"], p["conv_b"], p["w_down"],
                 p["g_post_ffn"], p["layer"], ts["tm_ffn"], ts["ff_chunk"])
    return x


def kernel(x_prompt, x_sample, g_pre_mix, g_post_mix, w_in, pool_w, pool_scale, q_norm, k_norm,
           fourier_w, w_out, g_pre_ffn, g_post_ffn, w_up, conv_w, conv_b, w_down):
    depth = w_in.shape[0]
    pool_width = pool_scale.shape[-1]
    four_width = fourier_w.shape[1] * fourier_w.shape[2]
    kv_width = N_KV_HEADS * HEAD_DIM
    q_width = w_in.shape[-1] - pool_width - four_width - 2 * kv_width
    w_up_b, w_down_b = w_up.astype(BF16), w_down.astype(BF16)
    layers = []
    for l in range(depth):
        layers.append(dict(
            layer=l, widths=(pool_width, q_width, kv_width, four_width),
            g_pre_mix=g_pre_mix[l][None, :], g_post_mix=g_post_mix[l][None, :],
            w_in=w_in[l].astype(BF16), pool_w=pool_w[l].astype(BF16),
            pool_scale=pool_scale[l][None, :], q_norm=q_norm[l][None, :], k_norm=k_norm[l][None, :],
            fourier_w=fourier_w[l].astype(BF16), w_out=w_out[l].astype(BF16),
            g_pre_ffn=g_pre_ffn[l][None, :], g_post_ffn=g_post_ffn[l][None, :],
            w_up=w_up_b, conv_w=conv_w[l], conv_b=conv_b[l][None, :], w_down=w_down_b))
    return _trunk(x_prompt, layers), _trunk(x_sample, layers)
```

```python
import functools
import math

import jax
import jax.numpy as jnp
import numpy as np
from jax import lax
from jax.experimental import pallas as pl
from jax.experimental.pallas import tpu as pltpu

F32 = jnp.float32
BF16 = jnp.bfloat16

NORM_EPS = 1e-6
GRID_W = 64
HEAD_DIM = 128
N_KV_HEADS = 2
GQA_GROUP = 4
POOL_WINDOWS = (2, 4, 8, 16)
POOL_HALO = 8
GROUP_CH = 128
N_GROUPS = 4
ROPE_THETA = 10000.0
CONV_HALO = 16
ATTN_CHUNKS_PER_TRIP = 4
FOURIER_S2_PER_STEP = 16
FFN_ROW_BLOCKS = 2
FFN_SUB_CHUNK = 256

V7X_VMEM_BYTES = 64 * 1024 * 1024
VMEM_LIMIT = V7X_VMEM_BYTES - 4 * 1024 * 1024


def _tiles(seq_len):
    return dict(
        tm_proj=min(512, seq_len),
        tm_mix=min(512, seq_len),
        tm_ffn=min(1024, seq_len),
        ff_chunk=512,
        tq=min(512, seq_len),
    )


def _params(*sem):
    return pltpu.CompilerParams(dimension_semantics=sem, vmem_limit_bytes=VMEM_LIMIT)


def _rms(x, g):
    return x * lax.rsqrt(jnp.mean(x * x, axis=-1, keepdims=True) + NORM_EPS) * g


def _in_proj_kernel(x_ref, g_ref, w_ref, cc_ref, ss_ref, qn_ref, kn_ref,
                    pool_ref, q_ref, k_ref, v_ref, four_ref, *, widths, scale):
    pool_w, q_w, kv_w, four_w = widths
    h = _rms(x_ref[...], g_ref[...]).astype(BF16)
    cc = cc_ref[...]
    ss = ss_ref[...]

    def seg(lo, width):
        return jnp.dot(h, w_ref[:, lo:lo + width], preferred_element_type=F32)

    def norm_rope(zh, gain, out_scale):
        y = zh * lax.rsqrt(jnp.mean(zh * zh, axis=-1, keepdims=True) + NORM_EPS) * gain
        y = y * cc + pltpu.roll(y, HEAD_DIM // 2, axis=1) * ss
        return y * out_scale

    pair = 2 * HEAD_DIM
    for p in range(q_w // pair):
        z = seg(pool_w + p * pair, pair)
        for s in range(2):
            zh = z[:, s * HEAD_DIM:(s + 1) * HEAD_DIM]
            q_ref[2 * p + s] = norm_rope(zh, qn_ref[...], scale).T.astype(BF16)
    z = seg(pool_w + q_w, kv_w)
    for s in range(kv_w // HEAD_DIM):
        zh = z[:, s * HEAD_DIM:(s + 1) * HEAD_DIM]
        k_ref[:, s * HEAD_DIM:(s + 1) * HEAD_DIM] = norm_rope(zh, kn_ref[...], 1.0).astype(BF16)
    z = seg(pool_w + q_w + kv_w, kv_w)
    for s in range(kv_w // HEAD_DIM):
        v_ref[s] = z[:, s * HEAD_DIM:(s + 1) * HEAD_DIM].T.astype(BF16)
    four_ref[...] = seg(pool_w + q_w + 2 * kv_w, four_w)
    pool_ref[...] = seg(0, pool_w)


def _in_proj(x, g, w, cc, ss, qn, kn, widths, tm):
    B, L, D = x.shape
    pool_w, q_w, kv_w, four_w = widths
    assert L % tm == 0 and w.shape == (D, sum(widths) + kv_w)
    nt = L // tm
    row = lambda b, i: (b, i, 0)
    const2 = lambda b, i: (0, 0)
    n_q = q_w // HEAD_DIM
    v_rows = HEAD_DIM
    kern = functools.partial(_in_proj_kernel, widths=widths, scale=HEAD_DIM ** -0.5 * math.log2(math.e))
    return pl.pallas_call(
        kern,
        grid=(B, nt),
        in_specs=[
            pl.BlockSpec((None, tm, D), row),
            pl.BlockSpec((1, D), const2),
            pl.BlockSpec(w.shape, const2),
            pl.BlockSpec((tm, HEAD_DIM), lambda b, i: (i, 0)),
            pl.BlockSpec((tm, HEAD_DIM), lambda b, i: (i, 0)),
            pl.BlockSpec((1, HEAD_DIM), const2),
            pl.BlockSpec((1, HEAD_DIM), const2),
        ],
        out_specs=[
            pl.BlockSpec((None, tm, pool_w), row),
            pl.BlockSpec((None, n_q, HEAD_DIM, tm), lambda b, i: (b, 0, 0, i)),
            pl.BlockSpec((None, tm, kv_w), row),
            pl.BlockSpec((None, N_KV_HEADS, None, v_rows, tm), lambda b, i: (b, 0, i, 0, 0)),
            pl.BlockSpec((None, tm, four_w), row),
        ],
        out_shape=[
            jax.ShapeDtypeStruct((B, L, pool_w), F32),
            jax.ShapeDtypeStruct((B, n_q, HEAD_DIM, L), BF16),
            jax.ShapeDtypeStruct((B, L, kv_w), BF16),
            jax.ShapeDtypeStruct((B, N_KV_HEADS, nt, v_rows, tm), BF16),
            jax.ShapeDtypeStruct((B, L, four_w), F32),
        ],
        compiler_params=_params("parallel", "parallel"),
        name="in_proj",
    )(x, g, w, cc, ss, qn, kn)


def _attn_kernel(qt_ref, qn_ref, k_ref, vt_ref, o_ref, m_ref, l_ref, acc_ref, sa_ref, sb_ref, ma_ref,
                 mb_ref, *, tk, nk, unroll):
    assert nk == 1 or (unroll % 2 == 0 and nk % unroll == 0)
    i = pl.program_id(2)
    m_ref[...] = jnp.full(m_ref.shape, -jnp.inf, F32)
    l_ref[...] = jnp.zeros(l_ref.shape, F32)
    acc_ref[...] = jnp.zeros(acc_ref.shape, F32)

    def scores(c, g, s_ref, cm_ref, q_ref=qt_ref):
        kc = k_ref[pl.ds(pl.multiple_of(c * tk, tk), tk), :]
        st = jnp.dot(kc, q_ref[g], preferred_element_type=F32)
        s_ref[g] = st
        cm_ref[g] = jnp.max(st, axis=0, keepdims=True)

    def accumulate(c, g, s_ref, cm_ref):
        m_old = m_ref[g]
        m_new = jnp.maximum(m_old, cm_ref[g])
        alpha = jnp.exp2(m_old - m_new)
        p = jnp.exp2(s_ref[g] - m_new)
        l_ref[g] = alpha * l_ref[g] + jnp.sum(p, axis=0, keepdims=True)
        acc_ref[g] = alpha * acc_ref[g] + jnp.dot(vt_ref[c], p.astype(BF16),
                                                  preferred_element_type=F32)
        m_ref[g] = m_new

    def step(c_scores, s_next, m_next, c_acc, s_cur, m_cur):
        for g in range(GQA_GROUP):
            if c_scores is not None:
                scores(c_scores, g, s_next, m_next)
            if c_acc is not None:
                accumulate(c_acc, g, s_cur, m_cur)

    bufs = ((sa_ref, ma_ref), (sb_ref, mb_ref))
    pl.when(i == 0)(functools.partial(step, 0, *bufs[0], None, None, None))
    if nk > 1:
        def group(i, carry):
            for u in range(unroll):
                c = unroll * i + u
                step(c + 1, *bufs[(u + 1) % 2], c, *bufs[u % 2])
            return carry

        lax.fori_loop(0, nk // unroll - 1, group, 0)
        for c in range(nk - unroll, nk - 1):
            step(c + 1, *bufs[(c + 1) % 2], c, *bufs[c % 2])
    def finish(with_next):
        for g in range(GQA_GROUP):
            accumulate(nk - 1, g, *bufs[(nk - 1) % 2])
            if with_next:
                scores(0, g, *bufs[0], q_ref=qn_ref)
        for g in range(GQA_GROUP):
            o = acc_ref[g] / l_ref[g]
            o_ref[:, g * HEAD_DIM:(g + 1) * HEAD_DIM] = o.T.astype(o_ref.dtype)

    last_i = pl.num_programs(2) - 1
    pl.when(i < last_i)(functools.partial(finish, True))
    pl.when(i == last_i)(functools.partial(finish, False))


def _attention(qt, k, vt, tq):
    B, n_q, _, L = qt.shape
    _, _, nk, v_rows, tk = vt.shape
    assert L % tq == 0 and nk * tk == L and n_q == N_KV_HEADS * GQA_GROUP
    gw = GQA_GROUP * HEAD_DIM
    unroll = ATTN_CHUNKS_PER_TRIP if nk % ATTN_CHUNKS_PER_TRIP == 0 and nk >= 4 * ATTN_CHUNKS_PER_TRIP else 2
    kern = functools.partial(_attn_kernel, tk=tk, nk=nk, unroll=unroll)
    return pl.pallas_call(
        kern,
        grid=(B, N_KV_HEADS, L // tq),
        in_specs=[
            pl.BlockSpec((None, GQA_GROUP, HEAD_DIM, tq), lambda b, j, i: (b, j, 0, i)),
            pl.BlockSpec((None, GQA_GROUP, HEAD_DIM, tq),
                         lambda b, j, i: (b, j, 0, jnp.minimum(i + 1, L // tq - 1))),
            pl.BlockSpec((None, L, HEAD_DIM), lambda b, j, i: (b, 0, j)),
            pl.BlockSpec((None, None, nk, v_rows, tk), lambda b, j, i: (b, j, 0, 0, 0)),
        ],
        out_specs=pl.BlockSpec((None, tq, gw), lambda b, j, i: (b, i, j)),
        out_shape=jax.ShapeDtypeStruct((B, L, n_q * HEAD_DIM), BF16),
        scratch_shapes=[
            pltpu.VMEM((GQA_GROUP, 1, tq), F32),
            pltpu.VMEM((GQA_GROUP, 1, tq), F32),
            pltpu.VMEM((GQA_GROUP, v_rows, tq), F32),
            pltpu.VMEM((GQA_GROUP, tk, tq), F32),
            pltpu.VMEM((GQA_GROUP, tk, tq), F32),
            pltpu.VMEM((GQA_GROUP, 1, tq), F32),
            pltpu.VMEM((GQA_GROUP, 1, tq), F32),
        ],
        compiler_params=_params("parallel", "parallel", "arbitrary"),
        name="attention",
    )(qt, qt, k, vt)


def _dft_cos_sin(n):
    idx = np.arange(n)
    ang = 2.0 * np.pi * ((idx[:, None] * idx[None, :]) % n) / n
    return np.cos(ang), np.sin(ang)


def _fourier_a_kernel(x_ref, f1_ref, twc_ref, tws_ref, zr_ref, zi_ref, *, l2, k):
    xt = pltpu.einshape("mkw->kmw", x_ref[...])
    zr, zi = [], []
    for t in range(k):
        z = jnp.dot(f1_ref[...], xt[t].astype(BF16), preferred_element_type=F32)
        a = z[:l2]
        b = z[l2:]
        c = twc_ref[:, t:t + 1]
        s = tws_ref[:, t:t + 1]
        zr.append(a * c + b * s)
        zi.append(b * c - a * s)
    zr_ref[...] = pltpu.einshape("kmw->mkw", jnp.stack(zr)).astype(zr_ref.dtype)
    zi_ref[...] = pltpu.einshape("kmw->mkw", jnp.stack(zi)).astype(zi_ref.dtype)


def _fourier_b_kernel(zr_ref, zi_ref, f2_ref, fc_ref, fw_ref, o_ref, hr_ref, hi_ref, *, l1, n_s2, norm):
    for a in range(n_s2):
        rows = slice(a * l1, (a + 1) * l1)
        zz = jnp.concatenate([zr_ref[rows, :], zi_ref[rows, :]], axis=0)
        hh = jnp.dot(f2_ref[...], zz, preferred_element_type=F32)
        hr_ref[rows, :] = hh[:l1].astype(BF16)
        hi_ref[rows, :] = hh[l1:].astype(BF16)
    ys = []
    for g in range(N_GROUPS):
        sl = slice(g * GROUP_CH, (g + 1) * GROUP_CH)
        hg = jnp.concatenate([hr_ref[:, sl], hi_ref[:, sl]], axis=1)
        f = jnp.dot(hg, fc_ref[...], preferred_element_type=F32) * norm
        ys.append(jnp.dot(f.astype(BF16), fw_ref[g], preferred_element_type=F32))
    y = jnp.concatenate(ys, axis=1).reshape(n_s2, l1, o_ref.shape[-1])
    o_ref[...] = pltpu.einshape("alw->law", y).astype(o_ref.dtype)


def _fourier(u, fw, l1, l2):
    B, L, width = u.shape
    k = min(FOURIER_S2_PER_STEP, l1)
    n_s2 = min(FOURIER_S2_PER_STEP, l2)
    assert l1 * l2 == L and l1 % k == 0 and l2 % n_s2 == 0 and width == N_GROUPS * GROUP_CH
    c2, s2 = _dft_cos_sin(l2)
    f1 = jnp.asarray(np.concatenate([c2, -s2], axis=0), BF16)
    c1, s1 = _dft_cos_sin(l1)
    f2 = jnp.asarray(np.block([[c1, s1], [-s1, c1]]), BF16)
    cc, sc = _dft_cos_sin(GROUP_CH)
    fc = jnp.asarray(np.concatenate([cc, sc], axis=0), BF16)
    ang = 2.0 * np.pi * ((np.arange(l2)[:, None] * np.arange(l1)[None, :]) % L) / L
    tw = lambda t: jnp.asarray(t.reshape(l2, l1 // k, k).transpose(1, 0, 2), F32)
    twc, tws = tw(np.cos(ang)), tw(np.sin(ang))

    x = u.reshape(B, l2, l1, width)
    col = lambda b, j: (b, 0, j, 0)
    const2 = lambda b, j: (0, 0)
    zr, zi = pl.pallas_call(
        functools.partial(_fourier_a_kernel, l2=l2, k=k),
        grid=(B, l1 // k),
        in_specs=[
            pl.BlockSpec((None, l2, k, width), col),
            pl.BlockSpec(f1.shape, const2),
            pl.BlockSpec((None, l2, k), lambda b, j: (j, 0, 0)),
            pl.BlockSpec((None, l2, k), lambda b, j: (j, 0, 0)),
        ],
        out_specs=[pl.BlockSpec((None, l2, k, width), col)] * 2,
        out_shape=[jax.ShapeDtypeStruct((B, l2, l1, width), BF16)] * 2,
        compiler_params=_params("parallel", "parallel"),
        name="fourier_a",
    )(x, f1, twc, tws)

    zr = zr.reshape(B, L, width)
    zi = zi.reshape(B, L, width)
    rows = lambda b, j: (b, j, 0)
    out = pl.pallas_call(
        functools.partial(_fourier_b_kernel, l1=l1, n_s2=n_s2, norm=1.0 / math.sqrt(L * GROUP_CH)),
        grid=(B, l2 // n_s2),
        in_specs=[
            pl.BlockSpec((None, n_s2 * l1, width), rows),
            pl.BlockSpec((None, n_s2 * l1, width), rows),
            pl.BlockSpec(f2.shape, const2),
            pl.BlockSpec(fc.shape, const2),
            pl.BlockSpec(fw.shape, lambda b, j: (0, 0, 0)),
        ],
        out_specs=pl.BlockSpec((None, l1, n_s2, width), lambda b, j: (b, 0, j, 0)),
        out_shape=jax.ShapeDtypeStruct((B, l1, l2, width), BF16),
        scratch_shapes=[pltpu.VMEM((n_s2 * l1, width), BF16)] * 2,
        compiler_params=_params("parallel", "parallel"),
        name="fourier_b",
    )(zr, zi, f2, fc, fw)
    return out.reshape(B, L, width)


def _mix_out_kernel(up_ref, uc_ref, un_ref, attn_ref, four_ref, x_ref, wo_ref, pw_ref, ps_ref,
                    g_ref, o_ref, ext_ref, v_ref, m_ref, pooled_ref, *, tm, seq_len, widths):
    pool_w, attn_w, four_w = widths
    i = pl.program_id(1)
    last = pl.num_programs(1) - 1
    ext_ref[0:POOL_HALO, :] = jnp.where(i > 0, up_ref[...], 0.0)
    ext_ref[POOL_HALO:POOL_HALO + tm, :] = uc_ref[...]
    ext_ref[POOL_HALO + tm:, :] = jnp.where(i < last, un_ref[...], 0.0)

    halves = [slice(r * (tm // 2), (r + 1) * (tm // 2)) for r in range(2)]
    t = i * tm + lax.broadcasted_iota(jnp.int32, (tm, 1), 0)
    for g in range(len(POOL_WINDOWS)):
        sl = slice(g * GROUP_CH, (g + 1) * GROUP_CH)
        ug = ext_ref[:, sl]
        hi = ug.astype(BF16)
        lo = (ug - hi.astype(F32)).astype(BF16)
        pw2 = jnp.concatenate([pw_ref[g], pw_ref[g]], axis=0)
        v_ref[:, sl] = jnp.dot(jnp.concatenate([hi, lo], axis=1), pw2, preferred_element_type=F32)

    for rows in halves:
        m_ref[rows, :] = (
            jnp.dot(attn_ref[rows, :], wo_ref[pool_w:pool_w + attn_w, :], preferred_element_type=F32)
            + jnp.dot(four_ref[rows, :], wo_ref[pool_w + attn_w:, :],
                      preferred_element_type=F32))

    for g, w in enumerate(POOL_WINDOWS):
        sl = slice(g * GROUP_CH, (g + 1) * GROUP_CH)
        lo = jnp.maximum(t - w // 2, 0)
        hi = jnp.minimum(t + (w - 1 - w // 2), seq_len - 1)
        cnt = (hi - lo + 1).astype(F32)
        win = v_ref[POOL_HALO - w // 2:POOL_HALO - w // 2 + tm, sl]
        for d in range(1 - w // 2, w - w // 2):
            win = win + v_ref[POOL_HALO + d:POOL_HALO + d + tm, sl]
        yg = (win / cnt - v_ref[POOL_HALO:POOL_HALO + tm, sl]) * ps_ref[:, sl]
        pooled_ref[:, sl] = yg.astype(BF16)
    for rows in halves:
        m = m_ref[rows, :] + jnp.dot(pooled_ref[rows, :], wo_ref[0:pool_w, :],
                                     preferred_element_type=F32)
        o_ref[rows, :] = x_ref[rows, :] + _rms(m, g_ref[...])


def _mix_out(x, u_pool, attn, four, wo, pw, ps, g, tm):
    B, L, D = x.shape
    widths = (u_pool.shape[-1], attn.shape[-1], four.shape[-1])
    assert L % tm == 0 and tm % (2 * POOL_HALO) == 0 and wo.shape == (sum(widths), D)
    assert widths[0] == len(POOL_WINDOWS) * GROUP_CH
    hb = tm // POOL_HALO
    n_hb = L // POOL_HALO
    row = lambda b, i: (b, i, 0)
    const2 = lambda b, i: (0, 0)
    kern = functools.partial(_mix_out_kernel, tm=tm, seq_len=L, widths=widths)
    return pl.pallas_call(
        kern,
        grid=(B, L // tm),
        in_specs=[
            pl.BlockSpec((None, POOL_HALO, widths[0]), lambda b, i: (b, jnp.maximum(i * hb - 1, 0), 0)),
            pl.BlockSpec((None, tm, widths[0]), row),
            pl.BlockSpec((None, POOL_HALO, widths[0]),
                         lambda b, i: (b, jnp.minimum((i + 1) * hb, n_hb - 1), 0)),
            pl.BlockSpec((None, tm, widths[1]), row),
            pl.BlockSpec((None, tm, widths[2]), row),
            pl.BlockSpec((None, tm, D), row),
            pl.BlockSpec(wo.shape, const2),
            pl.BlockSpec(pw.shape, lambda b, i: (0, 0, 0)),
            pl.BlockSpec((1, widths[0]), const2),
            pl.BlockSpec((1, D), const2),
        ],
        out_specs=pl.BlockSpec((None, tm, D), row),
        out_shape=jax.ShapeDtypeStruct((B, L, D), F32),
        scratch_shapes=[
            pltpu.VMEM((tm + 2 * POOL_HALO, widths[0]), F32),
            pltpu.VMEM((tm + 2 * POOL_HALO, widths[0]), F32),
            pltpu.VMEM((tm, D), F32),
            pltpu.VMEM((tm, widths[0]), BF16),
        ],
        compiler_params=_params("parallel", "parallel"),
        name="mix_out",
    )(u_pool, u_pool, u_pool, attn, four, x, wo, pw, ps, g)


def _gated_gelu(gate, half_val):
    c0 = math.sqrt(2.0 / math.pi)
    inner = gate * (gate * gate * (c0 * 0.044715) + c0)
    return gate * (1.0 + jnp.tanh(inner)) * half_val


def _ffn_kernel(xp_ref, x_ref, xn_ref, gpre_ref, wg_ref, wv_ref, cp_ref,
                wd_ref, gpost_ref, o_ref, h_ref, ug_ref, uv_ref, *, tm, sub, nc):
    i = pl.program_id(1)
    j = pl.program_id(2)
    rows = tm + 2 * CONV_HALO

    @pl.when(j == 0)
    def _():
        g = gpre_ref[...]
        hp = jnp.where(i > 0, _rms(xp_ref[...], g), 0.0)
        hn = jnp.where(i < pl.num_programs(1) - 1, _rms(xn_ref[...], g), 0.0)
        h_ref[0:CONV_HALO, :] = hp.astype(BF16)
        h_ref[CONV_HALO:CONV_HALO + tm, :] = _rms(x_ref[...], g).astype(BF16)
        h_ref[CONV_HALO + tm:, :] = hn.astype(BF16)

    def conv(u_ref, cw, cb, r0, nr):
        prev = u_ref[CONV_HALO - 1 + r0:CONV_HALO - 1 + r0 + nr, :]
        cur = u_ref[CONV_HALO + r0:CONV_HALO + r0 + nr, :]
        nxt = u_ref[CONV_HALO + 1 + r0:CONV_HALO + 1 + r0 + nr, :]
        return prev * cw[0:1, :] + cur * cw[1:2, :] + nxt * cw[2:3, :] + cb

    def step(first, finish):
        h = h_ref[...]
        n_sub = wg_ref.shape[1] // sub
        nr = tm // FFN_ROW_BLOCKS
        for s in range(n_sub):
            sl = slice(s * sub, (s + 1) * sub)
            ug_ref[s] = jnp.dot(h, wg_ref[:, sl], preferred_element_type=F32)
            uv_ref[s] = jnp.dot(h, wv_ref[:, sl], preferred_element_type=F32)
        for s in range(n_sub):
            sl = slice(s * sub, (s + 1) * sub)
            cwg, cbg = cp_ref[0:3, sl], cp_ref[3:4, sl]
            cwv, cbv = 0.5 * cp_ref[4:7, sl], 0.5 * cp_ref[7:8, sl]
            for r0 in range(0, tm, nr):
                rows = slice(r0, r0 + nr)
                gate = conv(ug_ref.at[s], cwg, cbg, r0, nr)
                half_val = conv(uv_ref.at[s], cwv, cbv, r0, nr)
                act = _gated_gelu(gate, half_val).astype(BF16)
                d = jnp.dot(act, wd_ref[sl, :], preferred_element_type=F32)
                total = d if first and s == 0 else o_ref[rows, :] + d
                if finish and s == n_sub - 1:
                    o_ref[rows, :] = x_ref[rows, :] + _rms(total, gpost_ref[...])
                else:
                    o_ref[rows, :] = total

    last = nc - 1
    if nc == 1:
        step(True, True)
    else:
        pl.when(j == 0)(functools.partial(step, True, False))
        if nc > 2:
            pl.when(jnp.logical_and(j > 0, j < last))(functools.partial(step, False, False))
        pl.when(j == last)(functools.partial(step, False, True))


def _ffn(x, gpre, w_up, conv_w, conv_b, w_down, gpost, layer, tm, chunk):
    B, L, D = x.shape
    d_ff = w_down.shape[1]
    assert L % tm == 0 and tm % (FFN_ROW_BLOCKS * CONV_HALO) == 0
    assert d_ff % chunk == 0 and chunk % FFN_SUB_CHUNK == 0 and w_up.shape[1:] == (D, 2 * d_ff)
    nc = d_ff // chunk
    hb = tm // CONV_HALO
    n_hb = L // CONV_HALO
    const2 = lambda b, i, j: (0, 0)
    taps = conv_w.reshape(3, 2, nc, chunk)
    bias = conv_b.reshape(1, 2, nc, chunk)
    conv_p = jnp.concatenate([taps[:, 0], bias[:, 0], taps[:, 1], bias[:, 1]], axis=0).transpose(1, 0, 2)
    sub = min(FFN_SUB_CHUNK, chunk)
    kern = functools.partial(_ffn_kernel, tm=tm, sub=sub, nc=nc)
    return pl.pallas_call(
        kern,
        grid=(B, L // tm, nc),
        in_specs=[
            pl.BlockSpec((None, CONV_HALO, D), lambda b, i, j: (b, jnp.maximum(i * hb - 1, 0), 0)),
            pl.BlockSpec((None, tm, D), lambda b, i, j: (b, i, 0)),
            pl.BlockSpec((None, CONV_HALO, D),
                         lambda b, i, j: (b, jnp.minimum((i + 1) * hb, n_hb - 1), 0)),
            pl.BlockSpec((1, D), const2),
            pl.BlockSpec((None, D, chunk), lambda b, i, j: (layer, 0, j)),
            pl.BlockSpec((None, D, chunk), lambda b, i, j: (layer, 0, nc + j)),
            pl.BlockSpec((None, 8, chunk), lambda b, i, j: (j, 0, 0)),
            pl.BlockSpec((None, chunk, D), lambda b, i, j: (layer, j, 0)),
            pl.BlockSpec((1, D), const2),
        ],
        out_specs=pl.BlockSpec((None, tm, D), lambda b, i, j: (b, i, 0)),
        out_shape=jax.ShapeDtypeStruct((B, L, D), F32),
        scratch_shapes=[
            pltpu.VMEM((tm + 2 * CONV_HALO, D), BF16),
            pltpu.VMEM((chunk // sub, tm + 2 * CONV_HALO, sub), F32),
            pltpu.VMEM((chunk // sub, tm + 2 * CONV_HALO, sub), F32),
        ],
        compiler_params=_params("parallel", "parallel", "arbitrary"),
        name="ffn",
    )(x, x, x, gpre, w_up, w_up, conv_p, w_down, gpost)


def _rope_tables(seq_len):
    quarter = HEAD_DIM // 4
    t = jnp.arange(seq_len, dtype=jnp.int32)
    row = (t // GRID_W).astype(F32)
    col = (t % GRID_W).astype(F32)
    inv_freq = 1.0 / (ROPE_THETA ** (jnp.arange(quarter, dtype=F32) / quarter))
    ang = jnp.concatenate([row[:, None] * inv_freq[None, :], col[:, None] * inv_freq[None, :]], axis=-1)
    cos, sin = jnp.cos(ang), jnp.sin(ang)
    return jnp.concatenate([cos, cos], axis=-1), jnp.concatenate([-sin, sin], axis=-1)


def _fourier_split(seq_len):
    l2 = 1 << (int(math.log2(seq_len)) // 2)
    return seq_len // l2, l2


def _trunk(x, layers):
    B, L, D = x.shape
    ts = _tiles(L)
    cc, ss = _rope_tables(L)
    l1, l2 = _fourier_split(L)
    for p in layers:
        u_pool, qt, k, vt, u_four = _in_proj(x, p["g_pre_mix"], p["w_in"], cc, ss, p["q_norm"],
                                             p["k_norm"], p["widths"], ts["tm_proj"])
        attn = _attention(qt, k, vt, ts["tq"])
        four = _fourier(u_four, p["fourier_w"], l1, l2)
        x = _mix_out(x, u_pool, attn, four, p["w_out"], p["pool_w"], p["pool_scale"],
                     p["g_post_mix"], ts["tm_mix"])
        x = _ffn(x, p["g_pre_ffn"], p["w_up"], p["conv_w"], p["conv_b"], p["w_down"],
                 p["g_post_ffn"], p["layer"], ts["tm_ffn"], ts["ff_chunk"])
    return x


def kernel(x_prompt, x_sample, g_pre_mix, g_post_mix, w_in, pool_w, pool_scale, q_norm, k_norm,
           fourier_w, w_out, g_pre_ffn, g_post_ffn, w_up, conv_w, conv_b, w_down):
    depth = w_in.shape[0]
    pool_width = pool_scale.shape[-1]
    four_width = fourier_w.shape[1] * fourier_w.shape[2]
    kv_width = N_KV_HEADS * HEAD_DIM
    q_width = w_in.shape[-1] - pool_width - four_width - 2 * kv_width
    w_up_b, w_down_b = w_up.astype(BF16), w_down.astype(BF16)
    layers = []
    for l in range(depth):
        layers.append(dict(
            layer=l, widths=(pool_width, q_width, kv_width, four_width),
            g_pre_mix=g_pre_mix[l][None, :], g_post_mix=g_post_mix[l][None, :],
            w_in=w_in[l].astype(BF16), pool_w=pool_w[l].astype(BF16),
            pool_scale=pool_scale[l][None, :], q_norm=q_norm[l][None, :], k_norm=k_norm[l][None, :],
            fourier_w=fourier_w[l].astype(BF16), w_out=w_out[l].astype(BF16),
            g_pre_ffn=g_pre_ffn[l][None, :], g_post_ffn=g_post_ffn[l][None, :],
            w_up=w_up_b, conv_w=conv_w[l], conv_b=conv_b[l][None, :], w_down=w_down_b))
    return _trunk(x_prompt, layers), _trunk(x_sample, layers)
```

```python
import functools
import math

import jax
import jax.numpy as jnp
import numpy as np
from jax import lax
from jax.experimental import pallas as pl
from jax.experimental.pallas import tpu as pltpu

F32 = jnp.float32
BF16 = jnp.bfloat16

NORM_EPS = 1e-6
GRID_W = 64
HEAD_DIM = 128
N_KV_HEADS = 2
GQA_GROUP = 4
POOL_WINDOWS = (2, 4, 8, 16)
POOL_HALO = 8
GROUP_CH = 128
N_GROUPS = 4
ROPE_THETA = 10000.0
CONV_HALO = 16
ATTN_CHUNKS_PER_TRIP = 4
FOURIER_S2_PER_STEP = 16
FFN_ROW_BLOCKS = 2
FFN_SUB_CHUNK = 256

V7X_VMEM_BYTES = 64 * 1024 * 1024
VMEM_LIMIT = V7X_VMEM_BYTES - 4 * 1024 * 1024


def _tiles(seq_len):
    return dict(
        tm_proj=min(512, seq_len),
        tm_mix=min(512, seq_len),
        tm_ffn=min(1024, seq_len),
        ff_chunk=512,
        tq=min(512, seq_len),
    )


def _params(*sem):
    return pltpu.CompilerParams(dimension_semantics=sem, vmem_limit_bytes=VMEM_LIMIT)


def _rms(x, g):
    return x * lax.rsqrt(jnp.mean(x * x, axis=-1, keepdims=True) + NORM_EPS) * g


def _in_proj_kernel(x_ref, g_ref, w_ref, cc_ref, ss_ref, qn_ref, kn_ref,
                    pool_ref, q_ref, k_ref, v_ref, four_ref, *, widths, scale):
    pool_w, q_w, kv_w, four_w = widths
    h = _rms(x_ref[...], g_ref[...]).astype(BF16)
    cc = cc_ref[...]
    ss = ss_ref[...]

    def seg(lo, width):
        return jnp.dot(h, w_ref[:, lo:lo + width], preferred_element_type=F32)

    def norm_rope(zh, gain, out_scale):
        y = zh * lax.rsqrt(jnp.mean(zh * zh, axis=-1, keepdims=True) + NORM_EPS) * gain
        y = y * cc + pltpu.roll(y, HEAD_DIM // 2, axis=1) * ss
        return y * out_scale

    pair = 2 * HEAD_DIM
    for p in range(q_w // pair):
        z = seg(pool_w + p * pair, pair)
        for s in range(2):
            zh = z[:, s * HEAD_DIM:(s + 1) * HEAD_DIM]
            q_ref[2 * p + s] = norm_rope(zh, qn_ref[...], scale).T.astype(BF16)
    z = seg(pool_w + q_w, kv_w)
    for s in range(kv_w // HEAD_DIM):
        zh = z[:, s * HEAD_DIM:(s + 1) * HEAD_DIM]
        k_ref[:, s * HEAD_DIM:(s + 1) * HEAD_DIM] = norm_rope(zh, kn_ref[...], 1.0).astype(BF16)
    z = seg(pool_w + q_w + kv_w, kv_w)
    for s in range(kv_w // HEAD_DIM):
        v_ref[s] = z[:, s * HEAD_DIM:(s + 1) * HEAD_DIM].T.astype(BF16)
    four_ref[...] = seg(pool_w + q_w + 2 * kv_w, four_w)
    pool_ref[...] = seg(0, pool_w)


def _in_proj(x, g, w, cc, ss, qn, kn, widths, tm):
    B, L, D = x.shape
    pool_w, q_w, kv_w, four_w = widths
    assert L % tm == 0 and w.shape == (D, sum(widths) + kv_w)
    nt = L // tm
    row = lambda b, i: (b, i, 0)
    const2 = lambda b, i: (0, 0)
    n_q = q_w // HEAD_DIM
    v_rows = HEAD_DIM
    kern = functools.partial(_in_proj_kernel, widths=widths, scale=HEAD_DIM ** -0.5 * math.log2(math.e))
    return pl.pallas_call(
        kern,
        grid=(B, nt),
        in_specs=[
            pl.BlockSpec((None, tm, D), row),
            pl.BlockSpec((1, D), const2),
            pl.BlockSpec(w.shape, const2),
            pl.BlockSpec((tm, HEAD_DIM), lambda b, i: (i, 0)),
            pl.BlockSpec((tm, HEAD_DIM), lambda b, i: (i, 0)),
            pl.BlockSpec((1, HEAD_DIM), const2),
            pl.BlockSpec((1, HEAD_DIM), const2),
        ],
        out_specs=[
            pl.BlockSpec((None, tm, pool_w), row),
            pl.BlockSpec((None, n_q, HEAD_DIM, tm), lambda b, i: (b, 0, 0, i)),
            pl.BlockSpec((None, tm, kv_w), row),
            pl.BlockSpec((None, N_KV_HEADS, None, v_rows, tm), lambda b, i: (b, 0, i, 0, 0)),
            pl.BlockSpec((None, tm, four_w), row),
        ],
        out_shape=[
            jax.ShapeDtypeStruct((B, L, pool_w), F32),
            jax.ShapeDtypeStruct((B, n_q, HEAD_DIM, L), BF16),
            jax.ShapeDtypeStruct((B, L, kv_w), BF16),
            jax.ShapeDtypeStruct((B, N_KV_HEADS, nt, v_rows, tm), BF16),
            jax.ShapeDtypeStruct((B, L, four_w), F32),
        ],
        compiler_params=_params("parallel", "parallel"),
        name="in_proj",
    )(x, g, w, cc, ss, qn, kn)


def _attn_kernel(qt_ref, qn_ref, k_ref, vt_ref, o_ref, m_ref, l_ref, acc_ref, sa_ref, sb_ref, ma_ref,
                 mb_ref, *, tk, nk, unroll):
    assert nk == 1 or (unroll % 2 == 0 and nk % unroll == 0)
    i = pl.program_id(2)
    m_ref[...] = jnp.full(m_ref.shape, -jnp.inf, F32)
    l_ref[...] = jnp.zeros(l_ref.shape, F32)
    acc_ref[...] = jnp.zeros(acc_ref.shape, F32)

    def scores(c, g, s_ref, cm_ref, q_ref=qt_ref):
        kc = k_ref[pl.ds(pl.multiple_of(c * tk, tk), tk), :]
        st = jnp.dot(kc, q_ref[g], preferred_element_type=F32)
        s_ref[g] = st
        cm_ref[g] = jnp.max(st, axis=0, keepdims=True)

    def accumulate(c, g, s_ref, cm_ref):
        m_old = m_ref[g]
        m_new = jnp.maximum(m_old, cm_ref[g])
        alpha = jnp.exp2(m_old - m_new)
        p = jnp.exp2(s_ref[g] - m_new)
        l_ref[g] = alpha * l_ref[g] + jnp.sum(p, axis=0, keepdims=True)
        acc_ref[g] = alpha * acc_ref[g] + jnp.dot(vt_ref[c], p.astype(BF16),
                                                  preferred_element_type=F32)
        m_ref[g] = m_new

    def step(c_scores, s_next, m_next, c_acc, s_cur, m_cur):
        for g in range(GQA_GROUP):
            if c_scores is not None:
                scores(c_scores, g, s_next, m_next)
            if c_acc is not None:
                accumulate(c_acc, g, s_cur, m_cur)

    bufs = ((sa_ref, ma_ref), (sb_ref, mb_ref))
    pl.when(i == 0)(functools.partial(step, 0, *bufs[0], None, None, None))
    if nk > 1:
        def group(i, carry):
            for u in range(unroll):
                c = unroll * i + u
                step(c + 1, *bufs[(u + 1) % 2], c, *bufs[u % 2])
            return carry

        lax.fori_loop(0, nk // unroll - 1, group, 0)
        for c in range(nk - unroll, nk - 1):
            step(c + 1, *bufs[(c + 1) % 2], c, *bufs[c % 2])
    def finish(with_next):
        for g in range(GQA_GROUP):
            accumulate(nk - 1, g, *bufs[(nk - 1) % 2])
            if with_next:
                scores(0, g, *bufs[0], q_ref=qn_ref)
        for g in range(GQA_GROUP):
            o = acc_ref[g] / l_ref[g]
            o_ref[:, g * HEAD_DIM:(g + 1) * HEAD_DIM] = o.T.astype(o_ref.dtype)

    last_i = pl.num_programs(2) - 1
    pl.when(i < last_i)(functools.partial(finish, True))
    pl.when(i == last_i)(functools.partial(finish, False))


def _attention(qt, k, vt, tq):
    B, n_q, _, L = qt.shape
    _, _, nk, v_rows, tk = vt.shape
    assert L % tq == 0 and nk * tk == L and n_q == N_KV_HEADS * GQA_GROUP
    gw = GQA_GROUP * HEAD_DIM
    unroll = ATTN_CHUNKS_PER_TRIP if nk % ATTN_CHUNKS_PER_TRIP == 0 and nk >= 4 * ATTN_CHUNKS_PER_TRIP else 2
    kern = functools.partial(_attn_kernel, tk=tk, nk=nk, unroll=unroll)
    return pl.pallas_call(
        kern,
        grid=(B, N_KV_HEADS, L // tq),
        in_specs=[
            pl.BlockSpec((None, GQA_GROUP, HEAD_DIM, tq), lambda b, j, i: (b, j, 0, i)),
            pl.BlockSpec((None, GQA_GROUP, HEAD_DIM, tq),
                         lambda b, j, i: (b, j, 0, jnp.minimum(i + 1, L // tq - 1))),
            pl.BlockSpec((None, L, HEAD_DIM), lambda b, j, i: (b, 0, j)),
            pl.BlockSpec((None, None, nk, v_rows, tk), lambda b, j, i: (b, j, 0, 0, 0)),
        ],
        out_specs=pl.BlockSpec((None, tq, gw), lambda b, j, i: (b, i, j)),
        out_shape=jax.ShapeDtypeStruct((B, L, n_q * HEAD_DIM), BF16),
        scratch_shapes=[
            pltpu.VMEM((GQA_GROUP, 1, tq), F32),
            pltpu.VMEM((GQA_GROUP, 1, tq), F32),
            pltpu.VMEM((GQA_GROUP, v_rows, tq), F32),
            pltpu.VMEM((GQA_GROUP, tk, tq), F32),
            pltpu.VMEM((GQA_GROUP, tk, tq), F32),
            pltpu.VMEM((GQA_GROUP, 1, tq), F32),
            pltpu.VMEM((GQA_GROUP, 1, tq), F32),
        ],
        compiler_params=_params("parallel", "parallel", "arbitrary"),
        name="attention",
    )(qt, qt, k, vt)


def _dft_cos_sin(n):
    idx = np.arange(n)
    ang = 2.0 * np.pi * ((idx[:, None] * idx[None, :]) % n) / n
    return np.cos(ang), np.sin(ang)


def _fourier_a_kernel(x_ref, f1_ref, twc_ref, tws_ref, zr_ref, zi_ref, *, l2, k):
    xt = pltpu.einshape("mkw->kmw", x_ref[...])
    zr, zi = [], []
    for t in range(k):
        z = jnp.dot(f1_ref[...], xt[t].astype(BF16), preferred_element_type=F32)
        a = z[:l2]
        b = z[l2:]
        c = twc_ref[:, t:t + 1]
        s = tws_ref[:, t:t + 1]
        zr.append(a * c + b * s)
        zi.append(b * c - a * s)
    zr_ref[...] = pltpu.einshape("kmw->mkw", jnp.stack(zr)).astype(zr_ref.dtype)
    zi_ref[...] = pltpu.einshape("kmw->mkw", jnp.stack(zi)).astype(zi_ref.dtype)


def _fourier_b_kernel(zr_ref, zi_ref, f2_ref, fc_ref, fw_ref, o_ref, hr_ref, hi_ref, *, l1, n_s2, norm):
    for a in range(n_s2):
        rows = slice(a * l1, (a + 1) * l1)
        zz = jnp.concatenate([zr_ref[rows, :], zi_ref[rows, :]], axis=0)
        hh = jnp.dot(f2_ref[...], zz, preferred_element_type=F32)
        hr_ref[rows, :] = hh[:l1].astype(BF16)
        hi_ref[rows, :] = hh[l1:].astype(BF16)
    ys = []
    for g in range(N_GROUPS):
        sl = slice(g * GROUP_CH, (g + 1) * GROUP_CH)
        hg = jnp.concatenate([hr_ref[:, sl], hi_ref[:, sl]], axis=1)
        f = jnp.dot(hg, fc_ref[...], preferred_element_type=F32) * norm
        ys.append(jnp.dot(f.astype(BF16), fw_ref[g], preferred_element_type=F32))
    y = jnp.concatenate(ys, axis=1).reshape(n_s2, l1, o_ref.shape[-1])
    o_ref[...] = pltpu.einshape("alw->law", y).astype(o_ref.dtype)


def _fourier(u, fw, l1, l2):
    B, L, width = u.shape
    k = min(FOURIER_S2_PER_STEP, l1)
    n_s2 = min(FOURIER_S2_PER_STEP, l2)
    assert l1 * l2 == L and l1 % k == 0 and l2 % n_s2 == 0 and width == N_GROUPS * GROUP_CH
    c2, s2 = _dft_cos_sin(l2)
    f1 = jnp.asarray(np.concatenate([c2, -s2], axis=0), BF16)
    c1, s1 = _dft_cos_sin(l1)
    f2 = jnp.asarray(np.block([[c1, s1], [-s1, c1]]), BF16)
    cc, sc = _dft_cos_sin(GROUP_CH)
    fc = jnp.asarray(np.concatenate([cc, sc], axis=0), BF16)
    ang = 2.0 * np.pi * ((np.arange(l2)[:, None] * np.arange(l1)[None, :]) % L) / L
    tw = lambda t: jnp.asarray(t.reshape(l2, l1 // k, k).transpose(1, 0, 2), F32)
    twc, tws = tw(np.cos(ang)), tw(np.sin(ang))

    x = u.reshape(B, l2, l1, width)
    col = lambda b, j: (b, 0, j, 0)
    const2 = lambda b, j: (0, 0)
    zr, zi = pl.pallas_call(
        functools.partial(_fourier_a_kernel, l2=l2, k=k),
        grid=(B, l1 // k),
        in_specs=[
            pl.BlockSpec((None, l2, k, width), col),
            pl.BlockSpec(f1.shape, const2),
            pl.BlockSpec((None, l2, k), lambda b, j: (j, 0, 0)),
            pl.BlockSpec((None, l2, k), lambda b, j: (j, 0, 0)),
        ],
        out_specs=[pl.BlockSpec((None, l2, k, width), col)] * 2,
        out_shape=[jax.ShapeDtypeStruct((B, l2, l1, width), BF16)] * 2,
        compiler_params=_params("parallel", "parallel"),
        name="fourier_a",
    )(x, f1, twc, tws)

    zr = zr.reshape(B, L, width)
    zi = zi.reshape(B, L, width)
    rows = lambda b, j: (b, j, 0)
    out = pl.pallas_call(
        functools.partial(_fourier_b_kernel, l1=l1, n_s2=n_s2, norm=1.0 / math.sqrt(L * GROUP_CH)),
        grid=(B, l2 // n_s2),
        in_specs=[
            pl.BlockSpec((None, n_s2 * l1, width), rows),
            pl.BlockSpec((None, n_s2 * l1, width), rows),
            pl.BlockSpec(f2.shape, const2),
            pl.BlockSpec(fc.shape, const2),
            pl.BlockSpec(fw.shape, lambda b, j: (0, 0, 0)),
        ],
        out_specs=pl.BlockSpec((None, l1, n_s2, width), lambda b, j: (b, 0, j, 0)),
        out_shape=jax.ShapeDtypeStruct((B, l1, l2, width), BF16),
        scratch_shapes=[pltpu.VMEM((n_s2 * l1, width), BF16)] * 2,
        compiler_params=_params("parallel", "parallel"),
        name="fourier_b",
    )(zr, zi, f2, fc, fw)
    return out.reshape(B, L, width)


def _mix_out_kernel(up_ref, uc_ref, un_ref, attn_ref, four_ref, x_ref, wo_ref, pw_ref, ps_ref,
                    g_ref, o_ref, ext_ref, v_ref, m_ref, pooled_ref, *, tm, seq_len, widths):
    pool_w, attn_w, four_w = widths
    i = pl.program_id(1)
    last = pl.num_programs(1) - 1
    ext_ref[0:POOL_HALO, :] = jnp.where(i > 0, up_ref[...], 0.0)
    ext_ref[POOL_HALO:POOL_HALO + tm, :] = uc_ref[...]
    ext_ref[POOL_HALO + tm:, :] = jnp.where(i < last, un_ref[...], 0.0)

    halves = [slice(r * (tm // 2), (r + 1) * (tm // 2)) for r in range(2)]
    t = i * tm + lax.broadcasted_iota(jnp.int32, (tm, 1), 0)
    for g in range(len(POOL_WINDOWS)):
        sl = slice(g * GROUP_CH, (g + 1) * GROUP_CH)
        ug = ext_ref[:, sl]
        hi = ug.astype(BF16)
        lo = (ug - hi.astype(F32)).astype(BF16)
        pw2 = jnp.concatenate([pw_ref[g], pw_ref[g]], axis=0)
        v_ref[:, sl] = jnp.dot(jnp.concatenate([hi, lo], axis=1), pw2, preferred_element_type=F32)

    for rows in halves:
        m_ref[rows, :] = (
            jnp.dot(attn_ref[rows, :], wo_ref[pool_w:pool_w + attn_w, :], preferred_element_type=F32)
            + jnp.dot(four_ref[rows, :], wo_ref[pool_w + attn_w:, :],
                      preferred_element_type=F32))

    for g, w in enumerate(POOL_WINDOWS):
        sl = slice(g * GROUP_CH, (g + 1) * GROUP_CH)
        lo = jnp.maximum(t - w // 2, 0)
        hi = jnp.minimum(t + (w - 1 - w // 2), seq_len - 1)
        cnt = (hi - lo + 1).astype(F32)
        win = v_ref[POOL_HALO - w // 2:POOL_HALO - w // 2 + tm, sl]
        for d in range(1 - w // 2, w - w // 2):
            win = win + v_ref[POOL_HALO + d:POOL_HALO + d + tm, sl]
        yg = (win / cnt - v_ref[POOL_HALO:POOL_HALO + tm, sl]) * ps_ref[:, sl]
        pooled_ref[:, sl] = yg.astype(BF16)
    for rows in halves:
        m = m_ref[rows, :] + jnp.dot(pooled_ref[rows, :], wo_ref[0:pool_w, :],
                                     preferred_element_type=F32)
        o_ref[rows, :] = x_ref[rows, :] + _rms(m, g_ref[...])


def _mix_out(x, u_pool, attn, four, wo, pw, ps, g, tm):
    B, L, D = x.shape
    widths = (u_pool.shape[-1], attn.shape[-1], four.shape[-1])
    assert L % tm == 0 and tm % (2 * POOL_HALO) == 0 and wo.shape == (sum(widths), D)
    assert widths[0] == len(POOL_WINDOWS) * GROUP_CH
    hb = tm // POOL_HALO
    n_hb = L // POOL_HALO
    row = lambda b, i: (b, i, 0)
    const2 = lambda b, i: (0, 0)
    kern = functools.partial(_mix_out_kernel, tm=tm, seq_len=L, widths=widths)
    return pl.pallas_call(
        kern,
        grid=(B, L // tm),
        in_specs=[
            pl.BlockSpec((None, POOL_HALO, widths[0]), lambda b, i: (b, jnp.maximum(i * hb - 1, 0), 0)),
            pl.BlockSpec((None, tm, widths[0]), row),
            pl.BlockSpec((None, POOL_HALO, widths[0]),
                         lambda b, i: (b, jnp.minimum((i + 1) * hb, n_hb - 1), 0)),
            pl.BlockSpec((None, tm, widths[1]), row),
            pl.BlockSpec((None, tm, widths[2]), row),
            pl.BlockSpec((None, tm, D), row),
            pl.BlockSpec(wo.shape, const2),
            pl.BlockSpec(pw.shape, lambda b, i: (0, 0, 0)),
            pl.BlockSpec((1, widths[0]), const2),
            pl.BlockSpec((1, D), const2),
        ],
        out_specs=pl.BlockSpec((None, tm, D), row),
        out_shape=jax.ShapeDtypeStruct((B, L, D), F32),
        scratch_shapes=[
            pltpu.VMEM((tm + 2 * POOL_HALO, widths[0]), F32),
            pltpu.VMEM((tm + 2 * POOL_HALO, widths[0]), F32),
            pltpu.VMEM((tm, D), F32),
            pltpu.VMEM((tm, widths[0]), BF16),
        ],
        compiler_params=_params("parallel", "parallel"),
        name="mix_out",
    )(u_pool, u_pool, u_pool, attn, four, x, wo, pw, ps, g)


def _gated_gelu(gate, half_val):
    c0 = math.sqrt(2.0 / math.pi)
    inner = gate * (gate * gate * (c0 * 0.044715) + c0)
    return gate * (1.0 + jnp.tanh(inner)) * half_val


def _ffn_kernel(xp_ref, x_ref, xn_ref, gpre_ref, wg_ref, wv_ref, cp_ref,
                wd_ref, gpost_ref, o_ref, h_ref, ug_ref, uv_ref, *, tm, sub, nc):
    i = pl.program_id(1)
    j = pl.program_id(2)
    rows = tm + 2 * CONV_HALO

    @pl.when(j == 0)
    def _():
        g = gpre_ref[...]
        hp = jnp.where(i > 0, _rms(xp_ref[...], g), 0.0)
        hn = jnp.where(i < pl.num_programs(1) - 1, _rms(xn_ref[...], g), 0.0)
        h_ref[0:CONV_HALO, :] = hp.astype(BF16)
        h_ref[CONV_HALO:CONV_HALO + tm, :] = _rms(x_ref[...], g).astype(BF16)
        h_ref[CONV_HALO + tm:, :] = hn.astype(BF16)

    def conv(u_ref, cw, cb, r0, nr):
        prev = u_ref[CONV_HALO - 1 + r0:CONV_HALO - 1 + r0 + nr, :]
        cur = u_ref[CONV_HALO + r0:CONV_HALO + r0 + nr, :]
        nxt = u_ref[CONV_HALO + 1 + r0:CONV_HALO + 1 + r0 + nr, :]
        return prev * cw[0:1, :] + cur * cw[1:2, :] + nxt * cw[2:3, :] + cb

    def step(first, finish):
        h = h_ref[...]
        n_sub = wg_ref.shape[1] // sub
        nr = tm // FFN_ROW_BLOCKS
        for s in range(n_sub):
            sl = slice(s * sub, (s + 1) * sub)
            ug_ref[s] = jnp.dot(h, wg_ref[:, sl], preferred_element_type=F32)
            uv_ref[s] = jnp.dot(h, wv_ref[:, sl], preferred_element_type=F32)
        for r0 in range(0, tm, nr):
            for s in range(n_sub):
                sl = slice(s * sub, (s + 1) * sub)
                cwg, cbg = cp_ref[0:3, sl], cp_ref[3:4, sl]
                cwv, cbv = 0.5 * cp_ref[4:7, sl], 0.5 * cp_ref[7:8, sl]
                rows = slice(r0, r0 + nr)
                gate = conv(ug_ref.at[s], cwg, cbg, r0, nr)
                half_val = conv(uv_ref.at[s], cwv, cbv, r0, nr)
                act = _gated_gelu(gate, half_val).astype(BF16)
                d = jnp.dot(act, wd_ref[sl, :], preferred_element_type=F32)
                total = d if first and s == 0 else o_ref[rows, :] + d
                if finish and s == n_sub - 1:
                    o_ref[rows, :] = x_ref[rows, :] + _rms(total, gpost_ref[...])
                else:
                    o_ref[rows, :] = total

    last = nc - 1
    if nc == 1:
        step(True, True)
    else:
        pl.when(j == 0)(functools.partial(step, True, False))
        if nc > 2:
            pl.when(jnp.logical_and(j > 0, j < last))(functools.partial(step, False, False))
        pl.when(j == last)(functools.partial(step, False, True))


def _ffn(x, gpre, w_up, conv_w, conv_b, w_down, gpost, layer, tm, chunk):
    B, L, D = x.shape
    d_ff = w_down.shape[1]
    assert L % tm == 0 and tm % (FFN_ROW_BLOCKS * CONV_HALO) == 0
    assert d_ff % chunk == 0 and chunk % FFN_SUB_CHUNK == 0 and w_up.shape[1:] == (D, 2 * d_ff)
    nc = d_ff // chunk
    hb = tm // CONV_HALO
    n_hb = L // CONV_HALO
    const2 = lambda b, i, j: (0, 0)
    taps = conv_w.reshape(3, 2, nc, chunk)
    bias = conv_b.reshape(1, 2, nc, chunk)
    conv_p = jnp.concatenate([taps[:, 0], bias[:, 0], taps[:, 1], bias[:, 1]], axis=0).transpose(1, 0, 2)
    sub = min(FFN_SUB_CHUNK, chunk)
    kern = functools.partial(_ffn_kernel, tm=tm, sub=sub, nc=nc)
    return pl.pallas_call(
        kern,
        grid=(B, L // tm, nc),
        in_specs=[
            pl.BlockSpec((None, CONV_HALO, D), lambda b, i, j: (b, jnp.maximum(i * hb - 1, 0), 0)),
            pl.BlockSpec((None, tm, D), lambda b, i, j: (b, i, 0)),
            pl.BlockSpec((None, CONV_HALO, D),
                         lambda b, i, j: (b, jnp.minimum((i + 1) * hb, n_hb - 1), 0)),
            pl.BlockSpec((1, D), const2),
            pl.BlockSpec((None, D, chunk), lambda b, i, j: (layer, 0, j)),
            pl.BlockSpec((None, D, chunk), lambda b, i, j: (layer, 0, nc + j)),
            pl.BlockSpec((None, 8, chunk), lambda b, i, j: (j, 0, 0)),
            pl.BlockSpec((None, chunk, D), lambda b, i, j: (layer, j, 0)),
            pl.BlockSpec((1, D), const2),
        ],
        out_specs=pl.BlockSpec((None, tm, D), lambda b, i, j: (b, i, 0)),
        out_shape=jax.ShapeDtypeStruct((B, L, D), F32),
        scratch_shapes=[
            pltpu.VMEM((tm + 2 * CONV_HALO, D), BF16),
            pltpu.VMEM((chunk // sub, tm + 2 * CONV_HALO, sub), F32),
            pltpu.VMEM((chunk // sub, tm + 2 * CONV_HALO, sub), F32),
        ],
        compiler_params=_params("parallel", "parallel", "arbitrary"),
        name="ffn",
    )(x, x, x, gpre, w_up, w_up, conv_p, w_down, gpost)


def _rope_tables(seq_len):
    quarter = HEAD_DIM // 4
    t = jnp.arange(seq_len, dtype=jnp.int32)
    row = (t // GRID_W).astype(F32)
    col = (t % GRID_W).astype(F32)
    inv_freq = 1.0 / (ROPE_THETA ** (jnp.arange(quarter, dtype=F32) / quarter))
    ang = jnp.concatenate([row[:, None] * inv_freq[None, :], col[:, None] * inv_freq[None, :]], axis=-1)
    cos, sin = jnp.cos(ang), jnp.sin(ang)
    return jnp.concatenate([cos, cos], axis=-1), jnp.concatenate([-sin, sin], axis=-1)


def _fourier_split(seq_len):
    l2 = 1 << (int(math.log2(seq_len)) // 2)
    return seq_len // l2, l2


def _trunk(x, layers):
    B, L, D = x.shape
    ts = _tiles(L)
    cc, ss = _rope_tables(L)
    l1, l2 = _fourier_split(L)
    for p in layers:
        u_pool, qt, k, vt, u_four = _in_proj(x, p["g_pre_mix"], p["w_in"], cc, ss, p["q_norm"],
                                             p["k_norm"], p["widths"], ts["tm_proj"])
        attn = _attention(qt, k, vt, ts["tq"])
        four = _fourier(u_four, p["fourier_w"], l1, l2)
        x = _mix_out(x, u_pool, attn, four, p["w_out"], p["pool_w"], p["pool_scale"],
                     p["g_post_mix"], ts["tm_mix"])
        x = _ffn(x, p["g_pre_ffn"], p["w_up"], p["conv_w"], p["conv_b"], p["w_down"],
                 p["g_post_ffn"], p["layer"], ts["tm_ffn"], ts["ff_chunk"])
    return x


def kernel(x_prompt, x_sample, g_pre_mix, g_post_mix, w_in, pool_w, pool_scale, q_norm, k_norm,
           fourier_w, w_out, g_pre_ffn, g_post_ffn, w_up, conv_w, conv_b, w_down):
    depth = w_in.shape[0]
    pool_width = pool_scale.shape[-1]
    four_width = fourier_w.shape[1] * fourier_w.shape[2]
    kv_width = N_KV_HEADS * HEAD_DIM
    q_width = w_in.shape[-1] - pool_width - four_width - 2 * kv_width
    w_up_b, w_down_b = w_up.astype(BF16), w_down.astype(BF16)
    layers = []
    for l in range(depth):
        layers.append(dict(
            layer=l, widths=(pool_width, q_width, kv_width, four_width),
            g_pre_mix=g_pre_mix[l][None, :], g_post_mix=g_post_mix[l][None, :],
            w_in=w_in[l].astype(BF16), pool_w=pool_w[l].astype(BF16),
            pool_scale=pool_scale[l][None, :], q_norm=q_norm[l][None, :], k_norm=k_norm[l][None, :],
            fourier_w=fourier_w[l].astype(BF16), w_out=w_out[l].astype(BF16),
            g_pre_ffn=g_pre_ffn[l][None, :], g_post_ffn=g_post_ffn[l][None, :],
            w_up=w_up_b, conv_w=conv_w[l], conv_b=conv_b[l][None, :], w_down=w_down_b))
    return _trunk(x_prompt, layers), _trunk(x_sample, layers)
```
